```python
import jax, jax.numpy as jnp
from jax import lax
import numpy as np

D_MODEL = 4096
BATCH = 4
SEQ = 2048
DEPTH = 2

N_META = 16
N_MIXERS = 2
N_RWKV = (DEPTH + N_MIXERS - 1) // N_MIXERS
N_GLA = DEPTH // N_MIXERS
NORM_EPS = 1e-6

RW_HEAD = 64
RW_HEADS = D_MODEL // RW_HEAD
RW_DECAY_LORA = max(32, int(round(D_MODEL ** 0.5 * 1.8 / 32)) * 32)
RW_AAA_LORA = max(32, int(round(D_MODEL ** 0.5 * 1.8 / 32)) * 32)
RW_GATE_LORA = max(32, int(round(D_MODEL ** 0.8 * 0.6 / 32)) * 32)
RW_LNX_EPS = 64e-5
N_SHIFT_MIX = 6

GLA_HEADS = max(4, D_MODEL // 512)
GLA_DK = D_MODEL // 2
GLA_DV = D_MODEL
GLA_HEAD_K = GLA_DK // GLA_HEADS
GLA_HEAD_V = GLA_DV // GLA_HEADS
GLA_GATE_LORA = 16
GLA_GATE_TAU = 16.0
GLA_CHUNK = 64
GLA_PAD = GLA_CHUNK - N_META
GLA_IN = 2 * GLA_DK + 2 * GLA_DV

D_FF = int(round(D_MODEL * 8 / 3 / 256)) * 256
CONV_W = 3

kernel_name = "hybrid_rwkv7_gla_convffn_sandwich_meta"


def _rmsnorm(x, g):
    x32 = x.astype(jnp.float32)
    y = x32 * lax.rsqrt(jnp.mean(x32 * x32, axis=-1, keepdims=True) + NORM_EPS)
    return (y * g.astype(jnp.float32)).astype(x.dtype)


def _rwkv7_mix(h, mu, w0, w1, w2, a0, a1, a2, g1, g2, k_k, k_a, r_k, w_r, w_k, w_v, w_o, lnx_g, lnx_b):
    B, L, D = h.shape
    xx = jnp.pad(h, ((0, 0), (1, 0), (0, 0)))[:, :-1] - h
    xr, xw, xk, xv, xa, xg = [h + xx * mu[i] for i in range(N_SHIFT_MIX)]
    r = xr @ w_r
    k = xk @ w_k
    v = xv @ w_v
    w = -jax.nn.softplus(-(w0 + jnp.tanh(xw @ w1) @ w2).astype(jnp.float32)) - 0.5
    a = jax.nn.sigmoid((a0 + (xa @ a1) @ a2).astype(jnp.float32))
    g = jax.nn.sigmoid(xg @ g1) @ g2

    heads = lambda t: t.astype(jnp.float32).reshape(B, L, RW_HEADS, RW_HEAD)
    kk = heads(k * k_k)
    kk = kk / jnp.maximum(jnp.sqrt(jnp.sum(kk * kk, axis=-1, keepdims=True)), 1e-12)
    k_h = heads(k.astype(jnp.float32) * (1.0 + (a - 1.0) * k_a.astype(jnp.float32)))
    r_h, v_h, a_h = heads(r), heads(v), heads(a)
    decay = jnp.exp(-jnp.exp(heads(w)))

    def step(S, inp):
        r_t, d_t, k_t, v_t, ra_t, rb_t = inp
        sa = jnp.einsum('bhij,bhj->bhi', S, ra_t)
        S = S * d_t[..., None, :] + sa[..., :, None] * rb_t[..., None, :] + v_t[..., :, None] * k_t[..., None, :]
        return S, jnp.einsum('bhij,bhj->bhi', S, r_t)

    xs = tuple(jnp.moveaxis(t, 1, 0) for t in (r_h, decay, k_h, v_h, -kk, kk * a_h))
    S0 = jnp.zeros((B, RW_HEADS, RW_HEAD, RW_HEAD), jnp.float32)
    _, y = lax.scan(step, S0, xs)
    y = jnp.moveaxis(y, 0, 1)
    mean = jnp.mean(y, axis=-1, keepdims=True)
    var = jnp.mean(jnp.square(y - mean), axis=-1, keepdims=True)
    y = ((y - mean) * lax.rsqrt(var + RW_LNX_EPS)).reshape(B, L, D)
    y = y * lnx_g.astype(jnp.float32) + lnx_b.astype(jnp.float32)
    bonus = jnp.sum(r_h * k_h * r_k.astype(jnp.float32), axis=-1, keepdims=True) * v_h
    y = y + bonus.reshape(B, L, D)
    return (y * g.astype(jnp.float32)).astype(h.dtype) @ w_o


def _to_chunks(t, head_dim):
    B, L = t.shape[:2]
    t = t.astype(jnp.float32).reshape(B, L, GLA_HEADS, head_dim)
    t = jnp.pad(t, ((0, 0), (GLA_PAD, 0), (0, 0), (0, 0)))
    n = (L + GLA_PAD) // GLA_CHUNK
    return t.reshape(B, n, GLA_CHUNK, GLA_HEADS, head_dim).transpose(1, 0, 3, 2, 4)


def _gla_mix(h, w_in, a1, a2, a_b, r_b, norm_g, w_o):
    B, L, D = h.shape
    proj = h @ w_in
    q, k, v, r = jnp.split(proj, [GLA_DK, 2 * GLA_DK, 2 * GLA_DK + GLA_DV], axis=-1)
    log_a = jax.nn.log_sigmoid(((h @ a1) @ a2 + a_b).astype(jnp.float32)) / GLA_GATE_TAU
    qc = _to_chunks(q.astype(jnp.float32) * (GLA_HEAD_K ** -0.5), GLA_HEAD_K)
    kc = _to_chunks(k, GLA_HEAD_K)
    vc = _to_chunks(v, GLA_HEAD_V)
    gc = _to_chunks(log_a, GLA_HEAD_K)
    causal = jnp.tril(jnp.ones((GLA_CHUNK, GLA_CHUNK), dtype=bool))

    def chunk_step(S, inp):
        q_c, k_c, v_c, g_c = inp
        b = jnp.cumsum(g_c, axis=2)
        o_inter = jnp.einsum('bhtd,bhdv->bhtv', q_c * jnp.exp(b), S)
        diff = b[:, :, :, None, :] - b[:, :, None, :, :]
        dec = jnp.exp(jnp.where(causal[:, :, None], diff, -jnp.inf))
        A = jnp.einsum('bhtd,bhsd,bhtsd->bhts', q_c, k_c, dec)
        o_intra = jnp.einsum('bhts,bhsv->bhtv', A, v_c)
        b_last = b[:, :, -1:, :]
        S = S * jnp.exp(b_last[:, :, 0, :, None]) + jnp.einsum('bhsd,bhsv->bhdv', k_c * jnp.exp(b_last - b), v_c)
        return S, o_inter + o_intra

    S0 = jnp.zeros((B, GLA_HEADS, GLA_HEAD_K, GLA_HEAD_V), jnp.float32)
    _, o = lax.scan(chunk_step, S0, (qc, kc, vc, gc))
    n = o.shape[0]
    o = o.transpose(1, 0, 3, 2, 4).reshape(B, n * GLA_CHUNK, GLA_HEADS, GLA_HEAD_V)[:, GLA_PAD:]
    o = o * lax.rsqrt(jnp.mean(o * o, axis=-1, keepdims=True) + NORM_EPS) * norm_g.astype(jnp.float32)
    o = o.reshape(B, L, GLA_DV) * jax.nn.silu((r + r_b).astype(jnp.float32))
    return o.astype(h.dtype) @ w_o


def _conv_ffn(h, w_up, w_gate, conv_w, conv_b, w_down):
    L = h.shape[1]
    u = h @ w_up
    z = h @ w_gate
    zp = jnp.pad(z, ((0, 0), (CONV_W - 1, 0), (0, 0)))
    zc = sum(zp[:, i:i + L] * conv_w[i] for i in range(CONV_W)) + conv_b
    return (jax.nn.silu(zc) * u) @ w_down


def setup_inputs(seed: int = 0) -> dict:
    key = jax.random.key(seed)
    ks = iter(jax.random.split(key, 40))
    nrm = lambda shape, scale: scale * jax.random.normal(next(ks), shape, jnp.float32)
    D, F = D_MODEL, D_FF
    return {
        "x": nrm((BATCH, SEQ, D), 1.0),
        "meta": nrm((N_META, D), 1.0),
        "norm_g": 1.0 + nrm((DEPTH, 4, D), 0.02),
        "rw_mu": jax.random.uniform(next(ks), (N_RWKV, N_SHIFT_MIX, D), jnp.float32),
        "rw_w0": 0.5 + nrm((N_RWKV, D), 0.5),
        "rw_w1": nrm((N_RWKV, D, RW_DECAY_LORA), D ** -0.5),
        "rw_w2": nrm((N_RWKV, RW_DECAY_LORA, D), 0.5 * RW_DECAY_LORA ** -0.5),
        "rw_a0": nrm((N_RWKV, D), 0.3),
        "rw_a1": nrm((N_RWKV, D, RW_AAA_LORA), D ** -0.5),
        "rw_a2": nrm((N_RWKV, RW_AAA_LORA, D), 0.5 * RW_AAA_LORA ** -0.5),
        "rw_g1": nrm((N_RWKV, D, RW_GATE_LORA), D ** -0.5),
        "rw_g2": nrm((N_RWKV, RW_GATE_LORA, D), RW_GATE_LORA ** -0.5),
        "rw_k_k": 0.85 + nrm((N_RWKV, D), 0.05),
        "rw_k_a": 1.0 + nrm((N_RWKV, D), 0.05),
        "rw_r_k": nrm((N_RWKV, RW_HEADS, RW_HEAD), 0.1),
        "rw_wr": nrm((N_RWKV, D, D), D ** -0.5),
        "rw_wk": nrm((N_RWKV, D, D), D ** -0.5),
        "rw_wv": nrm((N_RWKV, D, D), D ** -0.5),
        "rw_wo": nrm((N_RWKV, D, D), D ** -0.5),
        "rw_lnx_g": 1.0 + nrm((N_RWKV, D), 0.02),
        "rw_lnx_b": nrm((N_RWKV, D), 0.01),
        "gla_w_in": nrm((N_GLA, D, GLA_IN), D ** -0.5),
        "gla_a1": nrm((N_GLA, D, GLA_GATE_LORA), D ** -0.5),
        "gla_a2": nrm((N_GLA, GLA_GATE_LORA, GLA_DK), GLA_GATE_LORA ** -0.5),
        "gla_a_b": 1.0 + nrm((N_GLA, GLA_DK), 0.5),
        "gla_r_b": nrm((N_GLA, GLA_DV), 0.01),
        "gla_norm_g": 1.0 + nrm((N_GLA, GLA_HEAD_V), 0.02),
        "gla_wo": nrm((N_GLA, GLA_DV, D), GLA_DV ** -0.5),
        "ffn_up": nrm((DEPTH, D, F), D ** -0.5),
        "ffn_gate": nrm((DEPTH, D, F), D ** -0.5),
        "ffn_conv": nrm((DEPTH, CONV_W, F), CONV_W ** -0.5),
        "ffn_conv_b": nrm((DEPTH, F), 0.01),
        "ffn_down": nrm((DEPTH, F, D), F ** -0.5),
    }


def reference(x, meta, norm_g, rw_mu, rw_w0, rw_w1, rw_w2, rw_a0, rw_a1, rw_a2, rw_g1, rw_g2,
              rw_k_k, rw_k_a, rw_r_k, rw_wr, rw_wk, rw_wv, rw_wo, rw_lnx_g, rw_lnx_b,
              gla_w_in, gla_a1, gla_a2, gla_a_b, gla_r_b, gla_norm_g, gla_wo,
              ffn_up, ffn_gate, ffn_conv, ffn_conv_b, ffn_down):
    B = x.shape[0]
    m = jnp.broadcast_to(meta.astype(x.dtype)[None], (B, N_META, D_MODEL))
    h = jnp.concatenate([m, x], axis=1)
    for i in range(DEPTH):
        j = i // N_MIXERS
        pre = _rmsnorm(h, norm_g[i, 0])
        if i % N_MIXERS == 0:
            mix = _rwkv7_mix(pre, rw_mu[j], rw_w0[j], rw_w1[j], rw_w2[j], rw_a0[j], rw_a1[j], rw_a2[j],
                             rw_g1[j], rw_g2[j], rw_k_k[j], rw_k_a[j], rw_r_k[j], rw_wr[j], rw_wk[j],
                             rw_wv[j], rw_wo[j], rw_lnx_g[j], rw_lnx_b[j])
        else:
            mix = _gla_mix(pre, gla_w_in[j], gla_a1[j], gla_a2[j], gla_a_b[j], gla_r_b[j],
                           gla_norm_g[j], gla_wo[j])
        h = h + _rmsnorm(mix, norm_g[i, 1])
        f = _conv_ffn(_rmsnorm(h, norm_g[i, 2]), ffn_up[i], ffn_gate[i], ffn_conv[i], ffn_conv_b[i], ffn_down[i])
        h = h + _rmsnorm(f, norm_g[i, 3])
    return h[:, N_META:]
```

```python
import functools

import jax
import jax.numpy as jnp
from jax import lax
from jax.experimental import pallas as pl
from jax.experimental.pallas import tpu as pltpu

F32 = jnp.float32
BF16 = jnp.bfloat16

NORM_EPS = 1e-6
RW_LNX_EPS = 64e-5
RW_HEAD = 64
GLA_GATE_TAU = 16.0
CHUNK = 64
LANES = 128
GROUP = 256
HEADS_PER_GROUP = GROUP // RW_HEAD
VMEM_LIMIT = 56 * 1024 * 1024


def _cparams(*sem):
    return pltpu.CompilerParams(dimension_semantics=sem, vmem_limit_bytes=VMEM_LIMIT)


def _pick(n, cands):
    for c in cands:
        if n % c == 0:
            return c
    raise ValueError(f"no tile for {n} in {cands}")


def _rms(x):
    return x * lax.rsqrt(jnp.mean(x * x, axis=-1, keepdims=True) + NORM_EPS)


def _softplus(z):
    return jnp.maximum(z, 0.0) + jnp.log(1.0 + jnp.exp(-jnp.abs(z)))


def _sigmoid(z):
    return 1.0 / (1.0 + jnp.exp(-z))


def _dot(a, b):
    return jnp.dot(a, b, preferred_element_type=F32)


def _dot_nt(a, b):
    return lax.dot_general(a, b, (((1,), (1,)), ((), ())), preferred_element_type=F32)


def _dot_tn(a, b):
    return lax.dot_general(a, b, (((0,), (0,)), ((), ())), preferred_element_type=F32)


def _split(x):
    hi = x.astype(BF16)
    lo = (x - hi.astype(F32)).astype(BF16)
    return hi, lo


def _mm_kernel(x_ref, w_ref, *rest, epilogue, n_extra, group_out):
    extra = rest[:n_extra]
    o_ref = rest[n_extra]
    acc = _dot(x_ref[...], w_ref[...])
    if epilogue is not None:
        acc = epilogue(acc, *[e[...] for e in extra])
    if group_out:
        for g in range(o_ref.shape[0]):
            o_ref[g] = acc[:, g * GROUP:(g + 1) * GROUP].astype(o_ref.dtype)
    else:
        o_ref[...] = acc.astype(o_ref.dtype)


def _matmul(x, w, *, tm, tn, out_dtype=F32, epilogue=None, extras=(), group_out=False, name):
    t, k = x.shape
    n = w.shape[1]
    grid = (t // tm, n // tn)
    in_specs = [pl.BlockSpec((tm, k), lambda i, j: (i, 0)),
                pl.BlockSpec((k, tn), lambda i, j: (0, j))]
    in_specs += [pl.BlockSpec((1, tn), lambda i, j: (0, j)) for _ in extras]
    if group_out:
        gpt = tn // GROUP
        out_shape = jax.ShapeDtypeStruct((n // GROUP, t, GROUP), out_dtype)
        out_spec = pl.BlockSpec((gpt, tm, GROUP), lambda i, j: (j, i, 0))
    else:
        out_shape = jax.ShapeDtypeStruct((t, n), out_dtype)
        out_spec = pl.BlockSpec((tm, tn), lambda i, j: (i, j))
    return pl.pallas_call(
        functools.partial(_mm_kernel, epilogue=epilogue, n_extra=len(extras), group_out=group_out),
        grid=grid, in_specs=in_specs, out_specs=out_spec, out_shape=out_shape,
        compiler_params=_cparams("parallel", "arbitrary"), name=name,
    )(x, w, *extras)


def _lora_kernel(x_ref, w1_ref, w2_ref, b_ref, o_ref, *, act1, act2, group_out):
    t = _dot(x_ref[...], w1_ref[...])
    if act1 is not None:
        t = act1(t)
    y = _dot(t.astype(BF16), w2_ref[...]) + b_ref[...]
    if act2 is not None:
        y = act2(y)
    if group_out:
        for g in range(o_ref.shape[0]):
            o_ref[g] = y[:, g * GROUP:(g + 1) * GROUP]
    else:
        o_ref[...] = y


def _pad_rank(w1, w2):
    r = w1.shape[1]
    rp = -(-r // LANES) * LANES
    return (jnp.pad(w1, ((0, 0), (0, rp - r))).astype(BF16),
            jnp.pad(w2, ((0, rp - r), (0, 0))).astype(BF16))


def _lora(x, w1, w2, bias, *, act1, act2, tm, group_out, name):
    t, k = x.shape
    w1, w2 = _pad_rank(w1, w2)
    rp = w1.shape[1]
    n = w2.shape[1]
    if group_out:
        out_shape = jax.ShapeDtypeStruct((n // GROUP, t, GROUP), F32)
        out_spec = pl.BlockSpec((n // GROUP, tm, GROUP), lambda i: (0, i, 0))
    else:
        out_shape = jax.ShapeDtypeStruct((t, n), F32)
        out_spec = pl.BlockSpec((tm, n), lambda i: (i, 0))
    return pl.pallas_call(
        functools.partial(_lora_kernel, act1=act1, act2=act2, group_out=group_out),
        grid=(t // tm,),
        in_specs=[pl.BlockSpec((tm, k), lambda i: (i, 0)),
                  pl.BlockSpec((k, rp), lambda i: (0, 0)),
                  pl.BlockSpec((rp, n), lambda i: (0, 0)),
                  pl.BlockSpec((1, n), lambda i: (0, 0))],
        out_specs=out_spec, out_shape=out_shape,
        compiler_params=_cparams("parallel"), name=name,
    )(x, w1, w2, bias.reshape(1, n).astype(F32))


def _shift_mix_kernel(h_ref, halo_ref, g_ref, mu_ref, *out_refs, blocks_per_seq):
    i = pl.program_id(0)
    g = g_ref[...]
    pre = _rms(h_ref[...]) * g
    halo = (_rms(halo_ref[...]) * g)[7:8, :]
    halo = jnp.where(i % blocks_per_seq == 0, 0.0, halo)
    row = lax.broadcasted_iota(jnp.int32, pre.shape, 0)
    prev = jnp.where(row == 0, halo, pltpu.roll(pre, 1, 0))
    xx = prev - pre
    for m, o_ref in enumerate(out_refs):
        o_ref[...] = (pre + xx * mu_ref[m:m + 1, :]).astype(BF16)


def _shift_mix(h, g, mu, *, lp, tm):
    t, d = h.shape
    nmix = mu.shape[0]
    row_spec = pl.BlockSpec((tm, d), lambda i: (i, 0))
    return pl.pallas_call(
        functools.partial(_shift_mix_kernel, blocks_per_seq=lp // tm),
        grid=(t // tm,),
        in_specs=[row_spec,
                  pl.BlockSpec((8, d), lambda i: (jnp.maximum(i * (tm // 8) - 1, 0), 0)),
                  pl.BlockSpec((1, d), lambda i: (0, 0)),
                  pl.BlockSpec((nmix, d), lambda i: (0, 0))],
        out_specs=[row_spec] * nmix,
        out_shape=[jax.ShapeDtypeStruct((t, d), BF16)] * nmix,
        compiler_params=_cparams("parallel"), name="rw_shift_mix",
    )(h, h, g.reshape(1, d), mu)


def _rwkv_kernel(r_ref, k_ref, v_ref, w_ref, a_ref, kk_ref, ka_ref, y_ref, s_ref, *, groups):
    c_id = pl.program_id(2)

    @pl.when(c_id == 0)
    def _():
        s_ref[...] = jnp.zeros_like(s_ref)

    C, W = CHUNK, GROUP
    row = lax.broadcasted_iota(jnp.int32, (C, W), 0)
    lane = lax.broadcasted_iota(jnp.int32, (C, W), 1)
    lane_in = lane & (RW_HEAD - 1)
    lane_head = lane >> 6
    tril_incl = lane_in <= row
    tril_strict = lane_in < row
    eye = lane_in == row
    bd_mask = (lax.broadcasted_iota(jnp.int32, (HEADS_PER_GROUP * C, W), 0) >> 6) == \
              (lax.broadcasted_iota(jnp.int32, (HEADS_PER_GROUP * C, W), 1) >> 6)
    ones_bd = jnp.where(bd_mask, 1.0, 0.0).astype(BF16)
    ltri = jnp.where(lax.broadcasted_iota(jnp.int32, (C, C), 1) <=
                     lax.broadcasted_iota(jnp.int32, (C, C), 0), 1.0, 0.0).astype(BF16)

    def bd(x):
        return jnp.where(bd_mask, jnp.concatenate([x] * HEADS_PER_GROUP, axis=0), jnp.zeros((), x.dtype))

    def seg_sum(x):
        hi, lo = _split(x)
        return _dot(hi, ones_bd) + _dot(lo, ones_bd)

    def head_mm(x, y):
        xh, xl = _split(x)
        yh, yl = _split(y)
        byh = bd(yh)
        return _dot(xh, byh) + _dot(xl, byh) + _dot(xh, bd(yl))

    for g in range(groups):
        r = r_ref[g]
        k = k_ref[g]
        v = v_ref[g]
        a = a_ref[g]
        k_k = kk_ref[g]
        k_a = ka_ref[g]
        ld = -jnp.exp(-_softplus(-w_ref[g]) - 0.5)
        kk = k * k_k
        kk = kk / jnp.maximum(jnp.sqrt(seg_sum(kk * kk)), 1e-12)
        kh = k * (1.0 + (a - 1.0) * k_a)
        ra = -kk
        rb = kk * a

        ld_hi, ld_lo = _split(ld)
        cum = _dot(ltri, ld_hi) + _dot(ltri, ld_lo)
        c_last = cum[C - 1:C, :]
        e_nc = jnp.exp(-cum)
        e_rem = jnp.exp(c_last - cum)
        r_t = r * jnp.exp(cum)
        a_t = ra * jnp.exp(cum - ld)
        k_hat = (kh * e_nc).astype(BF16)
        b_hat = (rb * e_nc).astype(BF16)
        lhs = jnp.concatenate([a_t, r_t], axis=0).astype(BF16)
        g_b = _dot_nt(lhs, bd(b_hat))
        g_k = _dot_nt(lhs, bd(k_hat))
        n_ab = jnp.where(tril_strict, g_b[:C], 0.0)
        b_rb = jnp.where(tril_incl, g_b[C:], 0.0)
        a_ak = jnp.where(tril_strict, g_k[:C], 0.0)
        b_rk = jnp.where(tril_incl, g_k[C:], 0.0)

        p = jnp.where(eye, 1.0, 0.0) + n_ab
        n_pow = n_ab
        for _ in range(5):
            n_pow = head_mm(n_pow, n_pow)
            p = p + head_mm(p, n_pow)

        s_t = s_ref[g]
        v_bd = bd(v.astype(BF16))
        z = _dot(lhs, bd(s_t.astype(BF16)))
        sa = head_mm(p, z[:C] + _dot(a_ak.astype(BF16), v_bd))
        sa_bf = sa.astype(BF16)
        y_ref[g] = z[C:] + _dot(b_rb.astype(BF16), bd(sa_bf)) + _dot(b_rk.astype(BF16), v_bd)

        upd_l = jnp.concatenate([rb * e_rem, kh * e_rem], axis=0).astype(BF16)
        upd_r = jnp.concatenate([sa_bf, v.astype(BF16)], axis=0)
        full = _dot_tn(upd_l, upd_r)
        gain = jnp.zeros((C, W), F32)
        for h in range(HEADS_PER_GROUP):
            gain = gain + jnp.where(lane_head == h, full[h * C:(h + 1) * C, :], 0.0)
        c_col = seg_sum(jnp.where(eye, jnp.broadcast_to(c_last, (C, W)), 0.0))
        s_ref[g] = jnp.exp(c_col) * s_t + gain


def _rwkv_recurrence(r, k, v, w_raw, a, k_k, k_a, *, batch, groups_per_step):
    ng, t, _ = r.shape
    chunks = t // (batch * CHUNK)
    gb = groups_per_step
    seq_spec = pl.BlockSpec((gb, CHUNK, GROUP), lambda b, g, c: (g, b * chunks + c, 0))
    par_spec = pl.BlockSpec((gb, 1, GROUP), lambda b, g, c: (g, 0, 0))
    return pl.pallas_call(
        functools.partial(_rwkv_kernel, groups=gb),
        grid=(batch, ng // gb, chunks),
        in_specs=[seq_spec] * 5 + [par_spec] * 2,
        out_specs=seq_spec,
        out_shape=jax.ShapeDtypeStruct((ng, t, GROUP), F32),
        scratch_shapes=[pltpu.VMEM((gb, CHUNK, GROUP), F32)],
        compiler_params=_cparams("parallel", "parallel", "arbitrary"), name="rwkv_recurrence",
    )(r, k, v, w_raw, a, k_k.reshape(ng, 1, GROUP), k_a.reshape(ng, 1, GROUP))


def _rwkv_out_kernel(y_ref, r_ref, k_ref, v_ref, a_ref, g_ref, ka_ref, rk_ref, lg_ref, lb_ref, o_ref):
    bd_mask = (lax.broadcasted_iota(jnp.int32, (GROUP, GROUP), 0) >> 6) == \
              (lax.broadcasted_iota(jnp.int32, (GROUP, GROUP), 1) >> 6)
    ones_bd = jnp.where(bd_mask, 1.0, 0.0).astype(BF16)

    def seg_sum(x):
        hi, lo = _split(x)
        return _dot(hi, ones_bd) + _dot(lo, ones_bd)

    y = y_ref[...]
    mean = seg_sum(y) * (1.0 / RW_HEAD)
    yc = y - mean
    var = seg_sum(yc * yc) * (1.0 / RW_HEAD)
    yn = yc * lax.rsqrt(var + RW_LNX_EPS) * lg_ref[...] + lb_ref[...]
    r = r_ref[...]
    kh = k_ref[...] * (1.0 + (a_ref[...] - 1.0) * ka_ref[...])
    bonus = seg_sum(r * kh * rk_ref[...]) * v_ref[...]
    o_ref[...] = ((yn + bonus) * g_ref[...]).astype(BF16)


def _rwkv_out(y, r, k, v, a, gate, k_a, r_k, lnx_g, lnx_b, *, tm):
    ng, t, _ = y.shape
    seq_spec = pl.BlockSpec((None, tm, GROUP), lambda i, g: (g, i, 0))
    par_spec = pl.BlockSpec((None, 1, GROUP), lambda i, g: (g, 0, 0))
    par = lambda p: p.reshape(ng, 1, GROUP)
    return pl.pallas_call(
        _rwkv_out_kernel,
        grid=(t // tm, ng),
        in_specs=[seq_spec] * 6 + [par_spec] * 4,
        out_specs=pl.BlockSpec((tm, GROUP), lambda i, g: (i, g)),
        out_shape=jax.ShapeDtypeStruct((t, ng * GROUP), BF16),
        compiler_params=_cparams("parallel", "parallel"), name="rwkv_out",
    )(y, r, k, v, a, gate, par(k_a), par(r_k), par(lnx_g), par(lnx_b))


def _post_norm_kernel(h_ref, m_ref, gp_ref, gn_ref, h_out, p_out):
    h = h_ref[...] + _rms(m_ref[...]) * gp_ref[...]
    h_out[...] = h
    p_out[...] = (_rms(h) * gn_ref[...]).astype(BF16)


def _post_norm(h, m, g_post, g_next, *, tm):
    t, d = h.shape
    row_spec = pl.BlockSpec((tm, d), lambda i: (i, 0))
    par_spec = pl.BlockSpec((1, d), lambda i: (0, 0))
    return pl.pallas_call(
        _post_norm_kernel, grid=(t // tm,),
        in_specs=[row_spec, row_spec, par_spec, par_spec],
        out_specs=[row_spec, row_spec],
        out_shape=[jax.ShapeDtypeStruct((t, d), F32), jax.ShapeDtypeStruct((t, d), BF16)],
        compiler_params=_cparams("parallel"), name="post_norm",
    )(h, m, g_post.reshape(1, d), g_next.reshape(1, d))


def _ffn_in_kernel(x_ref, halo_ref, wu_ref, wg_ref, cw_ref, cb_ref, o_ref, *, blocks_per_seq, halo_rows):
    i = pl.program_id(0)
    x = x_ref[...]
    u = _dot(x, wu_ref[...])
    z = _dot(x, wg_ref[...])
    zh = _dot(halo_ref[...], wg_ref[...])
    zh = jnp.where(i % blocks_per_seq == 0, 0.0, zh)
    row = lax.broadcasted_iota(jnp.int32, z.shape, 0)
    z1 = jnp.where(row == 0, zh[halo_rows - 1:halo_rows, :], pltpu.roll(z, 1, 0))
    z2 = pltpu.roll(z, 2, 0)
    z2 = jnp.where(row == 1, zh[halo_rows - 1:halo_rows, :], z2)
    z2 = jnp.where(row == 0, zh[halo_rows - 2:halo_rows - 1, :], z2)
    zc = z * cw_ref[2:3, :] + z1 * cw_ref[1:2, :] + z2 * cw_ref[0:1, :] + cb_ref[...]
    o_ref[...] = (zc * _sigmoid(zc) * u).astype(BF16)


def _ffn_in(x, w_up, w_gate, conv_w, conv_b, *, lp, tm, tn):
    t, d = x.shape
    f = w_up.shape[1]
    hr = 16
    return pl.pallas_call(
        functools.partial(_ffn_in_kernel, blocks_per_seq=lp // tm, halo_rows=hr),
        grid=(t // tm, f // tn),
        in_specs=[pl.BlockSpec((tm, d), lambda i, j: (i, 0)),
                  pl.BlockSpec((hr, d), lambda i, j: (jnp.maximum(i * (tm // hr) - 1, 0), 0)),
                  pl.BlockSpec((d, tn), lambda i, j: (0, j)),
                  pl.BlockSpec((d, tn), lambda i, j: (0, j)),
                  pl.BlockSpec((conv_w.shape[0], tn), lambda i, j: (0, j)),
                  pl.BlockSpec((1, tn), lambda i, j: (0, j))],
        out_specs=pl.BlockSpec((tm, tn), lambda i, j: (i, j)),
        out_shape=jax.ShapeDtypeStruct((t, f), BF16),
        compiler_params=_cparams("parallel", "arbitrary"), name="ffn_in",
    )(x, x, w_up, w_gate, conv_w, conv_b.reshape(1, f))


GLA_LEVELS = 6


def _gla_kernel(q_ref, k_ref, v_ref, g_ref, o_ref, s_ref):
    c_id = pl.program_id(2)

    @pl.when(c_id == 0)
    def _():
        s_ref[...] = jnp.zeros_like(s_ref)

    C = CHUNK
    t_i = lax.broadcasted_iota(jnp.int32, (C, C), 0)
    u_i = lax.broadcasted_iota(jnp.int32, (C, C), 1)
    row = lax.broadcasted_iota(jnp.int32, (C, 1), 0)
    one = lambda m: jnp.where(m, 1.0, 0.0).astype(BF16)

    mats = [one(u_i <= t_i)]
    for l in range(GLA_LEVELS):
        half = 1 << l
        p = (t_i & ~(2 * half - 1)) + (half - 1)
        mats.append(one((u_i > p) & (u_i <= t_i)))
        mats.append(one((u_i > t_i) & (u_i <= p)))
    sel = jnp.concatenate(mats, axis=0)

    q = q_ref[...]
    k = k_ref[...]
    v_bf = v_ref[...].astype(BF16)
    g_hi, g_lo = _split(g_ref[...])
    cums = _dot(sel, g_hi) + _dot(sel, g_lo)
    b = cums[:C]
    b_last = b[C - 1:C, :]

    a = jnp.where(t_i == u_i, jnp.sum(q * k, axis=-1, keepdims=True), 0.0)
    for l in range(GLA_LEVELS):
        half = 1 << l
        dq = cums[(1 + 2 * l) * C:(2 + 2 * l) * C]
        dk = cums[(2 + 2 * l) * C:(3 + 2 * l) * C]
        upper = (row & half) != 0
        ql = jnp.where(upper, q * jnp.exp(dq), 0.0).astype(BF16)
        kl = jnp.where(upper, 0.0, k * jnp.exp(dk)).astype(BF16)
        same_block = (t_i >> (l + 1)) == (u_i >> (l + 1))
        a = a + jnp.where(same_block, _dot_nt(ql, kl), 0.0)

    s_t = s_ref[...]
    qe = (q * jnp.exp(b)).astype(BF16)
    o_ref[...] = _dot_nt(qe, s_t.astype(BF16)) + _dot(a.astype(BF16), v_bf)
    kr = (k * jnp.exp(b_last - b)).astype(BF16)
    s_ref[...] = s_t * jnp.exp(b_last) + _dot_tn(v_bf, kr)


def _gla_recurrence(q, k, v, g, *, batch, hk, hv):
    t = q.shape[0]
    heads = q.shape[1] // hk
    chunks = t // (batch * CHUNK)
    k_spec = pl.BlockSpec((CHUNK, hk), lambda b, h, c: (b * chunks + c, h))
    v_spec = pl.BlockSpec((CHUNK, hv), lambda b, h, c: (b * chunks + c, h))
    return pl.pallas_call(
        _gla_kernel, grid=(batch, heads, chunks),
        in_specs=[k_spec, k_spec, v_spec, k_spec],
        out_specs=v_spec,
        out_shape=jax.ShapeDtypeStruct((t, heads * hv), F32),
        scratch_shapes=[pltpu.VMEM((hv, hk), F32)],
        compiler_params=_cparams("parallel", "parallel", "arbitrary"), name="gla_recurrence",
    )(q, k, v, g)


def _gla_out_kernel(o_ref, r_ref, rb_ref, ng_ref, out_ref):
    o = o_ref[...]
    z = r_ref[...] + rb_ref[...]
    out_ref[...] = (_rms(o) * ng_ref[...] * (z * _sigmoid(z))).astype(BF16)


def _gla_out(o, r, r_b, norm_g, *, hv, tm):
    t, dv = o.shape
    spec = pl.BlockSpec((tm, hv), lambda i, h: (i, h))
    return pl.pallas_call(
        _gla_out_kernel, grid=(t // tm, dv // hv),
        in_specs=[spec, spec, pl.BlockSpec((1, hv), lambda i, h: (0, h)),
                  pl.BlockSpec((1, hv), lambda i, h: (0, 0))],
        out_specs=spec, out_shape=jax.ShapeDtypeStruct((t, dv), BF16),
        compiler_params=_cparams("parallel", "parallel"), name="gla_out",
    )(o, r, r_b.reshape(1, dv), norm_g.reshape(1, hv))


def _tiles(t, lp):
    return dict(
        ew=_pick(lp, (192, 128, 64)),
        lora=_pick(lp, (352, 192, 128, 64)),
        mm=_pick(lp, (704, 1056, 352, 192, 128, 64)),
        ffn=_pick(lp, (1056, 704, 352, 192, 128, 64)),
    )


def _col_tile(n, cands=(512, 256, 128)):
    return _pick(n, cands)


def _rwkv_layer(h, g_pre, p, *, batch, lp, tiles):
    d = h.shape[1]
    xr, xw, xk, xv, xa, xg = _shift_mix(h, g_pre, p["mu"], lp=lp, tm=tiles["ew"])
    tn = _col_tile(d)
    proj = lambda x, w, name: _matmul(x, w.astype(BF16), tm=tiles["mm"], tn=tn, group_out=True, name=name)
    r = proj(xr, p["wr"], "rw_r")
    k = proj(xk, p["wk"], "rw_k")
    v = proj(xv, p["wv"], "rw_v")
    lora = functools.partial(_lora, tm=tiles["lora"], group_out=True)
    w_raw = lora(xw, p["w1"], p["w2"], p["w0"], act1=jnp.tanh, act2=None, name="rw_decay")
    a = lora(xa, p["a1"], p["a2"], p["a0"], act1=None, act2=_sigmoid, name="rw_rate")
    gate = lora(xg, p["g1"], p["g2"], jnp.zeros((d,), F32), act1=_sigmoid, act2=None, name="rw_gate")
    y = _rwkv_recurrence(r, k, v, w_raw, a, p["k_k"], p["k_a"], batch=batch, groups_per_step=2)
    yo = _rwkv_out(y, r, k, v, a, gate, p["k_a"], p["r_k"].reshape(-1), p["lnx_g"], p["lnx_b"],
                   tm=tiles["mm"])
    return _matmul(yo, p["wo"].astype(BF16), tm=tiles["mm"], tn=tn, name="rw_o")


def _log_gate(x):
    return (jnp.minimum(x, 0.0) - jnp.log(1.0 + jnp.exp(-jnp.abs(x)))) * (1.0 / GLA_GATE_TAU)


def _gla_layer(pre, p, *, batch, tiles):
    d = pre.shape[1]
    dk = p["a2"].shape[1]
    dv = p["wo"].shape[0]
    heads = max(4, d // 512)
    hk, hv = dk // heads, dv // heads
    w_in = p["w_in"].astype(BF16)
    mm = functools.partial(_matmul, pre, tm=tiles["mm"])
    q_scale = hk ** -0.5
    q = mm(w_in[:, :dk], tn=_col_tile(dk), epilogue=lambda acc: acc * q_scale, name="gla_q")
    k = mm(w_in[:, dk:2 * dk], tn=_col_tile(dk), name="gla_k")
    v = mm(w_in[:, 2 * dk:2 * dk + dv], tn=_col_tile(dv), name="gla_v")
    r = mm(w_in[:, 2 * dk + dv:], tn=_col_tile(dv), name="gla_r")
    g = _lora(pre, p["a1"], p["a2"], p["a_b"], act1=None, act2=_log_gate, tm=tiles["lora"],
              group_out=False, name="gla_gate")
    o = _gla_recurrence(q, k, v, g, batch=batch, hk=hk, hv=hv)
    og = _gla_out(o, r, p["r_b"], p["norm_g"], hv=hv, tm=tiles["mm"])
    return _matmul(og, p["wo"].astype(BF16), tm=tiles["mm"], tn=_col_tile(d), name="gla_o")


def _ffn(pre, w_up, w_gate, conv_w, conv_b, w_down, *, lp, tiles):
    f = w_up.shape[1]
    act = _ffn_in(pre, w_up.astype(BF16), w_gate.astype(BF16), conv_w, conv_b,
                  lp=lp, tm=tiles["ffn"], tn=_col_tile(f, (256, 128)))
    return _matmul(act, w_down.astype(BF16), tm=tiles["mm"], tn=_col_tile(w_down.shape[1], (256, 128)),
                   name="ffn_down")


def kernel(x, meta, norm_g, rw_mu, rw_w0, rw_w1, rw_w2, rw_a0, rw_a1, rw_a2, rw_g1, rw_g2, rw_k_k, rw_k_a, rw_r_k, rw_wr, rw_wk, rw_wv, rw_wo, rw_lnx_g, rw_lnx_b, gla_w_in, gla_a1, gla_a2, gla_a_b, gla_r_b, gla_norm_g, gla_wo, ffn_up, ffn_gate, ffn_conv, ffn_conv_b, ffn_down):
    batch, seq, d = x.shape
    n_meta = meta.shape[0]
    depth = norm_g.shape[0]
    length = n_meta + seq
    lp = -(-length // CHUNK) * CHUNK
    m = jnp.broadcast_to(meta.astype(x.dtype)[None], (batch, n_meta, d))
    h = jnp.concatenate([m, x, jnp.zeros((batch, lp - length, d), x.dtype)], axis=1).reshape(batch * lp, d)
    tiles = _tiles(batch * lp, lp)

    pre = None
    for i in range(depth):
        j = i // 2
        if i % 2 == 0:
            p = dict(mu=rw_mu[j], w0=rw_w0[j], w1=rw_w1[j], w2=rw_w2[j], a0=rw_a0[j], a1=rw_a1[j],
                     a2=rw_a2[j], g1=rw_g1[j], g2=rw_g2[j], k_k=rw_k_k[j], k_a=rw_k_a[j], r_k=rw_r_k[j],
                     wr=rw_wr[j], wk=rw_wk[j], wv=rw_wv[j], wo=rw_wo[j], lnx_g=rw_lnx_g[j],
                     lnx_b=rw_lnx_b[j])
            mix = _rwkv_layer(h, norm_g[i, 0], p, batch=batch, lp=lp, tiles=tiles)
        else:
            p = dict(w_in=gla_w_in[j], a1=gla_a1[j], a2=gla_a2[j], a_b=gla_a_b[j], r_b=gla_r_b[j],
                     norm_g=gla_norm_g[j], wo=gla_wo[j])
            mix = _gla_layer(pre, p, batch=batch, tiles=tiles)
        h, pre = _post_norm(h, mix, norm_g[i, 1], norm_g[i, 2], tm=tiles["ew"])
        f = _ffn(pre, ffn_up[i], ffn_gate[i], ffn_conv[i], ffn_conv_b[i], ffn_down[i], lp=lp, tiles=tiles)
        g_next = norm_g[i + 1, 0] if i + 1 < depth else norm_g[i, 3]
        h, pre = _post_norm(h, f, norm_g[i, 3], g_next, tm=tiles["ew"])
    return h.reshape(batch, lp, d)[:, n_meta:length]
```

```python
import functools

import numpy as np
import jax
import jax.numpy as jnp
from jax import lax
from jax.experimental import pallas as pl
from jax.experimental.pallas import tpu as pltpu

F32 = jnp.float32
BF16 = jnp.bfloat16

NORM_EPS = 1e-6
RW_LNX_EPS = 64e-5
RW_HEAD = 64
GLA_GATE_TAU = 16.0
CHUNK = 64
LANES = 128
GROUP = 256
HEADS_PER_GROUP = GROUP // RW_HEAD
VMEM_LIMIT = 56 * 1024 * 1024


def _cparams(*sem):
    return pltpu.CompilerParams(dimension_semantics=sem, vmem_limit_bytes=VMEM_LIMIT)


def _pick(n, cands):
    for c in cands:
        if n % c == 0:
            return c
    raise ValueError(f"no tile for {n} in {cands}")


def _rms(x):
    return x * lax.rsqrt(jnp.mean(x * x, axis=-1, keepdims=True) + NORM_EPS)


def _softplus(z):
    return jnp.maximum(z, 0.0) + jnp.log(1.0 + jnp.exp(-jnp.abs(z)))


def _sigmoid(z):
    return 1.0 / (1.0 + jnp.exp(-z))


def _dot(a, b):
    return jnp.dot(a, b, preferred_element_type=F32)


def _dot_nt(a, b):
    return lax.dot_general(a, b, (((1,), (1,)), ((), ())), preferred_element_type=F32)


def _dot_tn(a, b):
    return lax.dot_general(a, b, (((0,), (0,)), ((), ())), preferred_element_type=F32)


def _split(x):
    hi = x.astype(BF16)
    lo = (x - hi.astype(F32)).astype(BF16)
    return hi, lo


def _mm_kernel(x_ref, w_ref, *rest, epilogue, n_extra, group_out):
    extra = rest[:n_extra]
    o_ref = rest[n_extra]
    acc = _dot(x_ref[...], w_ref[...])
    if epilogue is not None:
        acc = epilogue(acc, *[e[...] for e in extra])
    if group_out:
        for g in range(o_ref.shape[0]):
            o_ref[g] = acc[:, g * GROUP:(g + 1) * GROUP].astype(o_ref.dtype)
    else:
        o_ref[...] = acc.astype(o_ref.dtype)


def _matmul(x, w, *, tm, tn, out_dtype=F32, epilogue=None, extras=(), group_out=False, name):
    t, k = x.shape
    n = w.shape[1]
    grid = (t // tm, n // tn)
    in_specs = [pl.BlockSpec((tm, k), lambda i, j: (i, 0)),
                pl.BlockSpec((k, tn), lambda i, j: (0, j))]
    in_specs += [pl.BlockSpec((1, tn), lambda i, j: (0, j)) for _ in extras]
    if group_out:
        gpt = tn // GROUP
        out_shape = jax.ShapeDtypeStruct((n // GROUP, t, GROUP), out_dtype)
        out_spec = pl.BlockSpec((gpt, tm, GROUP), lambda i, j: (j, i, 0))
    else:
        out_shape = jax.ShapeDtypeStruct((t, n), out_dtype)
        out_spec = pl.BlockSpec((tm, tn), lambda i, j: (i, j))
    return pl.pallas_call(
        functools.partial(_mm_kernel, epilogue=epilogue, n_extra=len(extras), group_out=group_out),
        grid=grid, in_specs=in_specs, out_specs=out_spec, out_shape=out_shape,
        compiler_params=_cparams("parallel", "arbitrary"), name=name,
    )(x, w, *extras)


def _lora_kernel(x_ref, w1_ref, w2_ref, b_ref, o_ref, *, act1, act2, group_out):
    t = _dot(x_ref[...], w1_ref[...])
    if act1 is not None:
        t = act1(t)
    y = _dot(t.astype(BF16), w2_ref[...]) + b_ref[...]
    if act2 is not None:
        y = act2(y)
    if group_out:
        for g in range(o_ref.shape[0]):
            o_ref[g] = y[:, g * GROUP:(g + 1) * GROUP]
    else:
        o_ref[...] = y


def _pad_rank(w1, w2):
    r = w1.shape[1]
    rp = -(-r // LANES) * LANES
    return (jnp.pad(w1, ((0, 0), (0, rp - r))).astype(BF16),
            jnp.pad(w2, ((0, rp - r), (0, 0))).astype(BF16))


def _lora(x, w1, w2, bias, *, act1, act2, tm, group_out, name):
    t, k = x.shape
    w1, w2 = _pad_rank(w1, w2)
    rp = w1.shape[1]
    n = w2.shape[1]
    if group_out:
        out_shape = jax.ShapeDtypeStruct((n // GROUP, t, GROUP), F32)
        out_spec = pl.BlockSpec((n // GROUP, tm, GROUP), lambda i: (0, i, 0))
    else:
        out_shape = jax.ShapeDtypeStruct((t, n), F32)
        out_spec = pl.BlockSpec((tm, n), lambda i: (i, 0))
    return pl.pallas_call(
        functools.partial(_lora_kernel, act1=act1, act2=act2, group_out=group_out),
        grid=(t // tm,),
        in_specs=[pl.BlockSpec((tm, k), lambda i: (i, 0)),
                  pl.BlockSpec((k, rp), lambda i: (0, 0)),
                  pl.BlockSpec((rp, n), lambda i: (0, 0)),
                  pl.BlockSpec((1, n), lambda i: (0, 0))],
        out_specs=out_spec, out_shape=out_shape,
        compiler_params=_cparams("parallel"), name=name,
    )(x, w1, w2, bias.reshape(1, n).astype(F32))


def _shift_mix_kernel(h_ref, halo_ref, g_ref, mu_ref, *out_refs, blocks_per_seq):
    i = pl.program_id(0)
    g = g_ref[...]
    pre = _rms(h_ref[...]) * g
    halo = (_rms(halo_ref[...]) * g)[7:8, :]
    halo = jnp.where(i % blocks_per_seq == 0, 0.0, halo)
    row = lax.broadcasted_iota(jnp.int32, pre.shape, 0)
    prev = jnp.where(row == 0, halo, pltpu.roll(pre, 1, 0))
    xx = prev - pre
    for m, o_ref in enumerate(out_refs):
        o_ref[...] = (pre + xx * mu_ref[m:m + 1, :]).astype(BF16)


def _shift_mix(h, g, mu, *, lp, tm):
    t, d = h.shape
    nmix = mu.shape[0]
    row_spec = pl.BlockSpec((tm, d), lambda i: (i, 0))
    return pl.pallas_call(
        functools.partial(_shift_mix_kernel, blocks_per_seq=lp // tm),
        grid=(t // tm,),
        in_specs=[row_spec,
                  pl.BlockSpec((8, d), lambda i: (jnp.maximum(i * (tm // 8) - 1, 0), 0)),
                  pl.BlockSpec((1, d), lambda i: (0, 0)),
                  pl.BlockSpec((nmix, d), lambda i: (0, 0))],
        out_specs=[row_spec] * nmix,
        out_shape=[jax.ShapeDtypeStruct((t, d), BF16)] * nmix,
        compiler_params=_cparams("parallel"), name="rw_shift_mix",
    )(h, h, g.reshape(1, d), mu)


def _rwkv_kernel(r_ref, k_ref, v_ref, w_ref, a_ref, kk_ref, ka_ref, y_ref, s_ref, *, groups):
    c_id = pl.program_id(2)

    @pl.when(c_id == 0)
    def _():
        s_ref[...] = jnp.zeros_like(s_ref)

    C, W = CHUNK, GROUP
    row = lax.broadcasted_iota(jnp.int32, (C, W), 0)
    lane = lax.broadcasted_iota(jnp.int32, (C, W), 1)
    lane_in = lane & (RW_HEAD - 1)
    lane_head = lane >> 6
    tril_incl = lane_in <= row
    tril_strict = lane_in < row
    eye = jnp.where(lane_in == row, 1.0, 0.0)
    diag_block = (lane_in >> 4) == (row >> 4)
    bd_mask = (lax.broadcasted_iota(jnp.int32, (HEADS_PER_GROUP * C, W), 0) >> 6) == \
              (lax.broadcasted_iota(jnp.int32, (HEADS_PER_GROUP * C, W), 1) >> 6)
    ones_bd = jnp.where(bd_mask, 1.0, 0.0).astype(BF16)
    ltri = jnp.where(lax.broadcasted_iota(jnp.int32, (C, C), 1) <=
                     lax.broadcasted_iota(jnp.int32, (C, C), 0), 1.0, 0.0).astype(BF16)
    gs = range(groups)

    def bd(x):
        return jnp.concatenate([x] * HEADS_PER_GROUP, axis=0) * ones_bd

    def seg_sum(x):
        res = _dot(jnp.concatenate(_split(x), axis=0), ones_bd)
        return res[:C] + res[C:]

    def head_mm(lhs, ys):
        n = len(lhs)
        his = [x[0] for x in lhs]
        r_hi = _dot(jnp.concatenate(his + [x[1] for x in lhs], axis=0), bd(ys[0]))
        r_lo = _dot(jnp.concatenate(his, axis=0), bd(ys[1]))
        return [r_hi[i * C:(i + 1) * C] + r_hi[(n + i) * C:(n + i + 1) * C] + r_lo[i * C:(i + 1) * C]
                for i in range(n)]

    r = [r_ref[g] for g in gs]
    k = [k_ref[g] for g in gs]
    v = [v_ref[g].astype(BF16) for g in gs]
    a = [a_ref[g] for g in gs]
    ld = [-jnp.exp(-_softplus(-w_ref[g]) - 0.5) for g in gs]
    kk = [k[g] * kk_ref[g] for g in gs]
    kk = [kk[g] / jnp.maximum(jnp.sqrt(seg_sum(kk[g] * kk[g])), 1e-12) for g in gs]
    kh = [k[g] * (1.0 + (a[g] - 1.0) * ka_ref[g]) for g in gs]
    rb = [kk[g] * a[g] for g in gs]

    ld_s = [_split(ld[g]) for g in gs]
    cum = [_dot(ltri, ld_s[g][0]) + _dot(ltri, ld_s[g][1]) for g in gs]
    c_last = [cum[g][C - 1:C, :] for g in gs]
    e_nc = [jnp.exp(-cum[g]) for g in gs]
    e_rem = [jnp.exp(c_last[g] - cum[g]) for g in gs]
    lhs = [jnp.concatenate([-kk[g] * jnp.exp(cum[g] - ld[g]), r[g] * jnp.exp(cum[g])], axis=0).astype(BF16)
           for g in gs]
    g_b = [_dot_nt(lhs[g], bd((rb[g] * e_nc[g]).astype(BF16))) for g in gs]
    g_k = [_dot_nt(lhs[g], bd((kh[g] * e_nc[g]).astype(BF16))) for g in gs]
    n_ab = [jnp.where(tril_strict, g_b[g][:C], 0.0) for g in gs]
    b_rb = [jnp.where(tril_incl, g_b[g][C:], 0.0).astype(BF16) for g in gs]
    ak_rk = [jnp.concatenate([jnp.where(tril_strict, g_k[g][:C], 0.0),
                              jnp.where(tril_incl, g_k[g][C:], 0.0)], axis=0).astype(BF16) for g in gs]

    nd = [jnp.where(diag_block, n_ab[g], 0.0) for g in gs]
    no_s = [_split(n_ab[g] - nd[g]) for g in gs]
    pd = [eye + nd[g] for g in gs]
    x_s = [_split(nd[g]) for g in gs]
    x_s = [_split(head_mm([x_s[g]], x_s[g])[0]) for g in gs]
    for _ in range(2):
        prod = [head_mm([x_s[g], _split(pd[g])], x_s[g]) for g in gs]
        pd = [pd[g] + prod[g][1] for g in gs]
        x_s = [_split(prod[g][0]) for g in gs]
    pd = [pd[g] + head_mm([_split(pd[g])], x_s[g])[0] for g in gs]
    pd_s = [_split(pd[g]) for g in gs]
    m = [head_mm([pd_s[g]], no_s[g])[0] for g in gs]
    m_s = [_split(m[g]) for g in gs]
    m2_s = [_split(head_mm([m_s[g]], m_s[g])[0]) for g in gs]
    q = [eye + m[g] for g in gs]
    q_s = [_split(q[g] + head_mm([_split(q[g])], m2_s[g])[0]) for g in gs]
    p_s = [_split(head_mm([q_s[g]], pd_s[g])[0]) for g in gs]

    s_t = [s_ref[g] for g in gs]
    z = [_dot(lhs[g], bd(s_t[g].astype(BF16))) for g in gs]
    zv = [_dot(ak_rk[g], bd(v[g])) for g in gs]
    sa = [head_mm([p_s[g]], _split(z[g][:C] + zv[g][:C]))[0].astype(BF16) for g in gs]
    for g in gs:
        y_ref[g] = z[g][C:] + zv[g][C:] + _dot(b_rb[g], bd(sa[g]))

    upd_l = [jnp.concatenate([rb[g] * e_rem[g], kh[g] * e_rem[g]], axis=0).astype(BF16) for g in gs]
    upd_r = [jnp.concatenate([sa[g], v[g]], axis=0) for g in gs]
    full = [_dot_tn(upd_l[g], upd_r[g]) for g in gs]
    c_col = [seg_sum(eye * c_last[g]) for g in gs]
    for g in gs:
        gain = jnp.zeros((C, W), F32)
        for h in range(HEADS_PER_GROUP):
            gain = gain + jnp.where(lane_head == h, full[g][h * C:(h + 1) * C, :], 0.0)
        s_ref[g] = jnp.exp(c_col[g]) * s_t[g] + gain


def _rwkv_recurrence(r, k, v, w_raw, a, k_k, k_a, *, batch, groups_per_step):
    ng, t, _ = r.shape
    chunks = t // (batch * CHUNK)
    gb = groups_per_step
    seq_spec = pl.BlockSpec((gb, CHUNK, GROUP), lambda b, g, c: (g, b * chunks + c, 0))
    par_spec = pl.BlockSpec((gb, 1, GROUP), lambda b, g, c: (g, 0, 0))
    return pl.pallas_call(
        functools.partial(_rwkv_kernel, groups=gb),
        grid=(batch, ng // gb, chunks),
        in_specs=[seq_spec] * 5 + [par_spec] * 2,
        out_specs=seq_spec,
        out_shape=jax.ShapeDtypeStruct((ng, t, GROUP), F32),
        scratch_shapes=[pltpu.VMEM((gb, CHUNK, GROUP), F32)],
        compiler_params=_cparams("parallel", "parallel", "arbitrary"), name="rwkv_recurrence",
    )(r, k, v, w_raw, a, k_k.reshape(ng, 1, GROUP), k_a.reshape(ng, 1, GROUP))


def _rwkv_out_kernel(y_ref, r_ref, k_ref, v_ref, a_ref, g_ref, ka_ref, rk_ref, lg_ref, lb_ref, o_ref):
    bd_mask = (lax.broadcasted_iota(jnp.int32, (GROUP, GROUP), 0) >> 6) == \
              (lax.broadcasted_iota(jnp.int32, (GROUP, GROUP), 1) >> 6)
    ones_bd = jnp.where(bd_mask, 1.0, 0.0).astype(BF16)

    def seg_sum(x):
        hi, lo = _split(x)
        return _dot(hi, ones_bd) + _dot(lo, ones_bd)

    y = y_ref[...]
    mean = seg_sum(y) * (1.0 / RW_HEAD)
    yc = y - mean
    var = seg_sum(yc * yc) * (1.0 / RW_HEAD)
    yn = yc * lax.rsqrt(var + RW_LNX_EPS) * lg_ref[...] + lb_ref[...]
    r = r_ref[...]
    kh = k_ref[...] * (1.0 + (a_ref[...] - 1.0) * ka_ref[...])
    bonus = seg_sum(r * kh * rk_ref[...]) * v_ref[...]
    o_ref[...] = ((yn + bonus) * g_ref[...]).astype(BF16)


def _rwkv_out(y, r, k, v, a, gate, k_a, r_k, lnx_g, lnx_b, *, tm):
    ng, t, _ = y.shape
    seq_spec = pl.BlockSpec((None, tm, GROUP), lambda i, g: (g, i, 0))
    par_spec = pl.BlockSpec((None, 1, GROUP), lambda i, g: (g, 0, 0))
    par = lambda p: p.reshape(ng, 1, GROUP)
    return pl.pallas_call(
        _rwkv_out_kernel,
        grid=(t // tm, ng),
        in_specs=[seq_spec] * 6 + [par_spec] * 4,
        out_specs=pl.BlockSpec((tm, GROUP), lambda i, g: (i, g)),
        out_shape=jax.ShapeDtypeStruct((t, ng * GROUP), BF16),
        compiler_params=_cparams("parallel", "parallel"), name="rwkv_out",
    )(y, r, k, v, a, gate, par(k_a), par(r_k), par(lnx_g), par(lnx_b))


def _post_norm_kernel(h_ref, m_ref, gp_ref, *rest):
    h = h_ref[...] + _rms(m_ref[...]) * gp_ref[...]
    if len(rest) == 1:
        rest[0][...] = h
    else:
        gn_ref, h_out, p_out = rest
        h_out[...] = h
        p_out[...] = (_rms(h) * gn_ref[...]).astype(BF16)


def _post_norm(h, m, g_post, g_next, *, tm):
    t, d = h.shape
    row_spec = pl.BlockSpec((tm, d), lambda i: (i, 0))
    par_spec = pl.BlockSpec((1, d), lambda i: (0, 0))
    h_shape = jax.ShapeDtypeStruct((t, d), F32)
    if g_next is None:
        return pl.pallas_call(
            _post_norm_kernel, grid=(t // tm,),
            in_specs=[row_spec, row_spec, par_spec], out_specs=row_spec, out_shape=h_shape,
            compiler_params=_cparams("parallel"), name="post_norm_last",
        )(h, m, g_post.reshape(1, d)), None
    return pl.pallas_call(
        _post_norm_kernel, grid=(t // tm,),
        in_specs=[row_spec, row_spec, par_spec, par_spec],
        out_specs=[row_spec, row_spec],
        out_shape=[h_shape, jax.ShapeDtypeStruct((t, d), BF16)],
        compiler_params=_cparams("parallel"), name="post_norm",
    )(h, m, g_post.reshape(1, d), g_next.reshape(1, d))


def _ffn_in_kernel(x_ref, halo_ref, wu_ref, wg_ref, cw_ref, cb_ref, o_ref, *, blocks_per_seq, halo_rows):
    i = pl.program_id(0)
    x = x_ref[...]
    u = _dot(x, wu_ref[...])
    z = _dot(x, wg_ref[...])
    zh = _dot(halo_ref[...], wg_ref[...])
    zh = jnp.where(i % blocks_per_seq == 0, 0.0, zh)
    row = lax.broadcasted_iota(jnp.int32, z.shape, 0)
    z1 = jnp.where(row == 0, zh[halo_rows - 1:halo_rows, :], pltpu.roll(z, 1, 0))
    z2 = pltpu.roll(z, 2, 0)
    z2 = jnp.where(row == 1, zh[halo_rows - 1:halo_rows, :], z2)
    z2 = jnp.where(row == 0, zh[halo_rows - 2:halo_rows - 1, :], z2)
    zc = z * cw_ref[2:3, :] + z1 * cw_ref[1:2, :] + z2 * cw_ref[0:1, :] + cb_ref[...]
    o_ref[...] = (zc * _sigmoid(zc) * u).astype(BF16)


def _ffn_in(x, w_up, w_gate, conv_w, conv_b, *, lp, tm, tn):
    t, d = x.shape
    f = w_up.shape[1]
    hr = 16
    return pl.pallas_call(
        functools.partial(_ffn_in_kernel, blocks_per_seq=lp // tm, halo_rows=hr),
        grid=(t // tm, f // tn),
        in_specs=[pl.BlockSpec((tm, d), lambda i, j: (i, 0)),
                  pl.BlockSpec((hr, d), lambda i, j: (jnp.maximum(i * (tm // hr) - 1, 0), 0)),
                  pl.BlockSpec((d, tn), lambda i, j: (0, j)),
                  pl.BlockSpec((d, tn), lambda i, j: (0, j)),
                  pl.BlockSpec((conv_w.shape[0], tn), lambda i, j: (0, j)),
                  pl.BlockSpec((1, tn), lambda i, j: (0, j))],
        out_specs=pl.BlockSpec((tm, tn), lambda i, j: (i, j)),
        out_shape=jax.ShapeDtypeStruct((t, f), BF16),
        compiler_params=_cparams("parallel", "arbitrary"), name="ffn_in",
    )(x, x, w_up, w_gate, conv_w, conv_b.reshape(1, f))


GLA_LEVELS = 6


def _gla_select_matrices():
    t_i = np.arange(CHUNK)[:, None]
    u_i = np.arange(CHUNK)[None, :]
    mats = [u_i <= t_i]
    for l in range(GLA_LEVELS):
        half = 1 << l
        p = (t_i & ~(2 * half - 1)) + (half - 1)
        mats.append((u_i > p) & (u_i <= t_i))
        mats.append((u_i > t_i) & (u_i <= p))
    return jnp.asarray(np.concatenate(mats, axis=0).astype(np.float32), dtype=BF16)


def _gla_kernel(sel_ref, q_ref, k_ref, v_ref, g_ref, o_ref, s_ref, *, heads, hk, hv):
    c_id = pl.program_id(2)

    @pl.when(c_id == 0)
    def _():
        s_ref[...] = jnp.zeros_like(s_ref)

    C = CHUNK
    t_i = lax.broadcasted_iota(jnp.int32, (C, C), 0)
    u_i = lax.broadcasted_iota(jnp.int32, (C, C), 1)
    row = lax.broadcasted_iota(jnp.int32, (C, 1), 0)
    eye = t_i == u_i
    hs = range(heads)
    sel = sel_ref[...]

    q = [q_ref[:, h * hk:(h + 1) * hk] for h in hs]
    k = [k_ref[:, h * hk:(h + 1) * hk] for h in hs]
    v = [v_ref[:, h * hv:(h + 1) * hv].astype(BF16) for h in hs]
    g_s = [_split(g_ref[:, h * hk:(h + 1) * hk]) for h in hs]
    cums = [_dot(sel, g_s[h][0]) + _dot(sel, g_s[h][1]) for h in hs]
    b = [cums[h][:C] for h in hs]
    b_last = [b[h][C - 1:C, :] for h in hs]

    a = [jnp.where(eye, jnp.sum(q[h] * k[h], axis=-1, keepdims=True), 0.0) for h in hs]
    for l in range(GLA_LEVELS):
        upper = (row & (1 << l)) != 0
        same_block = (t_i >> (l + 1)) == (u_i >> (l + 1))
        ql = [jnp.where(upper, q[h] * jnp.exp(cums[h][(1 + 2 * l) * C:(2 + 2 * l) * C]), 0.0).astype(BF16)
              for h in hs]
        kl = [jnp.where(upper, 0.0, k[h] * jnp.exp(cums[h][(2 + 2 * l) * C:(3 + 2 * l) * C])).astype(BF16)
              for h in hs]
        a = [a[h] + jnp.where(same_block, _dot_nt(ql[h], kl[h]), 0.0) for h in hs]

    s_t = [s_ref[h] for h in hs]
    qe = [(q[h] * jnp.exp(b[h])).astype(BF16) for h in hs]
    kr = [(k[h] * jnp.exp(b_last[h] - b[h])).astype(BF16) for h in hs]
    for h in hs:
        o_ref[:, h * hv:(h + 1) * hv] = _dot_nt(qe[h], s_t[h].astype(BF16)) + _dot(a[h].astype(BF16), v[h])
    for h in hs:
        s_ref[h] = s_t[h] * jnp.exp(b_last[h]) + _dot_tn(v[h], kr[h])


def _gla_recurrence(q, k, v, g, *, batch, hk, hv, heads_per_step):
    t = q.shape[0]
    heads = q.shape[1] // hk
    hb = heads_per_step
    chunks = t // (batch * CHUNK)
    sel = _gla_select_matrices()
    k_spec = pl.BlockSpec((CHUNK, hb * hk), lambda b, h, c: (b * chunks + c, h))
    v_spec = pl.BlockSpec((CHUNK, hb * hv), lambda b, h, c: (b * chunks + c, h))
    return pl.pallas_call(
        functools.partial(_gla_kernel, heads=hb, hk=hk, hv=hv),
        grid=(batch, heads // hb, chunks),
        in_specs=[pl.BlockSpec(sel.shape, lambda b, h, c: (0, 0)), k_spec, k_spec, v_spec, k_spec],
        out_specs=v_spec,
        out_shape=jax.ShapeDtypeStruct((t, heads * hv), F32),
        scratch_shapes=[pltpu.VMEM((hb, hv, hk), F32)],
        compiler_params=_cparams("parallel", "parallel", "arbitrary"), name="gla_recurrence",
    )(sel, q, k, v, g)


def _gla_out_kernel(o_ref, r_ref, rb_ref, ng_ref, out_ref):
    o = o_ref[...]
    z = r_ref[...] + rb_ref[...]
    out_ref[...] = (_rms(o) * ng_ref[...] * (z * _sigmoid(z))).astype(BF16)


def _gla_out(o, r, r_b, norm_g, *, hv, tm):
    t, dv = o.shape
    spec = pl.BlockSpec((tm, hv), lambda i, h: (i, h))
    return pl.pallas_call(
        _gla_out_kernel, grid=(t // tm, dv // hv),
        in_specs=[spec, spec, pl.BlockSpec((1, hv), lambda i, h: (0, h)),
                  pl.BlockSpec((1, hv), lambda i, h: (0, 0))],
        out_specs=spec, out_shape=jax.ShapeDtypeStruct((t, dv), BF16),
        compiler_params=_cparams("parallel", "parallel"), name="gla_out",
    )(o, r, r_b.reshape(1, dv), norm_g.reshape(1, hv))


def _tiles(lp):
    return dict(
        ew=_pick(lp, (192, 128, 64)),
        lora=_pick(lp, (352, 192, 128, 64)),
        mm=_pick(lp, (704, 1056, 352, 192, 128, 64)),
        ffn=_pick(lp, (1056, 704, 352, 192, 128, 64)),
    )


def _col_tile(n, cands=(512, 256, 128)):
    return _pick(n, cands)


def _rwkv_layer(h, g_pre, p, *, batch, lp, tiles):
    d = h.shape[1]
    xr, xw, xk, xv, xa, xg = _shift_mix(h, g_pre, p["mu"], lp=lp, tm=tiles["ew"])
    tn = _col_tile(d)
    proj = lambda x, w, name: _matmul(x, w.astype(BF16), tm=tiles["mm"], tn=tn, group_out=True, name=name)
    r = proj(xr, p["wr"], "rw_r")
    k = proj(xk, p["wk"], "rw_k")
    v = proj(xv, p["wv"], "rw_v")
    lora = functools.partial(_lora, tm=tiles["lora"], group_out=True)
    w_raw = lora(xw, p["w1"], p["w2"], p["w0"], act1=jnp.tanh, act2=None, name="rw_decay")
    a = lora(xa, p["a1"], p["a2"], p["a0"], act1=None, act2=_sigmoid, name="rw_rate")
    gate = lora(xg, p["g1"], p["g2"], jnp.zeros((d,), F32), act1=_sigmoid, act2=None, name="rw_gate")
    y = _rwkv_recurrence(r, k, v, w_raw, a, p["k_k"], p["k_a"], batch=batch, groups_per_step=4)
    yo = _rwkv_out(y, r, k, v, a, gate, p["k_a"], p["r_k"].reshape(-1), p["lnx_g"], p["lnx_b"],
                   tm=tiles["mm"])
    return _matmul(yo, p["wo"].astype(BF16), tm=tiles["mm"], tn=tn, name="rw_o")


def _log_gate(x):
    return (jnp.minimum(x, 0.0) - jnp.log(1.0 + jnp.exp(-jnp.abs(x)))) * (1.0 / GLA_GATE_TAU)


def _gla_layer(pre, p, *, batch, tiles):
    d = pre.shape[1]
    dk = p["a2"].shape[1]
    dv = p["wo"].shape[0]
    heads = max(4, d // 512)
    hk, hv = dk // heads, dv // heads
    w_in = p["w_in"].astype(BF16)
    mm = functools.partial(_matmul, pre, tm=tiles["mm"])
    q_scale = hk ** -0.5
    q = mm(w_in[:, :dk], tn=_col_tile(dk), epilogue=lambda acc: acc * q_scale, name="gla_q")
    k = mm(w_in[:, dk:2 * dk], tn=_col_tile(dk), name="gla_k")
    v = mm(w_in[:, 2 * dk:2 * dk + dv], tn=_col_tile(dv), name="gla_v")
    r = mm(w_in[:, 2 * dk + dv:], tn=_col_tile(dv), name="gla_r")
    g = _lora(pre, p["a1"], p["a2"], p["a_b"], act1=None, act2=_log_gate, tm=tiles["lora"],
              group_out=False, name="gla_gate")
    o = _gla_recurrence(q, k, v, g, batch=batch, hk=hk, hv=hv, heads_per_step=2)
    og = _gla_out(o, r, p["r_b"], p["norm_g"], hv=hv, tm=tiles["mm"])
    return _matmul(og, p["wo"].astype(BF16), tm=tiles["mm"], tn=_col_tile(d), name="gla_o")


def _ffn(pre, w_up, w_gate, conv_w, conv_b, w_down, *, lp, tiles):
    f = w_up.shape[1]
    act = _ffn_in(pre, w_up.astype(BF16), w_gate.astype(BF16), conv_w, conv_b,
                  lp=lp, tm=tiles["ffn"], tn=_col_tile(f, (256, 128)))
    return _matmul(act, w_down.astype(BF16), tm=tiles["mm"], tn=_col_tile(w_down.shape[1], (256, 128)),
                   name="ffn_down")


def kernel(x, meta, norm_g, rw_mu, rw_w0, rw_w1, rw_w2, rw_a0, rw_a1, rw_a2, rw_g1, rw_g2, rw_k_k, rw_k_a, rw_r_k, rw_wr, rw_wk, rw_wv, rw_wo, rw_lnx_g, rw_lnx_b, gla_w_in, gla_a1, gla_a2, gla_a_b, gla_r_b, gla_norm_g, gla_wo, ffn_up, ffn_gate, ffn_conv, ffn_conv_b, ffn_down):
    batch, seq, d = x.shape
    n_meta = meta.shape[0]
    depth = norm_g.shape[0]
    length = n_meta + seq
    lp = -(-length // CHUNK) * CHUNK
    m = jnp.broadcast_to(meta.astype(x.dtype)[None], (batch, n_meta, d))
    h = jnp.concatenate([m, x, jnp.zeros((batch, lp - length, d), x.dtype)], axis=1).reshape(batch * lp, d)
    tiles = _tiles(lp)

    pre = None
    for i in range(depth):
        j = i // 2
        if i % 2 == 0:
            p = dict(mu=rw_mu[j], w0=rw_w0[j], w1=rw_w1[j], w2=rw_w2[j], a0=rw_a0[j], a1=rw_a1[j],
                     a2=rw_a2[j], g1=rw_g1[j], g2=rw_g2[j], k_k=rw_k_k[j], k_a=rw_k_a[j], r_k=rw_r_k[j],
                     wr=rw_wr[j], wk=rw_wk[j], wv=rw_wv[j], wo=rw_wo[j], lnx_g=rw_lnx_g[j],
                     lnx_b=rw_lnx_b[j])
            mix = _rwkv_layer(h, norm_g[i, 0], p, batch=batch, lp=lp, tiles=tiles)
        else:
            p = dict(w_in=gla_w_in[j], a1=gla_a1[j], a2=gla_a2[j], a_b=gla_a_b[j], r_b=gla_r_b[j],
                     norm_g=gla_norm_g[j], wo=gla_wo[j])
            mix = _gla_layer(pre, p, batch=batch, tiles=tiles)
        h, pre = _post_norm(h, mix, norm_g[i, 1], norm_g[i, 2], tm=tiles["ew"])
        f = _ffn(pre, ffn_up[i], ffn_gate[i], ffn_conv[i], ffn_conv_b[i], ffn_down[i], lp=lp, tiles=tiles)
        g_next = norm_g[i + 1, 0] if i + 1 < depth else None
        h, pre = _post_norm(h, f, norm_g[i, 3], g_next, tm=tiles["ew"])
    return h.reshape(batch, lp, d)[:, n_meta:length]
```

```python
import functools

import numpy as np
import jax
import jax.numpy as jnp
from jax import lax
from jax.experimental import pallas as pl
from jax.experimental.pallas import tpu as pltpu

F32 = jnp.float32
BF16 = jnp.bfloat16

NORM_EPS = 1e-6
RW_LNX_EPS = 64e-5
RW_HEAD = 64
GLA_GATE_TAU = 16.0
CHUNK = 64
LANES = 128
GROUP = LANES
HEADS_PER_GROUP = GROUP // RW_HEAD
VMEM_LIMIT = 56 * 1024 * 1024


def _cparams(*sem):
    return pltpu.CompilerParams(dimension_semantics=sem, vmem_limit_bytes=VMEM_LIMIT)


def _pick(n, cands):
    for c in cands:
        if n % c == 0:
            return c
    raise ValueError(f"no tile for {n} in {cands}")


def _rms(x):
    return x * lax.rsqrt(jnp.mean(x * x, axis=-1, keepdims=True) + NORM_EPS)


def _softplus(z):
    return jnp.maximum(z, 0.0) + jnp.log(1.0 + jnp.exp(-jnp.abs(z)))


def _sigmoid(z):
    return 1.0 / (1.0 + jnp.exp(-z))


def _dot(a, b):
    return jnp.dot(a, b, preferred_element_type=F32)


def _dot_nt(a, b):
    return lax.dot_general(a, b, (((1,), (1,)), ((), ())), preferred_element_type=F32)


def _dot_tn(a, b):
    return lax.dot_general(a, b, (((0,), (0,)), ((), ())), preferred_element_type=F32)


def _split(x):
    hi = x.astype(BF16)
    lo = (x - hi.astype(F32)).astype(BF16)
    return hi, lo


def _mm_kernel(x_ref, w_ref, *rest, epilogue, n_extra, group_out):
    extra = rest[:n_extra]
    o_ref = rest[n_extra]
    acc = _dot(x_ref[...], w_ref[...].astype(BF16))
    if epilogue is not None:
        acc = epilogue(acc, *[e[...] for e in extra])
    if group_out:
        for g in range(o_ref.shape[0]):
            o_ref[g] = acc[:, g * GROUP:(g + 1) * GROUP].astype(o_ref.dtype)
    else:
        o_ref[...] = acc.astype(o_ref.dtype)


def _matmul(x, w, *, tm, tn, cols=None, out_dtype=F32, epilogue=None, extras=(), group_out=False, name):
    t, k = x.shape
    c0, c1 = (0, w.shape[1]) if cols is None else cols
    n = c1 - c0
    off = c0 // tn
    assert c0 % tn == 0 and n % tn == 0
    grid = (t // tm, n // tn)
    in_specs = [pl.BlockSpec((tm, k), lambda i, j: (i, 0)),
                pl.BlockSpec((k, tn), lambda i, j: (0, j + off))]
    in_specs += [pl.BlockSpec((1, tn), lambda i, j: (0, j)) for _ in extras]
    if group_out:
        gpt = tn // GROUP
        out_shape = jax.ShapeDtypeStruct((n // GROUP, t, GROUP), out_dtype)
        out_spec = pl.BlockSpec((gpt, tm, GROUP), lambda i, j: (j, i, 0))
    else:
        out_shape = jax.ShapeDtypeStruct((t, n), out_dtype)
        out_spec = pl.BlockSpec((tm, tn), lambda i, j: (i, j))
    return pl.pallas_call(
        functools.partial(_mm_kernel, epilogue=epilogue, n_extra=len(extras), group_out=group_out),
        grid=grid, in_specs=in_specs, out_specs=out_spec, out_shape=out_shape,
        compiler_params=_cparams("parallel", "arbitrary"), name=name,
    )(x, w, *extras)


def _lora_kernel(x_ref, w1_ref, w2_ref, b_ref, o_ref, *, act1, act2, group_out):
    t = _dot(x_ref[...], w1_ref[...])
    if act1 is not None:
        t = act1(t)
    y = _dot(t.astype(BF16), w2_ref[...]) + b_ref[...]
    if act2 is not None:
        y = act2(y)
    if group_out:
        for g in range(o_ref.shape[0]):
            o_ref[g] = y[:, g * GROUP:(g + 1) * GROUP]
    else:
        o_ref[...] = y


def _pad_rank(w1, w2):
    r = w1.shape[1]
    rp = -(-r // LANES) * LANES
    return (jnp.pad(w1, ((0, 0), (0, rp - r))).astype(BF16),
            jnp.pad(w2, ((0, rp - r), (0, 0))).astype(BF16))


def _lora(x, w1, w2, bias, *, act1, act2, tm, group_out, name):
    t, k = x.shape
    w1, w2 = _pad_rank(w1, w2)
    rp = w1.shape[1]
    n = w2.shape[1]
    if group_out:
        out_shape = jax.ShapeDtypeStruct((n // GROUP, t, GROUP), F32)
        out_spec = pl.BlockSpec((n // GROUP, tm, GROUP), lambda i: (0, i, 0))
    else:
        out_shape = jax.ShapeDtypeStruct((t, n), F32)
        out_spec = pl.BlockSpec((tm, n), lambda i: (i, 0))
    return pl.pallas_call(
        functools.partial(_lora_kernel, act1=act1, act2=act2, group_out=group_out),
        grid=(t // tm,),
        in_specs=[pl.BlockSpec((tm, k), lambda i: (i, 0)),
                  pl.BlockSpec((k, rp), lambda i: (0, 0)),
                  pl.BlockSpec((rp, n), lambda i: (0, 0)),
                  pl.BlockSpec((1, n), lambda i: (0, 0))],
        out_specs=out_spec, out_shape=out_shape,
        compiler_params=_cparams("parallel"), name=name,
    )(x, w1, w2, bias.reshape(1, n).astype(F32))


def _shift_mix_kernel(h_ref, halo_ref, g_ref, mu_ref, *out_refs, blocks_per_seq):
    i = pl.program_id(0)
    g = g_ref[...]
    pre = _rms(h_ref[...]) * g
    halo = (_rms(halo_ref[...]) * g)[7:8, :]
    halo = jnp.where(i % blocks_per_seq == 0, 0.0, halo)
    row = lax.broadcasted_iota(jnp.int32, pre.shape, 0)
    prev = jnp.where(row == 0, halo, pltpu.roll(pre, 1, 0))
    xx = prev - pre
    for m, o_ref in enumerate(out_refs):
        o_ref[...] = (pre + xx * mu_ref[m:m + 1, :]).astype(BF16)


def _shift_mix(h, g, mu, *, lp, tm):
    t, d = h.shape
    nmix = mu.shape[0]
    row_spec = pl.BlockSpec((tm, d), lambda i: (i, 0))
    return pl.pallas_call(
        functools.partial(_shift_mix_kernel, blocks_per_seq=lp // tm),
        grid=(t // tm,),
        in_specs=[row_spec,
                  pl.BlockSpec((8, d), lambda i: (jnp.maximum(i * (tm // 8) - 1, 0), 0)),
                  pl.BlockSpec((1, d), lambda i: (0, 0)),
                  pl.BlockSpec((nmix, d), lambda i: (0, 0))],
        out_specs=[row_spec] * nmix,
        out_shape=[jax.ShapeDtypeStruct((t, d), BF16)] * nmix,
        compiler_params=_cparams("parallel"), name="rw_shift_mix",
    )(h, h, g.reshape(1, d), mu)


def _rwkv_kernel(r_ref, k_ref, v_ref, w_ref, a_ref, gate_ref, kk_ref, ka_ref, rk_ref, lg_ref, lb_ref,
                 o_ref, s_ref, *, groups):
    c_id = pl.program_id(2)

    @pl.when(c_id == 0)
    def _():
        s_ref[...] = jnp.zeros_like(s_ref)

    C, W = CHUNK, GROUP
    row = lax.broadcasted_iota(jnp.int32, (C, W), 0)
    lane = lax.broadcasted_iota(jnp.int32, (C, W), 1)
    lane_in = lane & (RW_HEAD - 1)
    lane_head = lane >> 6
    tril_incl = lane_in <= row
    tril_strict = lane_in < row
    eye = jnp.where(lane_in == row, 1.0, 0.0)
    diag_block = (lane_in >> 4) == (row >> 4)
    bd_mask = (lax.broadcasted_iota(jnp.int32, (HEADS_PER_GROUP * C, W), 0) >> 6) == \
              (lax.broadcasted_iota(jnp.int32, (HEADS_PER_GROUP * C, W), 1) >> 6)
    ones_bd = jnp.where(bd_mask, 1.0, 0.0).astype(BF16)
    ltri = jnp.where(lax.broadcasted_iota(jnp.int32, (C, C), 1) <=
                     lax.broadcasted_iota(jnp.int32, (C, C), 0), 1.0, 0.0).astype(BF16)
    gs = range(groups)

    def bd(x):
        return jnp.concatenate([x] * HEADS_PER_GROUP, axis=0) * ones_bd

    def seg_sum(x):
        res = _dot(jnp.concatenate(_split(x), axis=0), ones_bd)
        return res[:C] + res[C:]

    def head_mm(lhs, ys):
        n = len(lhs)
        his = [x[0] for x in lhs]
        r_hi = _dot(jnp.concatenate(his + [x[1] for x in lhs], axis=0), bd(ys[0]))
        r_lo = _dot(jnp.concatenate(his, axis=0), bd(ys[1]))
        return [r_hi[i * C:(i + 1) * C] + r_hi[(n + i) * C:(n + i + 1) * C] + r_lo[i * C:(i + 1) * C]
                for i in range(n)]

    r = [r_ref[g] for g in gs]
    k = [k_ref[g] for g in gs]
    v_f = [v_ref[g] for g in gs]
    v = [v_f[g].astype(BF16) for g in gs]
    a = [a_ref[g] for g in gs]
    ld = [-jnp.exp(-_softplus(-w_ref[g]) - 0.5) for g in gs]
    kk = [k[g] * kk_ref[g] for g in gs]
    kk = [kk[g] / jnp.maximum(jnp.sqrt(seg_sum(kk[g] * kk[g])), 1e-12) for g in gs]
    kh = [k[g] * (1.0 + (a[g] - 1.0) * ka_ref[g]) for g in gs]
    rb = [kk[g] * a[g] for g in gs]

    ld_s = [_split(ld[g]) for g in gs]
    cum = [_dot(ltri, ld_s[g][0]) + _dot(ltri, ld_s[g][1]) for g in gs]
    c_last = [cum[g][C - 1:C, :] for g in gs]
    e_nc = [jnp.exp(-cum[g]) for g in gs]
    e_rem = [jnp.exp(c_last[g] - cum[g]) for g in gs]
    lhs = [jnp.concatenate([-kk[g] * jnp.exp(cum[g] - ld[g]), r[g] * jnp.exp(cum[g])], axis=0).astype(BF16)
           for g in gs]
    g_b = [_dot_nt(lhs[g], bd((rb[g] * e_nc[g]).astype(BF16))) for g in gs]
    g_k = [_dot_nt(lhs[g], bd((kh[g] * e_nc[g]).astype(BF16))) for g in gs]
    n_ab = [jnp.where(tril_strict, g_b[g][:C], 0.0) for g in gs]
    b_rb = [jnp.where(tril_incl, g_b[g][C:], 0.0).astype(BF16) for g in gs]
    ak_rk = [jnp.concatenate([jnp.where(tril_strict, g_k[g][:C], 0.0),
                              jnp.where(tril_incl, g_k[g][C:], 0.0)], axis=0).astype(BF16) for g in gs]

    nd = [jnp.where(diag_block, n_ab[g], 0.0) for g in gs]
    no_s = [_split(n_ab[g] - nd[g]) for g in gs]
    pd = [eye + nd[g] for g in gs]
    x_s = [_split(nd[g]) for g in gs]
    x_s = [_split(head_mm([x_s[g]], x_s[g])[0]) for g in gs]
    for _ in range(2):
        prod = [head_mm([x_s[g], _split(pd[g])], x_s[g]) for g in gs]
        pd = [pd[g] + prod[g][1] for g in gs]
        x_s = [_split(prod[g][0]) for g in gs]
    pd = [pd[g] + head_mm([_split(pd[g])], x_s[g])[0] for g in gs]
    pd_s = [_split(pd[g]) for g in gs]
    m = [head_mm([pd_s[g]], no_s[g])[0] for g in gs]
    m_s = [_split(m[g]) for g in gs]
    m2_s = [_split(head_mm([m_s[g]], m_s[g])[0]) for g in gs]
    q = [eye + m[g] for g in gs]
    q_s = [_split(q[g] + head_mm([_split(q[g])], m2_s[g])[0]) for g in gs]
    p_s = [_split(head_mm([q_s[g]], pd_s[g])[0]) for g in gs]

    s_t = [s_ref[g] for g in gs]
    z = [_dot(lhs[g], bd(s_t[g].astype(BF16))) for g in gs]
    zv = [_dot(ak_rk[g], bd(v[g])) for g in gs]
    sa = [head_mm([p_s[g]], _split(z[g][:C] + zv[g][:C]))[0].astype(BF16) for g in gs]
    y = [z[g][C:] + zv[g][C:] + _dot(b_rb[g], bd(sa[g])) for g in gs]

    yc = [y[g] - seg_sum(y[g]) * (1.0 / RW_HEAD) for g in gs]
    var = [seg_sum(yc[g] * yc[g]) * (1.0 / RW_HEAD) for g in gs]
    bonus = [seg_sum(r[g] * kh[g] * rk_ref[g]) * v_f[g] for g in gs]
    for g in gs:
        yn = yc[g] * lax.rsqrt(var[g] + RW_LNX_EPS) * lg_ref[g] + lb_ref[g]
        o_ref[:, g * W:(g + 1) * W] = ((yn + bonus[g]) * gate_ref[g]).astype(BF16)

    upd_l = [jnp.concatenate([rb[g] * e_rem[g], kh[g] * e_rem[g]], axis=0).astype(BF16) for g in gs]
    upd_r = [jnp.concatenate([sa[g], v[g]], axis=0) for g in gs]
    full = [_dot_tn(upd_l[g], upd_r[g]) for g in gs]
    c_col = [seg_sum(eye * c_last[g]) for g in gs]
    for g in gs:
        gain = jnp.zeros((C, W), F32)
        for h in range(HEADS_PER_GROUP):
            gain = gain + jnp.where(lane_head == h, full[g][h * C:(h + 1) * C, :], 0.0)
        s_ref[g] = jnp.exp(c_col[g]) * s_t[g] + gain


def _rwkv_recurrence(r, k, v, w_raw, a, gate, k_k, k_a, r_k, lnx_g, lnx_b, *, batch, groups_per_step):
    ng, t, _ = r.shape
    chunks = t // (batch * CHUNK)
    gb = groups_per_step
    seq_spec = pl.BlockSpec((gb, CHUNK, GROUP), lambda b, g, c: (g, b * chunks + c, 0))
    par_spec = pl.BlockSpec((gb, 1, GROUP), lambda b, g, c: (g, 0, 0))
    par = lambda p: p.reshape(ng, 1, GROUP)
    return pl.pallas_call(
        functools.partial(_rwkv_kernel, groups=gb),
        grid=(batch, ng // gb, chunks),
        in_specs=[seq_spec] * 6 + [par_spec] * 5,
        out_specs=pl.BlockSpec((CHUNK, gb * GROUP), lambda b, g, c: (b * chunks + c, g)),
        out_shape=jax.ShapeDtypeStruct((t, ng * GROUP), BF16),
        scratch_shapes=[pltpu.VMEM((gb, CHUNK, GROUP), F32)],
        compiler_params=_cparams("parallel", "parallel", "arbitrary"), name="rwkv_recurrence",
    )(r, k, v, w_raw, a, gate, par(k_k), par(k_a), par(r_k), par(lnx_g), par(lnx_b))


def _post_norm_kernel(h_ref, m_ref, gp_ref, *rest):
    h = h_ref[...] + _rms(m_ref[...]) * gp_ref[...]
    if len(rest) == 1:
        rest[0][...] = h
    else:
        gn_ref, h_out, p_out = rest
        h_out[...] = h
        p_out[...] = (_rms(h) * gn_ref[...]).astype(BF16)


def _post_norm(h, m, g_post, g_next, *, tm):
    t, d = h.shape
    row_spec = pl.BlockSpec((tm, d), lambda i: (i, 0))
    par_spec = pl.BlockSpec((1, d), lambda i: (0, 0))
    h_shape = jax.ShapeDtypeStruct((t, d), F32)
    if g_next is None:
        return pl.pallas_call(
            _post_norm_kernel, grid=(t // tm,),
            in_specs=[row_spec, row_spec, par_spec], out_specs=row_spec, out_shape=h_shape,
            compiler_params=_cparams("parallel"), name="post_norm_last",
        )(h, m, g_post.reshape(1, d)), None
    return pl.pallas_call(
        _post_norm_kernel, grid=(t // tm,),
        in_specs=[row_spec, row_spec, par_spec, par_spec],
        out_specs=[row_spec, row_spec],
        out_shape=[h_shape, jax.ShapeDtypeStruct((t, d), BF16)],
        compiler_params=_cparams("parallel"), name="post_norm",
    )(h, m, g_post.reshape(1, d), g_next.reshape(1, d))


def _ffn_in_kernel(x_ref, halo_ref, wu_ref, wg_ref, cw_ref, cb_ref, o_ref, *, blocks_per_seq, halo_rows, sub):
    i = pl.program_id(0)
    wu = wu_ref[...].astype(BF16)
    wg = wg_ref[...].astype(BF16)
    zh = _dot(halo_ref[...], wg)
    zh = jnp.where(i % blocks_per_seq == 0, 0.0, zh)
    prev1, prev2 = zh[halo_rows - 1:halo_rows, :], zh[halo_rows - 2:halo_rows - 1, :]
    row = lax.broadcasted_iota(jnp.int32, (sub, wu.shape[1]), 0)
    n_sub = x_ref.shape[0] // sub

    def project(s):
        xs = x_ref[s * sub:(s + 1) * sub, :]
        return _dot(xs, wu), _dot(xs, wg)

    cur = project(0)
    for s in range(n_sub):
        nxt = project(s + 1) if s + 1 < n_sub else None
        u, z = cur
        z1 = jnp.where(row == 0, prev1, pltpu.roll(z, 1, 0))
        z2 = pltpu.roll(z, 2, 0)
        z2 = jnp.where(row == 1, prev1, z2)
        z2 = jnp.where(row == 0, prev2, z2)
        zc = z * cw_ref[2:3, :] + z1 * cw_ref[1:2, :] + z2 * cw_ref[0:1, :] + cb_ref[...]
        o_ref[s * sub:(s + 1) * sub, :] = (zc * _sigmoid(zc) * u).astype(BF16)
        prev1, prev2 = z[sub - 1:sub, :], z[sub - 2:sub - 1, :]
        cur = nxt


def _ffn_in(x, w_up, w_gate, conv_w, conv_b, *, lp, tm, tn):
    t, d = x.shape
    f = w_up.shape[1]
    hr = 16
    return pl.pallas_call(
        functools.partial(_ffn_in_kernel, blocks_per_seq=lp // tm, halo_rows=hr,
                          sub=_pick(tm, (352, 96, 64))),
        grid=(t // tm, f // tn),
        in_specs=[pl.BlockSpec((tm, d), lambda i, j: (i, 0)),
                  pl.BlockSpec((hr, d), lambda i, j: (jnp.maximum(i * (tm // hr) - 1, 0), 0)),
                  pl.BlockSpec((d, tn), lambda i, j: (0, j)),
                  pl.BlockSpec((d, tn), lambda i, j: (0, j)),
                  pl.BlockSpec((conv_w.shape[0], tn), lambda i, j: (0, j)),
                  pl.BlockSpec((1, tn), lambda i, j: (0, j))],
        out_specs=pl.BlockSpec((tm, tn), lambda i, j: (i, j)),
        out_shape=jax.ShapeDtypeStruct((t, f), BF16),
        compiler_params=_cparams("parallel", "arbitrary"), name="ffn_in",
    )(x, x, w_up, w_gate, conv_w, conv_b.reshape(1, f))


GLA_LEVELS = 6


def _gla_select_matrices():
    t_i = np.arange(CHUNK)[:, None]
    u_i = np.arange(CHUNK)[None, :]
    mats = [u_i <= t_i]
    for l in range(GLA_LEVELS):
        half = 1 << l
        p = (t_i & ~(2 * half - 1)) + (half - 1)
        mats.append((u_i > p) & (u_i <= t_i))
        mats.append((u_i > t_i) & (u_i <= p))
    return jnp.asarray(np.concatenate(mats, axis=0).astype(np.float32), dtype=BF16)


def _gla_kernel(sel_ref, q_ref, k_ref, v_ref, g_ref, r_ref, rb_ref, ng_ref, o_ref, s_ref, *, heads, hk, hv):
    c_id = pl.program_id(2)

    @pl.when(c_id == 0)
    def _():
        s_ref[...] = jnp.zeros_like(s_ref)

    C = CHUNK
    t_i = lax.broadcasted_iota(jnp.int32, (C, C), 0)
    u_i = lax.broadcasted_iota(jnp.int32, (C, C), 1)
    row = lax.broadcasted_iota(jnp.int32, (C, 1), 0)
    eye = t_i == u_i
    hs = range(heads)
    sel = sel_ref[...]

    q = [q_ref[:, h * hk:(h + 1) * hk] for h in hs]
    k = [k_ref[:, h * hk:(h + 1) * hk] for h in hs]
    v = [v_ref[:, h * hv:(h + 1) * hv].astype(BF16) for h in hs]
    g_s = [_split(g_ref[:, h * hk:(h + 1) * hk]) for h in hs]
    cums = [_dot(sel, g_s[h][0]) + _dot(sel, g_s[h][1]) for h in hs]
    b = [cums[h][:C] for h in hs]
    b_last = [b[h][C - 1:C, :] for h in hs]

    a = [jnp.where(eye, jnp.sum(q[h] * k[h], axis=-1, keepdims=True), 0.0) for h in hs]
    for l in range(GLA_LEVELS):
        upper = (row & (1 << l)) != 0
        same_block = (t_i >> (l + 1)) == (u_i >> (l + 1))
        ql = [jnp.where(upper, q[h] * jnp.exp(cums[h][(1 + 2 * l) * C:(2 + 2 * l) * C]), 0.0).astype(BF16)
              for h in hs]
        kl = [jnp.where(upper, 0.0, k[h] * jnp.exp(cums[h][(2 + 2 * l) * C:(3 + 2 * l) * C])).astype(BF16)
              for h in hs]
        a = [a[h] + jnp.where(same_block, _dot_nt(ql[h], kl[h]), 0.0) for h in hs]

    s_t = [s_ref[h] for h in hs]
    qe = [(q[h] * jnp.exp(b[h])).astype(BF16) for h in hs]
    kr = [(k[h] * jnp.exp(b_last[h] - b[h])).astype(BF16) for h in hs]
    o = [_dot_nt(qe[h], s_t[h].astype(BF16)) + _dot(a[h].astype(BF16), v[h]) for h in hs]
    for h in hs:
        s_ref[h] = s_t[h] * jnp.exp(b_last[h]) + _dot_tn(v[h], kr[h])
    for h in hs:
        zg = r_ref[:, h * hv:(h + 1) * hv] + rb_ref[:, h * hv:(h + 1) * hv]
        o_ref[:, h * hv:(h + 1) * hv] = (_rms(o[h]) * ng_ref[...] * (zg * _sigmoid(zg))).astype(BF16)


def _gla_recurrence(q, k, v, g, r, r_b, norm_g, *, batch, hk, hv, heads_per_step):
    t = q.shape[0]
    heads = q.shape[1] // hk
    hb = heads_per_step
    chunks = t // (batch * CHUNK)
    sel = _gla_select_matrices()
    k_spec = pl.BlockSpec((CHUNK, hb * hk), lambda b, h, c: (b * chunks + c, h))
    v_spec = pl.BlockSpec((CHUNK, hb * hv), lambda b, h, c: (b * chunks + c, h))
    return pl.pallas_call(
        functools.partial(_gla_kernel, heads=hb, hk=hk, hv=hv),
        grid=(batch, heads // hb, chunks),
        in_specs=[pl.BlockSpec(sel.shape, lambda b, h, c: (0, 0)), k_spec, k_spec, v_spec, k_spec, v_spec,
                  pl.BlockSpec((1, hb * hv), lambda b, h, c: (0, h)),
                  pl.BlockSpec((1, hv), lambda b, h, c: (0, 0))],
        out_specs=v_spec,
        out_shape=jax.ShapeDtypeStruct((t, heads * hv), BF16),
        scratch_shapes=[pltpu.VMEM((hb, hv, hk), F32)],
        compiler_params=_cparams("parallel", "parallel", "arbitrary"), name="gla_recurrence",
    )(sel, q, k, v, g, r, r_b.reshape(1, heads * hv), norm_g.reshape(1, hv))


def _tiles(lp):
    return dict(
        ew=_pick(lp, (192, 128, 64)),
        lora=_pick(lp, (352, 192, 128, 64)),
        mm=_pick(lp, (1056, 704, 352, 192, 128, 64)),
        down=_pick(lp, (704, 1056, 352, 192, 128, 64)),
        ffn=_pick(lp, (1056, 704, 352, 192, 128, 64)),
    )


def _col_tile(n, cands=(512, 256, 128)):
    return _pick(n, cands)


def _rwkv_layer(h, g_pre, p, *, batch, lp, tiles):
    d = h.shape[1]
    xr, xw, xk, xv, xa, xg = _shift_mix(h, g_pre, p["mu"], lp=lp, tm=tiles["ew"])
    tn = _col_tile(d)
    proj = lambda x, w, name: _matmul(x, w, tm=tiles["mm"], tn=tn, group_out=True, name=name)
    r = proj(xr, p["wr"], "rw_r")
    k = proj(xk, p["wk"], "rw_k")
    v = proj(xv, p["wv"], "rw_v")
    lora = functools.partial(_lora, tm=tiles["lora"], group_out=True)
    w_raw = lora(xw, p["w1"], p["w2"], p["w0"], act1=jnp.tanh, act2=None, name="rw_decay")
    a = lora(xa, p["a1"], p["a2"], p["a0"], act1=None, act2=_sigmoid, name="rw_rate")
    gate = lora(xg, p["g1"], p["g2"], jnp.zeros((d,), F32), act1=_sigmoid, act2=None, name="rw_gate")
    yo = _rwkv_recurrence(r, k, v, w_raw, a, gate, p["k_k"], p["k_a"], p["r_k"].reshape(-1),
                          p["lnx_g"], p["lnx_b"], batch=batch, groups_per_step=8)
    return _matmul(yo, p["wo"], tm=tiles["mm"], tn=tn, name="rw_o")


def _log_gate(x):
    return (jnp.minimum(x, 0.0) - jnp.log(1.0 + jnp.exp(-jnp.abs(x)))) * (1.0 / GLA_GATE_TAU)


def _gla_layer(pre, p, *, batch, tiles):
    d = pre.shape[1]
    dk = p["a2"].shape[1]
    dv = p["wo"].shape[0]
    heads = max(4, d // 512)
    hk, hv = dk // heads, dv // heads
    mm = functools.partial(_matmul, pre, p["w_in"], tm=tiles["mm"], tn=_col_tile(dk))
    q_scale = hk ** -0.5
    q = mm(cols=(0, dk), epilogue=lambda acc: acc * q_scale, name="gla_q")
    k = mm(cols=(dk, 2 * dk), name="gla_k")
    v = mm(cols=(2 * dk, 2 * dk + dv), name="gla_v")
    r = mm(cols=(2 * dk + dv, 2 * dk + 2 * dv), name="gla_r")
    g = _lora(pre, p["a1"], p["a2"], p["a_b"], act1=None, act2=_log_gate, tm=tiles["lora"],
              group_out=False, name="gla_gate")
    og = _gla_recurrence(q, k, v, g, r, p["r_b"], p["norm_g"], batch=batch, hk=hk, hv=hv, heads_per_step=2)
    return _matmul(og, p["wo"], tm=tiles["mm"], tn=_col_tile(d), name="gla_o")


def _ffn(pre, w_up, w_gate, conv_w, conv_b, w_down, *, lp, tiles):
    f = w_up.shape[1]
    act = _ffn_in(pre, w_up, w_gate, conv_w, conv_b, lp=lp, tm=tiles["ffn"], tn=_col_tile(f, (256, 128)))
    return _matmul(act, w_down.astype(BF16), tm=tiles["down"], tn=_col_tile(w_down.shape[1], (256, 128)),
                   name="ffn_down")


def kernel(x, meta, norm_g, rw_mu, rw_w0, rw_w1, rw_w2, rw_a0, rw_a1, rw_a2, rw_g1, rw_g2, rw_k_k, rw_k_a, rw_r_k, rw_wr, rw_wk, rw_wv, rw_wo, rw_lnx_g, rw_lnx_b, gla_w_in, gla_a1, gla_a2, gla_a_b, gla_r_b, gla_norm_g, gla_wo, ffn_up, ffn_gate, ffn_conv, ffn_conv_b, ffn_down):
    batch, seq, d = x.shape
    n_meta = meta.shape[0]
    depth = norm_g.shape[0]
    length = n_meta + seq
    lp = -(-length // CHUNK) * CHUNK
    m = jnp.broadcast_to(meta.astype(x.dtype)[None], (batch, n_meta, d))
    h = jnp.concatenate([m, x, jnp.zeros((batch, lp - length, d), x.dtype)], axis=1).reshape(batch * lp, d)
    tiles = _tiles(lp)

    pre = None
    for i in range(depth):
        j = i // 2
        if i % 2 == 0:
            p = dict(mu=rw_mu[j], w0=rw_w0[j], w1=rw_w1[j], w2=rw_w2[j], a0=rw_a0[j], a1=rw_a1[j],
                     a2=rw_a2[j], g1=rw_g1[j], g2=rw_g2[j], k_k=rw_k_k[j], k_a=rw_k_a[j], r_k=rw_r_k[j],
                     wr=rw_wr[j], wk=rw_wk[j], wv=rw_wv[j], wo=rw_wo[j], lnx_g=rw_lnx_g[j],
                     lnx_b=rw_lnx_b[j])
            mix = _rwkv_layer(h, norm_g[i, 0], p, batch=batch, lp=lp, tiles=tiles)
        else:
            p = dict(w_in=gla_w_in[j], a1=gla_a1[j], a2=gla_a2[j], a_b=gla_a_b[j], r_b=gla_r_b[j],
                     norm_g=gla_norm_g[j], wo=gla_wo[j])
            mix = _gla_layer(pre, p, batch=batch, tiles=tiles)
        h, pre = _post_norm(h, mix, norm_g[i, 1], norm_g[i, 2], tm=tiles["ew"])
        f = _ffn(pre, ffn_up[i], ffn_gate[i], ffn_conv[i], ffn_conv_b[i], ffn_down[i], lp=lp, tiles=tiles)
        g_next = norm_g[i + 1, 0] if i + 1 < depth else None
        h, pre = _post_norm(h, f, norm_g[i, 3], g_next, tm=tiles["ew"])
    return h.reshape(batch, lp, d)[:, n_meta:length]
```

```python
import functools

import numpy as np
import jax
import jax.numpy as jnp
from jax import lax
from jax.experimental import pallas as pl
from jax.experimental.pallas import tpu as pltpu

F32 = jnp.float32
BF16 = jnp.bfloat16

NORM_EPS = 1e-6
RW_LNX_EPS = 64e-5
RW_HEAD = 64
GLA_GATE_TAU = 16.0
CHUNK = 64
LANES = 128
GROUP = LANES
HEADS_PER_GROUP = GROUP // RW_HEAD
VMEM_LIMIT = 56 * 1024 * 1024


def _cparams(*sem):
    return pltpu.CompilerParams(dimension_semantics=sem, vmem_limit_bytes=VMEM_LIMIT)


def _pick(n, cands):
    for c in cands:
        if n % c == 0:
            return c
    raise ValueError(f"no tile for {n} in {cands}")


def _rms(x):
    return x * lax.rsqrt(jnp.mean(x * x, axis=-1, keepdims=True) + NORM_EPS)


def _softplus(z):
    return jnp.maximum(z, 0.0) + jnp.log(1.0 + jnp.exp(-jnp.abs(z)))


def _sigmoid(z):
    return 1.0 / (1.0 + jnp.exp(-z))


def _dot(a, b):
    return jnp.dot(a, b, preferred_element_type=F32)


def _dot_nt(a, b):
    return lax.dot_general(a, b, (((1,), (1,)), ((), ())), preferred_element_type=F32)


def _dot_tn(a, b):
    return lax.dot_general(a, b, (((0,), (0,)), ((), ())), preferred_element_type=F32)


def _split(x):
    hi = x.astype(BF16)
    lo = (x - hi.astype(F32)).astype(BF16)
    return hi, lo


def _mm_kernel(x_ref, w_ref, *rest, epilogue, n_extra, group_out):
    extra = rest[:n_extra]
    o_ref = rest[n_extra]
    acc = _dot(x_ref[...], w_ref[...].astype(BF16))
    if epilogue is not None:
        acc = epilogue(acc, *[e[...] for e in extra])
    if group_out:
        for g in range(o_ref.shape[0]):
            o_ref[g] = acc[:, g * GROUP:(g + 1) * GROUP].astype(o_ref.dtype)
    else:
        o_ref[...] = acc.astype(o_ref.dtype)


def _matmul(x, w, *, tm, tn, cols=None, out_dtype=F32, epilogue=None, extras=(), group_out=False, name):
    t, k = x.shape
    c0, c1 = (0, w.shape[1]) if cols is None else cols
    n = c1 - c0
    off = c0 // tn
    assert c0 % tn == 0 and n % tn == 0
    grid = (t // tm, n // tn)
    in_specs = [pl.BlockSpec((tm, k), lambda i, j: (i, 0)),
                pl.BlockSpec((k, tn), lambda i, j: (0, j + off))]
    in_specs += [pl.BlockSpec((1, tn), lambda i, j: (0, j)) for _ in extras]
    if group_out:
        gpt = tn // GROUP
        out_shape = jax.ShapeDtypeStruct((n // GROUP, t, GROUP), out_dtype)
        out_spec = pl.BlockSpec((gpt, tm, GROUP), lambda i, j: (j, i, 0))
    else:
        out_shape = jax.ShapeDtypeStruct((t, n), out_dtype)
        out_spec = pl.BlockSpec((tm, tn), lambda i, j: (i, j))
    return pl.pallas_call(
        functools.partial(_mm_kernel, epilogue=epilogue, n_extra=len(extras), group_out=group_out),
        grid=grid, in_specs=in_specs, out_specs=out_spec, out_shape=out_shape,
        compiler_params=_cparams("parallel", "arbitrary"), name=name,
    )(x, w, *extras)


def _lora_kernel(x_ref, w1_ref, w2_ref, b_ref, o_ref, *, act1, act2, group_out):
    t = _dot(x_ref[...], w1_ref[...])
    if act1 is not None:
        t = act1(t)
    y = _dot(t.astype(BF16), w2_ref[...]) + b_ref[...]
    if act2 is not None:
        y = act2(y)
    if group_out:
        for g in range(o_ref.shape[0]):
            o_ref[g] = y[:, g * GROUP:(g + 1) * GROUP]
    else:
        o_ref[...] = y


def _pad_rank(w1, w2):
    r = w1.shape[1]
    rp = -(-r // LANES) * LANES
    return (jnp.pad(w1, ((0, 0), (0, rp - r))).astype(BF16),
            jnp.pad(w2, ((0, rp - r), (0, 0))).astype(BF16))


def _lora(x, w1, w2, bias, *, act1, act2, tm, group_out, name):
    t, k = x.shape
    w1, w2 = _pad_rank(w1, w2)
    rp = w1.shape[1]
    n = w2.shape[1]
    if group_out:
        out_shape = jax.ShapeDtypeStruct((n // GROUP, t, GROUP), F32)
        out_spec = pl.BlockSpec((n // GROUP, tm, GROUP), lambda i: (0, i, 0))
    else:
        out_shape = jax.ShapeDtypeStruct((t, n), F32)
        out_spec = pl.BlockSpec((tm, n), lambda i: (i, 0))
    return pl.pallas_call(
        functools.partial(_lora_kernel, act1=act1, act2=act2, group_out=group_out),
        grid=(t // tm,),
        in_specs=[pl.BlockSpec((tm, k), lambda i: (i, 0)),
                  pl.BlockSpec((k, rp), lambda i: (0, 0)),
                  pl.BlockSpec((rp, n), lambda i: (0, 0)),
                  pl.BlockSpec((1, n), lambda i: (0, 0))],
        out_specs=out_spec, out_shape=out_shape,
        compiler_params=_cparams("parallel"), name=name,
    )(x, w1, w2, bias.reshape(1, n).astype(F32))


def _shift_mix_kernel(h_ref, halo_ref, g_ref, mu_ref, *out_refs, blocks_per_seq):
    i = pl.program_id(0)
    g = g_ref[...]
    pre = _rms(h_ref[...]) * g
    halo = (_rms(halo_ref[...]) * g)[7:8, :]
    halo = jnp.where(i % blocks_per_seq == 0, 0.0, halo)
    row = lax.broadcasted_iota(jnp.int32, pre.shape, 0)
    prev = jnp.where(row == 0, halo, pltpu.roll(pre, 1, 0))
    xx = prev - pre
    for m, o_ref in enumerate(out_refs):
        o_ref[...] = (pre + xx * mu_ref[m:m + 1, :]).astype(BF16)


def _shift_mix(h, g, mu, *, lp, tm):
    t, d = h.shape
    nmix = mu.shape[0]
    row_spec = pl.BlockSpec((tm, d), lambda i: (i, 0))
    return pl.pallas_call(
        functools.partial(_shift_mix_kernel, blocks_per_seq=lp // tm),
        grid=(t // tm,),
        in_specs=[row_spec,
                  pl.BlockSpec((8, d), lambda i: (jnp.maximum(i * (tm // 8) - 1, 0), 0)),
                  pl.BlockSpec((1, d), lambda i: (0, 0)),
                  pl.BlockSpec((nmix, d), lambda i: (0, 0))],
        out_specs=[row_spec] * nmix,
        out_shape=[jax.ShapeDtypeStruct((t, d), BF16)] * nmix,
        compiler_params=_cparams("parallel"), name="rw_shift_mix",
    )(h, h, g.reshape(1, d), mu)


def _rwkv_kernel(r_ref, k_ref, v_ref, w_ref, a_ref, gate_ref, kk_ref, ka_ref, rk_ref, lg_ref, lb_ref,
                 o_ref, s_ref, *, groups):
    c_id = pl.program_id(2)

    @pl.when(c_id == 0)
    def _():
        s_ref[...] = jnp.zeros_like(s_ref)

    C, W = CHUNK, GROUP
    row = lax.broadcasted_iota(jnp.int32, (C, W), 0)
    lane = lax.broadcasted_iota(jnp.int32, (C, W), 1)
    lane_in = lane & (RW_HEAD - 1)
    lane_head = lane >> 6
    tril_incl = lane_in <= row
    tril_strict = lane_in < row
    eye = jnp.where(lane_in == row, 1.0, 0.0)
    diag_block = (lane_in >> 4) == (row >> 4)
    bd_mask = (lax.broadcasted_iota(jnp.int32, (HEADS_PER_GROUP * C, W), 0) >> 6) == \
              (lax.broadcasted_iota(jnp.int32, (HEADS_PER_GROUP * C, W), 1) >> 6)
    ones_bd = jnp.where(bd_mask, 1.0, 0.0).astype(BF16)
    ltri = jnp.where(lax.broadcasted_iota(jnp.int32, (C, C), 1) <=
                     lax.broadcasted_iota(jnp.int32, (C, C), 0), 1.0, 0.0).astype(BF16)
    gs = range(groups)

    def bd(x):
        return jnp.concatenate([x] * HEADS_PER_GROUP, axis=0) * ones_bd

    def seg_sum(x):
        res = _dot(jnp.concatenate(_split(x), axis=0), ones_bd)
        return res[:C] + res[C:]

    def head_mm(lhs, ys):
        n = len(lhs)
        his = [x[0] for x in lhs]
        r_hi = _dot(jnp.concatenate(his + [x[1] for x in lhs], axis=0), bd(ys[0]))
        r_lo = _dot(jnp.concatenate(his, axis=0), bd(ys[1]))
        return [r_hi[i * C:(i + 1) * C] + r_hi[(n + i) * C:(n + i + 1) * C] + r_lo[i * C:(i + 1) * C]
                for i in range(n)]

    r = [r_ref[g] for g in gs]
    k = [k_ref[g] for g in gs]
    v_f = [v_ref[g] for g in gs]
    v = [v_f[g].astype(BF16) for g in gs]
    a = [a_ref[g] for g in gs]
    ld = [-jnp.exp(-_softplus(-w_ref[g]) - 0.5) for g in gs]
    kk = [k[g] * kk_ref[g] for g in gs]
    kk = [kk[g] / jnp.maximum(jnp.sqrt(seg_sum(kk[g] * kk[g])), 1e-12) for g in gs]
    kh = [k[g] * (1.0 + (a[g] - 1.0) * ka_ref[g]) for g in gs]
    rb = [kk[g] * a[g] for g in gs]

    ld_s = [_split(ld[g]) for g in gs]
    cum = [_dot(ltri, ld_s[g][0]) + _dot(ltri, ld_s[g][1]) for g in gs]
    c_last = [cum[g][C - 1:C, :] for g in gs]
    e_nc = [jnp.exp(-cum[g]) for g in gs]
    e_rem = [jnp.exp(c_last[g] - cum[g]) for g in gs]
    lhs = [jnp.concatenate([-kk[g] * jnp.exp(cum[g] - ld[g]), r[g] * jnp.exp(cum[g])], axis=0).astype(BF16)
           for g in gs]
    g_b = [_dot_nt(lhs[g], bd((rb[g] * e_nc[g]).astype(BF16))) for g in gs]
    g_k = [_dot_nt(lhs[g], bd((kh[g] * e_nc[g]).astype(BF16))) for g in gs]
    n_ab = [jnp.where(tril_strict, g_b[g][:C], 0.0) for g in gs]
    b_rb = [jnp.where(tril_incl, g_b[g][C:], 0.0).astype(BF16) for g in gs]
    ak_rk = [jnp.concatenate([jnp.where(tril_strict, g_k[g][:C], 0.0),
                              jnp.where(tril_incl, g_k[g][C:], 0.0)], axis=0).astype(BF16) for g in gs]

    nd = [jnp.where(diag_block, n_ab[g], 0.0) for g in gs]
    no_s = [_split(n_ab[g] - nd[g]) for g in gs]
    pd = [eye + nd[g] for g in gs]
    x_s = [_split(nd[g]) for g in gs]
    x_s = [_split(head_mm([x_s[g]], x_s[g])[0]) for g in gs]
    for _ in range(2):
        prod = [head_mm([x_s[g], _split(pd[g])], x_s[g]) for g in gs]
        pd = [pd[g] + prod[g][1] for g in gs]
        x_s = [_split(prod[g][0]) for g in gs]
    pd = [pd[g] + head_mm([_split(pd[g])], x_s[g])[0] for g in gs]
    pd_s = [_split(pd[g]) for g in gs]
    m = [head_mm([pd_s[g]], no_s[g])[0] for g in gs]
    m_s = [_split(m[g]) for g in gs]
    m2_s = [_split(head_mm([m_s[g]], m_s[g])[0]) for g in gs]
    q = [eye + m[g] for g in gs]
    q_s = [_split(q[g] + head_mm([_split(q[g])], m2_s[g])[0]) for g in gs]
    p_s = [_split(head_mm([q_s[g]], pd_s[g])[0]) for g in gs]

    s_t = [s_ref[g] for g in gs]
    z = [_dot(lhs[g], bd(s_t[g].astype(BF16))) for g in gs]
    zv = [_dot(ak_rk[g], bd(v[g])) for g in gs]
    sa = [head_mm([p_s[g]], _split(z[g][:C] + zv[g][:C]))[0].astype(BF16) for g in gs]
    y = [z[g][C:] + zv[g][C:] + _dot(b_rb[g], bd(sa[g])) for g in gs]

    yc = [y[g] - seg_sum(y[g]) * (1.0 / RW_HEAD) for g in gs]
    var = [seg_sum(yc[g] * yc[g]) * (1.0 / RW_HEAD) for g in gs]
    bonus = [seg_sum(r[g] * kh[g] * rk_ref[g]) * v_f[g] for g in gs]
    for g in gs:
        yn = yc[g] * lax.rsqrt(var[g] + RW_LNX_EPS) * lg_ref[g] + lb_ref[g]
        o_ref[:, g * W:(g + 1) * W] = ((yn + bonus[g]) * gate_ref[g]).astype(BF16)

    upd_l = [jnp.concatenate([rb[g] * e_rem[g], kh[g] * e_rem[g]], axis=0).astype(BF16) for g in gs]
    upd_r = [jnp.concatenate([sa[g], v[g]], axis=0) for g in gs]
    full = [_dot_tn(upd_l[g], upd_r[g]) for g in gs]
    c_col = [seg_sum(eye * c_last[g]) for g in gs]
    for g in gs:
        gain = jnp.zeros((C, W), F32)
        for h in range(HEADS_PER_GROUP):
            gain = gain + jnp.where(lane_head == h, full[g][h * C:(h + 1) * C, :], 0.0)
        s_ref[g] = jnp.exp(c_col[g]) * s_t[g] + gain


def _rwkv_recurrence(r, k, v, w_raw, a, gate, k_k, k_a, r_k, lnx_g, lnx_b, *, batch, groups_per_step):
    ng, t, _ = r.shape
    chunks = t // (batch * CHUNK)
    gb = groups_per_step
    seq_spec = pl.BlockSpec((gb, CHUNK, GROUP), lambda b, g, c: (g, b * chunks + c, 0))
    par_spec = pl.BlockSpec((gb, 1, GROUP), lambda b, g, c: (g, 0, 0))
    par = lambda p: p.reshape(ng, 1, GROUP)
    return pl.pallas_call(
        functools.partial(_rwkv_kernel, groups=gb),
        grid=(batch, ng // gb, chunks),
        in_specs=[seq_spec] * 6 + [par_spec] * 5,
        out_specs=pl.BlockSpec((CHUNK, gb * GROUP), lambda b, g, c: (b * chunks + c, g)),
        out_shape=jax.ShapeDtypeStruct((t, ng * GROUP), BF16),
        scratch_shapes=[pltpu.VMEM((gb, CHUNK, GROUP), F32)],
        compiler_params=_cparams("parallel", "parallel", "arbitrary"), name="rwkv_recurrence",
    )(r, k, v, w_raw, a, gate, par(k_k), par(k_a), par(r_k), par(lnx_g), par(lnx_b))


def _post_norm_kernel(h_ref, m_ref, gp_ref, *rest):
    h = h_ref[...] + _rms(m_ref[...]) * gp_ref[...]
    if len(rest) == 1:
        rest[0][...] = h
    else:
        gn_ref, h_out, p_out = rest
        h_out[...] = h
        p_out[...] = (_rms(h) * gn_ref[...]).astype(BF16)


def _post_norm(h, m, g_post, g_next, *, tm):
    t, d = h.shape
    row_spec = pl.BlockSpec((tm, d), lambda i: (i, 0))
    par_spec = pl.BlockSpec((1, d), lambda i: (0, 0))
    h_shape = jax.ShapeDtypeStruct((t, d), F32)
    if g_next is None:
        return pl.pallas_call(
            _post_norm_kernel, grid=(t // tm,),
            in_specs=[row_spec, row_spec, par_spec], out_specs=row_spec, out_shape=h_shape,
            compiler_params=_cparams("parallel"), name="post_norm_last",
        )(h, m, g_post.reshape(1, d)), None
    return pl.pallas_call(
        _post_norm_kernel, grid=(t // tm,),
        in_specs=[row_spec, row_spec, par_spec, par_spec],
        out_specs=[row_spec, row_spec],
        out_shape=[h_shape, jax.ShapeDtypeStruct((t, d), BF16)],
        compiler_params=_cparams("parallel"), name="post_norm",
    )(h, m, g_post.reshape(1, d), g_next.reshape(1, d))


def _ffn_in_kernel(x_ref, halo_ref, wu_ref, wg_ref, cw_ref, cb_ref, wd_ref, o_ref, wd_out, *,
                   blocks_per_seq, halo_rows, sub):
    i = pl.program_id(0)
    wd_out[...] = wd_ref[...].astype(BF16)
    wu = wu_ref[...].astype(BF16)
    wg = wg_ref[...].astype(BF16)
    zh = _dot(halo_ref[...], wg)
    zh = jnp.where(i % blocks_per_seq == 0, 0.0, zh)
    prev1, prev2 = zh[halo_rows - 1:halo_rows, :], zh[halo_rows - 2:halo_rows - 1, :]
    row = lax.broadcasted_iota(jnp.int32, (sub, wu.shape[1]), 0)
    n_sub = x_ref.shape[0] // sub

    def project(s):
        xs = x_ref[s * sub:(s + 1) * sub, :]
        return _dot(xs, wu), _dot(xs, wg)

    cur = project(0)
    for s in range(n_sub):
        nxt = project(s + 1) if s + 1 < n_sub else None
        u, z = cur
        z1 = jnp.where(row == 0, prev1, pltpu.roll(z, 1, 0))
        z2 = pltpu.roll(z, 2, 0)
        z2 = jnp.where(row == 1, prev1, z2)
        z2 = jnp.where(row == 0, prev2, z2)
        zc = z * cw_ref[2:3, :] + z1 * cw_ref[1:2, :] + z2 * cw_ref[0:1, :] + cb_ref[...]
        o_ref[s * sub:(s + 1) * sub, :] = (zc * _sigmoid(zc) * u).astype(BF16)
        prev1, prev2 = z[sub - 1:sub, :], z[sub - 2:sub - 1, :]
        cur = nxt


def _ffn_in(x, w_up, w_gate, w_down, layer, conv_w, conv_b, *, lp, tm, tn):
    t, d = x.shape
    f = w_up.shape[2]
    hr = 16
    ni, nj = t // tm, f // tn
    slab = f // (ni * nj)
    assert slab * ni * nj == f and slab % hr == 0
    w_spec = pl.BlockSpec((None, d, tn), lambda i, j: (layer, 0, j))
    return pl.pallas_call(
        functools.partial(_ffn_in_kernel, blocks_per_seq=lp // tm, halo_rows=hr,
                          sub=_pick(tm, (352, 96, 64))),
        grid=(ni, nj),
        in_specs=[pl.BlockSpec((tm, d), lambda i, j: (i, 0), pipeline_mode=pl.Buffered(1)),
                  pl.BlockSpec((hr, d), lambda i, j: (jnp.maximum(i * (tm // hr) - 1, 0), 0)),
                  w_spec, w_spec,
                  pl.BlockSpec((conv_w.shape[0], tn), lambda i, j: (0, j)),
                  pl.BlockSpec((1, tn), lambda i, j: (0, j)),
                  pl.BlockSpec((None, slab, d), lambda i, j: (layer, i * nj + j, 0))],
        out_specs=[pl.BlockSpec((tm, tn), lambda i, j: (i, j)),
                   pl.BlockSpec((slab, d), lambda i, j: (i * nj + j, 0))],
        out_shape=[jax.ShapeDtypeStruct((t, f), BF16), jax.ShapeDtypeStruct((f, d), BF16)],
        compiler_params=_cparams("parallel", "arbitrary"), name="ffn_in",
    )(x, x, w_up, w_gate, conv_w, conv_b.reshape(1, f), w_down)


GLA_LEVELS = 6


def _gla_select_matrices():
    t_i = np.arange(CHUNK)[:, None]
    u_i = np.arange(CHUNK)[None, :]
    mats = [u_i <= t_i]
    for l in range(GLA_LEVELS):
        half = 1 << l
        p = (t_i & ~(2 * half - 1)) + (half - 1)
        mats.append((u_i > p) & (u_i <= t_i))
        mats.append((u_i > t_i) & (u_i <= p))
    return jnp.asarray(np.concatenate(mats, axis=0).astype(np.float32), dtype=BF16)


def _gla_kernel(sel_ref, q_ref, k_ref, v_ref, g_ref, r_ref, rb_ref, ng_ref, o_ref, s_ref, *, heads, hk, hv):
    c_id = pl.program_id(2)

    @pl.when(c_id == 0)
    def _():
        s_ref[...] = jnp.zeros_like(s_ref)

    C = CHUNK
    t_i = lax.broadcasted_iota(jnp.int32, (C, C), 0)
    u_i = lax.broadcasted_iota(jnp.int32, (C, C), 1)
    row = lax.broadcasted_iota(jnp.int32, (C, 1), 0)
    eye = t_i == u_i
    hs = range(heads)
    sel = sel_ref[...]

    q = [q_ref[:, h * hk:(h + 1) * hk] for h in hs]
    k = [k_ref[:, h * hk:(h + 1) * hk] for h in hs]
    v = [v_ref[:, h * hv:(h + 1) * hv].astype(BF16) for h in hs]
    g_s = [_split(g_ref[:, h * hk:(h + 1) * hk]) for h in hs]
    cums = [_dot(sel, g_s[h][0]) + _dot(sel, g_s[h][1]) for h in hs]
    b = [cums[h][:C] for h in hs]
    b_last = [b[h][C - 1:C, :] for h in hs]

    a = [jnp.where(eye, jnp.sum(q[h] * k[h], axis=-1, keepdims=True), 0.0) for h in hs]
    for l in range(GLA_LEVELS):
        upper = (row & (1 << l)) != 0
        same_block = (t_i >> (l + 1)) == (u_i >> (l + 1))
        ql = [jnp.where(upper, q[h] * jnp.exp(cums[h][(1 + 2 * l) * C:(2 + 2 * l) * C]), 0.0).astype(BF16)
              for h in hs]
        kl = [jnp.where(upper, 0.0, k[h] * jnp.exp(cums[h][(2 + 2 * l) * C:(3 + 2 * l) * C])).astype(BF16)
              for h in hs]
        a = [a[h] + jnp.where(same_block, _dot_nt(ql[h], kl[h]), 0.0) for h in hs]

    s_t = [s_ref[h] for h in hs]
    qe = [(q[h] * jnp.exp(b[h])).astype(BF16) for h in hs]
    kr = [(k[h] * jnp.exp(b_last[h] - b[h])).astype(BF16) for h in hs]
    o = [_dot_nt(qe[h], s_t[h].astype(BF16)) + _dot(a[h].astype(BF16), v[h]) for h in hs]
    for h in hs:
        s_ref[h] = s_t[h] * jnp.exp(b_last[h]) + _dot_tn(v[h], kr[h])
    for h in hs:
        zg = r_ref[:, h * hv:(h + 1) * hv] + rb_ref[:, h * hv:(h + 1) * hv]
        o_ref[:, h * hv:(h + 1) * hv] = (_rms(o[h]) * ng_ref[...] * (zg * _sigmoid(zg))).astype(BF16)


def _gla_recurrence(q, k, v, g, r, r_b, norm_g, *, batch, hk, hv, heads_per_step):
    t = q.shape[0]
    heads = q.shape[1] // hk
    hb = heads_per_step
    chunks = t // (batch * CHUNK)
    sel = _gla_select_matrices()
    k_spec = pl.BlockSpec((CHUNK, hb * hk), lambda b, h, c: (b * chunks + c, h))
    v_spec = pl.BlockSpec((CHUNK, hb * hv), lambda b, h, c: (b * chunks + c, h))
    return pl.pallas_call(
        functools.partial(_gla_kernel, heads=hb, hk=hk, hv=hv),
        grid=(batch, heads // hb, chunks),
        in_specs=[pl.BlockSpec(sel.shape, lambda b, h, c: (0, 0)), k_spec, k_spec, v_spec, k_spec, v_spec,
                  pl.BlockSpec((1, hb * hv), lambda b, h, c: (0, h)),
                  pl.BlockSpec((1, hv), lambda b, h, c: (0, 0))],
        out_specs=v_spec,
        out_shape=jax.ShapeDtypeStruct((t, heads * hv), BF16),
        scratch_shapes=[pltpu.VMEM((hb, hv, hk), F32)],
        compiler_params=_cparams("parallel", "parallel", "arbitrary"), name="gla_recurrence",
    )(sel, q, k, v, g, r, r_b.reshape(1, heads * hv), norm_g.reshape(1, hv))


def _tiles(lp):
    return dict(
        ew=_pick(lp, (192, 128, 64)),
        lora=_pick(lp, (352, 192, 128, 64)),
        mm=_pick(lp, (1056, 704, 352, 192, 128, 64)),
        down=_pick(lp, (704, 1056, 352, 192, 128, 64)),
        ffn=_pick(lp, (2112, 1056, 704, 352, 192, 128, 64)),
    )


def _col_tile(n, cands=(512, 256, 128)):
    return _pick(n, cands)


def _rwkv_layer(h, g_pre, p, *, batch, lp, tiles):
    d = h.shape[1]
    xr, xw, xk, xv, xa, xg = _shift_mix(h, g_pre, p["mu"], lp=lp, tm=tiles["ew"])
    tn = _col_tile(d)
    proj = lambda x, w, name: _matmul(x, w, tm=tiles["mm"], tn=tn, group_out=True, name=name)
    r = proj(xr, p["wr"], "rw_r")
    k = proj(xk, p["wk"], "rw_k")
    v = proj(xv, p["wv"], "rw_v")
    lora = functools.partial(_lora, tm=tiles["lora"], group_out=True)
    w_raw = lora(xw, p["w1"], p["w2"], p["w0"], act1=jnp.tanh, act2=None, name="rw_decay")
    a = lora(xa, p["a1"], p["a2"], p["a0"], act1=None, act2=_sigmoid, name="rw_rate")
    gate = lora(xg, p["g1"], p["g2"], jnp.zeros((d,), F32), act1=_sigmoid, act2=None, name="rw_gate")
    yo = _rwkv_recurrence(r, k, v, w_raw, a, gate, p["k_k"], p["k_a"], p["r_k"].reshape(-1),
                          p["lnx_g"], p["lnx_b"], batch=batch, groups_per_step=8)
    return _matmul(yo, p["wo"], tm=tiles["mm"], tn=tn, name="rw_o")


def _log_gate(x):
    return (jnp.minimum(x, 0.0) - jnp.log(1.0 + jnp.exp(-jnp.abs(x)))) * (1.0 / GLA_GATE_TAU)


def _gla_layer(pre, p, *, batch, tiles):
    d = pre.shape[1]
    dk = p["a2"].shape[1]
    dv = p["wo"].shape[0]
    heads = max(4, d // 512)
    hk, hv = dk // heads, dv // heads
    mm = functools.partial(_matmul, pre, p["w_in"], tm=tiles["mm"], tn=_col_tile(dk))
    q_scale = hk ** -0.5
    q = mm(cols=(0, dk), epilogue=lambda acc: acc * q_scale, name="gla_q")
    k = mm(cols=(dk, 2 * dk), name="gla_k")
    v = mm(cols=(2 * dk, 2 * dk + dv), name="gla_v")
    r = mm(cols=(2 * dk + dv, 2 * dk + 2 * dv), name="gla_r")
    g = _lora(pre, p["a1"], p["a2"], p["a_b"], act1=None, act2=_log_gate, tm=tiles["lora"],
              group_out=False, name="gla_gate")
    og = _gla_recurrence(q, k, v, g, r, p["r_b"], p["norm_g"], batch=batch, hk=hk, hv=hv, heads_per_step=4)
    return _matmul(og, p["wo"], tm=tiles["mm"], tn=_col_tile(d), name="gla_o")


def _ffn(pre, w_up, w_gate, w_down, layer, conv_w, conv_b, *, lp, tiles):
    f = w_up.shape[2]
    act, w_down_bf = _ffn_in(pre, w_up, w_gate, w_down, layer, conv_w, conv_b,
                             lp=lp, tm=tiles["ffn"], tn=_col_tile(f, (256, 128)))
    return _matmul(act, w_down_bf, tm=tiles["down"], tn=_col_tile(w_down.shape[2], (256, 128)),
                   name="ffn_down")


def kernel(x, meta, norm_g, rw_mu, rw_w0, rw_w1, rw_w2, rw_a0, rw_a1, rw_a2, rw_g1, rw_g2, rw_k_k, rw_k_a, rw_r_k, rw_wr, rw_wk, rw_wv, rw_wo, rw_lnx_g, rw_lnx_b, gla_w_in, gla_a1, gla_a2, gla_a_b, gla_r_b, gla_norm_g, gla_wo, ffn_up, ffn_gate, ffn_conv, ffn_conv_b, ffn_down):
    batch, seq, d = x.shape
    n_meta = meta.shape[0]
    depth = norm_g.shape[0]
    length = n_meta + seq
    lp = -(-length // CHUNK) * CHUNK
    m = jnp.broadcast_to(meta.astype(x.dtype)[None], (batch, n_meta, d))
    h = jnp.concatenate([m, x, jnp.zeros((batch, lp - length, d), x.dtype)], axis=1).reshape(batch * lp, d)
    tiles = _tiles(lp)

    pre = None
    for i in range(depth):
        j = i // 2
        if i % 2 == 0:
            p = dict(mu=rw_mu[j], w0=rw_w0[j], w1=rw_w1[j], w2=rw_w2[j], a0=rw_a0[j], a1=rw_a1[j],
                     a2=rw_a2[j], g1=rw_g1[j], g2=rw_g2[j], k_k=rw_k_k[j], k_a=rw_k_a[j], r_k=rw_r_k[j],
                     wr=rw_wr[j], wk=rw_wk[j], wv=rw_wv[j], wo=rw_wo[j], lnx_g=rw_lnx_g[j],
                     lnx_b=rw_lnx_b[j])
            mix = _rwkv_layer(h, norm_g[i, 0], p, batch=batch, lp=lp, tiles=tiles)
        else:
            p = dict(w_in=gla_w_in[j], a1=gla_a1[j], a2=gla_a2[j], a_b=gla_a_b[j], r_b=gla_r_b[j],
                     norm_g=gla_norm_g[j], wo=gla_wo[j])
            mix = _gla_layer(pre, p, batch=batch, tiles=tiles)
        h, pre = _post_norm(h, mix, norm_g[i, 1], norm_g[i, 2], tm=tiles["ew"])
        f = _ffn(pre, ffn_up, ffn_gate, ffn_down, i, ffn_conv[i], ffn_conv_b[i], lp=lp, tiles=tiles)
        g_next = norm_g[i + 1, 0] if i + 1 < depth else None
        h, pre = _post_norm(h, f, norm_g[i, 3], g_next, tm=tiles["ew"])
    return h.reshape(batch, lp, d)[:, n_meta:length]
```

```python
import functools

import numpy as np
import jax
import jax.numpy as jnp
from jax import lax
from jax.experimental import pallas as pl
from jax.experimental.pallas import tpu as pltpu

F32 = jnp.float32
BF16 = jnp.bfloat16

NORM_EPS = 1e-6
RW_LNX_EPS = 64e-5
RW_HEAD = 64
GLA_GATE_TAU = 16.0
CHUNK = 64
LANES = 128
GROUP = LANES
HEADS_PER_GROUP = GROUP // RW_HEAD
VMEM_LIMIT = 56 * 1024 * 1024


def _cparams(*sem):
    return pltpu.CompilerParams(dimension_semantics=sem, vmem_limit_bytes=VMEM_LIMIT)


def _pick(n, cands):
    for c in cands:
        if n % c == 0:
            return c
    raise ValueError(f"no tile for {n} in {cands}")


def _rms(x):
    return x * lax.rsqrt(jnp.mean(x * x, axis=-1, keepdims=True) + NORM_EPS)


def _softplus(z):
    return jnp.maximum(z, 0.0) + jnp.log(1.0 + jnp.exp(-jnp.abs(z)))


def _sigmoid(z):
    return 1.0 / (1.0 + jnp.exp(-z))


def _dot(a, b):
    return jnp.dot(a, b, preferred_element_type=F32)


def _dot_nt(a, b):
    return lax.dot_general(a, b, (((1,), (1,)), ((), ())), preferred_element_type=F32)


def _dot_tn(a, b):
    return lax.dot_general(a, b, (((0,), (0,)), ((), ())), preferred_element_type=F32)


def _split(x):
    hi = x.astype(BF16)
    lo = (x - hi.astype(F32)).astype(BF16)
    return hi, lo


def _mm_kernel(x_ref, w_ref, *rest, epilogue, n_extra):
    extra = rest[:n_extra]
    o_ref = rest[n_extra]
    acc = _dot(x_ref[...], w_ref[...].astype(BF16))
    if epilogue is not None:
        acc = epilogue(acc, *[e[...] for e in extra])
    o_ref[...] = acc.astype(o_ref.dtype)


def _matmul(x, w, *, tm, tn, cols=None, out_dtype=F32, epilogue=None, extras=(), name):
    t, k = x.shape
    c0, c1 = (0, w.shape[1]) if cols is None else cols
    n = c1 - c0
    off = c0 // tn
    assert c0 % tn == 0 and n % tn == 0
    grid = (t // tm, n // tn)
    in_specs = [pl.BlockSpec((tm, k), lambda i, j: (i, 0)),
                pl.BlockSpec((k, tn), lambda i, j: (0, j + off))]
    in_specs += [pl.BlockSpec((1, tn), lambda i, j: (0, j)) for _ in extras]
    return pl.pallas_call(
        functools.partial(_mm_kernel, epilogue=epilogue, n_extra=len(extras)),
        grid=grid, in_specs=in_specs,
        out_specs=pl.BlockSpec((tm, tn), lambda i, j: (i, j)),
        out_shape=jax.ShapeDtypeStruct((t, n), out_dtype),
        compiler_params=_cparams("parallel", "arbitrary"), name=name,
    )(x, w, *extras)


def _lora_kernel(x_ref, w1_ref, w2_ref, b_ref, o_ref, *, act1, act2):
    t = _dot(x_ref[...], w1_ref[...])
    if act1 is not None:
        t = act1(t)
    y = _dot(t.astype(BF16), w2_ref[...]) + b_ref[...]
    if act2 is not None:
        y = act2(y)
    o_ref[...] = y


def _pad_rank(w1, w2):
    r = w1.shape[1]
    rp = -(-r // LANES) * LANES
    return (jnp.pad(w1, ((0, 0), (0, rp - r))).astype(BF16),
            jnp.pad(w2, ((0, rp - r), (0, 0))).astype(BF16))


def _lora(x, w1, w2, bias, *, act1, act2, tm, name):
    t, k = x.shape
    w1, w2 = _pad_rank(w1, w2)
    rp = w1.shape[1]
    n = w2.shape[1]
    return pl.pallas_call(
        functools.partial(_lora_kernel, act1=act1, act2=act2),
        grid=(t // tm,),
        in_specs=[pl.BlockSpec((tm, k), lambda i: (i, 0)),
                  pl.BlockSpec((k, rp), lambda i: (0, 0)),
                  pl.BlockSpec((rp, n), lambda i: (0, 0)),
                  pl.BlockSpec((1, n), lambda i: (0, 0))],
        out_specs=pl.BlockSpec((tm, n), lambda i: (i, 0)),
        out_shape=jax.ShapeDtypeStruct((t, n), F32),
        compiler_params=_cparams("parallel"), name=name,
    )(x, w1, w2, bias.reshape(1, n).astype(F32))


def _shift_mix_kernel(h_ref, halo_ref, g_ref, mu_ref, *out_refs, blocks_per_seq):
    i = pl.program_id(0)
    g = g_ref[...]
    pre = _rms(h_ref[...]) * g
    halo = (_rms(halo_ref[...]) * g)[7:8, :]
    halo = jnp.where(i % blocks_per_seq == 0, 0.0, halo)
    row = lax.broadcasted_iota(jnp.int32, pre.shape, 0)
    prev = jnp.where(row == 0, halo, pltpu.roll(pre, 1, 0))
    xx = prev - pre
    for m, o_ref in enumerate(out_refs):
        o_ref[...] = (pre + xx * mu_ref[m:m + 1, :]).astype(BF16)


def _shift_mix(h, g, mu, *, lp, tm):
    t, d = h.shape
    nmix = mu.shape[0]
    row_spec = pl.BlockSpec((tm, d), lambda i: (i, 0))
    return pl.pallas_call(
        functools.partial(_shift_mix_kernel, blocks_per_seq=lp // tm),
        grid=(t // tm,),
        in_specs=[row_spec,
                  pl.BlockSpec((8, d), lambda i: (jnp.maximum(i * (tm // 8) - 1, 0), 0)),
                  pl.BlockSpec((1, d), lambda i: (0, 0)),
                  pl.BlockSpec((nmix, d), lambda i: (0, 0))],
        out_specs=[row_spec] * nmix,
        out_shape=[jax.ShapeDtypeStruct((t, d), BF16)] * nmix,
        compiler_params=_cparams("parallel"), name="rw_shift_mix",
    )(h, h, g.reshape(1, d), mu)


def _rwkv_kernel(r_ref, k_ref, v_ref, w_ref, a_ref, gate_ref, kk_ref, ka_ref, rk_ref, lg_ref, lb_ref,
                 o_ref, s_ref, *, groups):
    c_id = pl.program_id(2)

    @pl.when(c_id == 0)
    def _():
        s_ref[...] = jnp.zeros_like(s_ref)

    C, W = CHUNK, GROUP
    row = lax.broadcasted_iota(jnp.int32, (C, W), 0)
    lane = lax.broadcasted_iota(jnp.int32, (C, W), 1)
    lane_in = lane & (RW_HEAD - 1)
    lane_head = lane >> 6
    tril_incl = lane_in <= row
    tril_strict = lane_in < row
    eye = jnp.where(lane_in == row, 1.0, 0.0)
    diag_block = (lane_in >> 4) == (row >> 4)
    bd_mask = (lax.broadcasted_iota(jnp.int32, (HEADS_PER_GROUP * C, W), 0) >> 6) == \
              (lax.broadcasted_iota(jnp.int32, (HEADS_PER_GROUP * C, W), 1) >> 6)
    ones_bd = jnp.where(bd_mask, 1.0, 0.0).astype(BF16)
    ones_bd2 = jnp.concatenate([ones_bd, ones_bd], axis=0)
    ltri = jnp.where(lax.broadcasted_iota(jnp.int32, (C, C), 1) <=
                     lax.broadcasted_iota(jnp.int32, (C, C), 0), 1.0, 0.0).astype(BF16)
    gs = range(groups)
    cols = lambda ref, g: ref[:, g * W:(g + 1) * W]

    def bd(x):
        return jnp.concatenate([x] * HEADS_PER_GROUP, axis=0) * ones_bd

    def seg_sum(x):
        return _dot(jnp.concatenate(_split(x), axis=1), ones_bd2)

    def head_mm(lhs, ys):
        top = jnp.concatenate([bd(ys[0]), bd(ys[1])], axis=1)
        res = _dot(jnp.concatenate([jnp.concatenate(x, axis=1) for x in lhs], axis=0),
                   jnp.concatenate([top, top], axis=0))
        return [res[i * C:(i + 1) * C, :W] + res[i * C:(i + 1) * C, W:] for i in range(len(lhs))]

    r = [cols(r_ref, g) for g in gs]
    k = [cols(k_ref, g) for g in gs]
    v_f = [cols(v_ref, g) for g in gs]
    v = [v_f[g].astype(BF16) for g in gs]
    a = [cols(a_ref, g) for g in gs]
    ld = [-jnp.exp(-_softplus(-cols(w_ref, g)) - 0.5) for g in gs]
    kk = [k[g] * cols(kk_ref, g) for g in gs]
    kk = [kk[g] / jnp.maximum(jnp.sqrt(seg_sum(kk[g] * kk[g])), 1e-12) for g in gs]
    kh = [k[g] * (1.0 + (a[g] - 1.0) * cols(ka_ref, g)) for g in gs]
    rb = [kk[g] * a[g] for g in gs]

    cum2 = [_dot(ltri, jnp.concatenate(_split(ld[g]), axis=1)) for g in gs]
    cum = [cum2[g][:, :W] + cum2[g][:, W:] for g in gs]
    c_last = [cum[g][C - 1:C, :] for g in gs]
    e_nc = [jnp.exp(-cum[g]) for g in gs]
    e_rem = [jnp.exp(c_last[g] - cum[g]) for g in gs]
    lhs = [jnp.concatenate([-kk[g] * jnp.exp(cum[g] - ld[g]), r[g] * jnp.exp(cum[g])], axis=0).astype(BF16)
           for g in gs]
    gram = [_dot_nt(lhs[g], jnp.concatenate([bd((rb[g] * e_nc[g]).astype(BF16)),
                                             bd((kh[g] * e_nc[g]).astype(BF16))], axis=0)) for g in gs]
    n_ab = [jnp.where(tril_strict, gram[g][:C, :W], 0.0) for g in gs]
    b_rb = [jnp.where(tril_incl, gram[g][C:, :W], 0.0).astype(BF16) for g in gs]
    ak_rk = [jnp.concatenate([jnp.where(tril_strict, gram[g][:C, W:], 0.0),
                              jnp.where(tril_incl, gram[g][C:, W:], 0.0)], axis=0).astype(BF16) for g in gs]

    nd = [jnp.where(diag_block, n_ab[g], 0.0) for g in gs]
    no_s = [_split(n_ab[g] - nd[g]) for g in gs]
    pd = [eye + nd[g] for g in gs]
    x_s = [_split(nd[g]) for g in gs]
    x_s = [_split(head_mm([x_s[g]], x_s[g])[0]) for g in gs]
    for _ in range(2):
        prod = [head_mm([x_s[g], _split(pd[g])], x_s[g]) for g in gs]
        pd = [pd[g] + prod[g][1] for g in gs]
        x_s = [_split(prod[g][0]) for g in gs]
    pd = [pd[g] + head_mm([_split(pd[g])], x_s[g])[0] for g in gs]
    pd_s = [_split(pd[g]) for g in gs]
    m = [head_mm([pd_s[g]], no_s[g])[0] for g in gs]
    m_s = [_split(m[g]) for g in gs]
    m2_s = [_split(head_mm([m_s[g]], m_s[g])[0]) for g in gs]
    q = [eye + m[g] for g in gs]
    q_s = [_split(q[g] + head_mm([_split(q[g])], m2_s[g])[0]) for g in gs]
    p_s = [_split(head_mm([q_s[g]], pd_s[g])[0]) for g in gs]

    s_t = [s_ref[g] for g in gs]
    zz = [_dot(jnp.concatenate([lhs[g], ak_rk[g]], axis=1),
               jnp.concatenate([bd(s_t[g].astype(BF16)), bd(v[g])], axis=0)) for g in gs]
    sa = [head_mm([p_s[g]], _split(zz[g][:C]))[0].astype(BF16) for g in gs]
    y = [zz[g][C:] + _dot(b_rb[g], bd(sa[g])) for g in gs]

    yc = [y[g] - seg_sum(y[g]) * (1.0 / RW_HEAD) for g in gs]
    var = [seg_sum(yc[g] * yc[g]) * (1.0 / RW_HEAD) for g in gs]
    bonus = [seg_sum(r[g] * kh[g] * cols(rk_ref, g)) * v_f[g] for g in gs]
    for g in gs:
        yn = yc[g] * lax.rsqrt(var[g] + RW_LNX_EPS) * cols(lg_ref, g) + cols(lb_ref, g)
        o_ref[:, g * W:(g + 1) * W] = ((yn + bonus[g]) * cols(gate_ref, g)).astype(BF16)

    upd_l = [jnp.concatenate([rb[g] * e_rem[g], kh[g] * e_rem[g]], axis=0).astype(BF16) for g in gs]
    upd_r = [jnp.concatenate([sa[g], v[g]], axis=0) for g in gs]
    full = [_dot_tn(upd_l[g], upd_r[g]) for g in gs]
    c_col = [seg_sum(eye * c_last[g]) for g in gs]
    for g in gs:
        gain = jnp.zeros((C, W), F32)
        for h in range(HEADS_PER_GROUP):
            gain = gain + jnp.where(lane_head == h, full[g][h * C:(h + 1) * C, :], 0.0)
        s_ref[g] = jnp.exp(c_col[g]) * s_t[g] + gain


def _rwkv_recurrence(r, k, v, w_raw, a, gate, k_k, k_a, r_k, lnx_g, lnx_b, *, batch, groups_per_step):
    t, d = r.shape
    chunks = t // (batch * CHUNK)
    gb = min(groups_per_step, d // GROUP)
    wb = gb * GROUP
    seq_spec = pl.BlockSpec((CHUNK, wb), lambda b, g, c: (b * chunks + c, g))
    par_spec = pl.BlockSpec((1, wb), lambda b, g, c: (0, g))
    par = lambda p: p.reshape(1, d)
    return pl.pallas_call(
        functools.partial(_rwkv_kernel, groups=gb),
        grid=(batch, d // wb, chunks),
        in_specs=[seq_spec] * 6 + [par_spec] * 5,
        out_specs=seq_spec,
        out_shape=jax.ShapeDtypeStruct((t, d), BF16),
        scratch_shapes=[pltpu.VMEM((gb, CHUNK, GROUP), F32)],
        compiler_params=_cparams("parallel", "parallel", "arbitrary"), name="rwkv_recurrence",
    )(r, k, v, w_raw, a, gate, par(k_k), par(k_a), par(r_k), par(lnx_g), par(lnx_b))


def _post_norm_kernel(h_ref, m_ref, gp_ref, *rest):
    h = h_ref[...] + _rms(m_ref[...].astype(F32)) * gp_ref[...]
    if len(rest) == 1:
        rest[0][...] = h
    else:
        gn_ref, h_out, p_out = rest
        h_out[...] = h
        p_out[...] = (_rms(h) * gn_ref[...]).astype(BF16)


def _post_norm(h, m, g_post, g_next, *, tm):
    t, d = h.shape
    row_spec = pl.BlockSpec((tm, d), lambda i: (i, 0))
    par_spec = pl.BlockSpec((1, d), lambda i: (0, 0))
    h_shape = jax.ShapeDtypeStruct((t, d), F32)
    if g_next is None:
        return pl.pallas_call(
            _post_norm_kernel, grid=(t // tm,),
            in_specs=[row_spec, row_spec, par_spec], out_specs=row_spec, out_shape=h_shape,
            compiler_params=_cparams("parallel"), name="post_norm_last",
        )(h, m, g_post.reshape(1, d)), None
    return pl.pallas_call(
        _post_norm_kernel, grid=(t // tm,),
        in_specs=[row_spec, row_spec, par_spec, par_spec],
        out_specs=[row_spec, row_spec],
        out_shape=[h_shape, jax.ShapeDtypeStruct((t, d), BF16)],
        compiler_params=_cparams("parallel"), name="post_norm",
    )(h, m, g_post.reshape(1, d), g_next.reshape(1, d))


def _ffn_in_kernel(x_ref, halo_ref, wu_ref, wg_ref, cw_ref, cb_ref, wd_ref, o_ref, wd_out, *,
                   blocks_per_seq, halo_rows, sub):
    i = pl.program_id(0)
    wd_out[...] = wd_ref[...].astype(BF16)
    wu = wu_ref[...].astype(BF16)
    wg = wg_ref[...].astype(BF16)
    zh = _dot(halo_ref[...], wg)
    zh = jnp.where(i % blocks_per_seq == 0, 0.0, zh)
    prev1, prev2 = zh[halo_rows - 1:halo_rows, :], zh[halo_rows - 2:halo_rows - 1, :]
    row = lax.broadcasted_iota(jnp.int32, (sub, wu.shape[1]), 0)
    n_sub = x_ref.shape[0] // sub

    def project(s):
        xs = x_ref[s * sub:(s + 1) * sub, :]
        return _dot(xs, wu), _dot(xs, wg)

    cur = project(0)
    for s in range(n_sub):
        nxt = project(s + 1) if s + 1 < n_sub else None
        u, z = cur
        z1 = jnp.where(row == 0, prev1, pltpu.roll(z, 1, 0))
        z2 = pltpu.roll(z, 2, 0)
        z2 = jnp.where(row == 1, prev1, z2)
        z2 = jnp.where(row == 0, prev2, z2)
        zc = z * cw_ref[2:3, :] + z1 * cw_ref[1:2, :] + z2 * cw_ref[0:1, :] + cb_ref[...]
        o_ref[s * sub:(s + 1) * sub, :] = (zc * _sigmoid(zc) * u).astype(BF16)
        prev1, prev2 = z[sub - 1:sub, :], z[sub - 2:sub - 1, :]
        cur = nxt


def _ffn_in(x, w_up, w_gate, w_down, layer, conv_w, conv_b, *, lp, tm, tn):
    t, d = x.shape
    f = w_up.shape[2]
    hr = 16
    ni, nj = t // tm, f // tn
    slab = f // (ni * nj)
    assert slab * ni * nj == f and slab % hr == 0
    w_spec = pl.BlockSpec((None, d, tn), lambda i, j: (layer, 0, j))
    return pl.pallas_call(
        functools.partial(_ffn_in_kernel, blocks_per_seq=lp // tm, halo_rows=hr,
                          sub=_pick(tm, (352, 96, 64))),
        grid=(ni, nj),
        in_specs=[pl.BlockSpec((tm, d), lambda i, j: (i, 0), pipeline_mode=pl.Buffered(1)),
                  pl.BlockSpec((hr, d), lambda i, j: (jnp.maximum(i * (tm // hr) - 1, 0), 0)),
                  w_spec, w_spec,
                  pl.BlockSpec((conv_w.shape[0], tn), lambda i, j: (0, j)),
                  pl.BlockSpec((1, tn), lambda i, j: (0, j)),
                  pl.BlockSpec((None, slab, d), lambda i, j: (layer, i * nj + j, 0))],
        out_specs=[pl.BlockSpec((tm, tn), lambda i, j: (i, j)),
                   pl.BlockSpec((slab, d), lambda i, j: (i * nj + j, 0))],
        out_shape=[jax.ShapeDtypeStruct((t, f), BF16), jax.ShapeDtypeStruct((f, d), BF16)],
        compiler_params=_cparams("parallel", "arbitrary"), name="ffn_in",
    )(x, x, w_up, w_gate, conv_w, conv_b.reshape(1, f), w_down)


GLA_LEVELS = 6


def _gla_select_matrices():
    t_i = np.arange(CHUNK)[:, None]
    u_i = np.arange(CHUNK)[None, :]
    mats = [u_i <= t_i]
    for l in range(GLA_LEVELS):
        half = 1 << l
        p = (t_i & ~(2 * half - 1)) + (half - 1)
        mats.append((u_i > p) & (u_i <= t_i))
        mats.append((u_i > t_i) & (u_i <= p))
    return jnp.asarray(np.concatenate(mats, axis=0).astype(np.float32), dtype=BF16)


def _gla_kernel(sel_ref, q_ref, k_ref, v_ref, g_ref, r_ref, rb_ref, ng_ref, o_ref, s_ref, *, heads, hk, hv):
    c_id = pl.program_id(2)

    @pl.when(c_id == 0)
    def _():
        s_ref[...] = jnp.zeros_like(s_ref)

    C = CHUNK
    t_i = lax.broadcasted_iota(jnp.int32, (C, C), 0)
    u_i = lax.broadcasted_iota(jnp.int32, (C, C), 1)
    row = lax.broadcasted_iota(jnp.int32, (C, 1), 0)
    eye = t_i == u_i
    hs = range(heads)
    sel = sel_ref[...]

    q = [q_ref[:, h * hk:(h + 1) * hk] for h in hs]
    k = [k_ref[:, h * hk:(h + 1) * hk] for h in hs]
    v = [v_ref[:, h * hv:(h + 1) * hv].astype(BF16) for h in hs]
    g_s = [_split(g_ref[:, h * hk:(h + 1) * hk]) for h in hs]
    cums = [_dot(sel, g_s[h][0]) + _dot(sel, g_s[h][1]) for h in hs]
    b = [cums[h][:C] for h in hs]
    b_last = [b[h][C - 1:C, :] for h in hs]

    a = [jnp.where(eye, jnp.sum(q[h] * k[h], axis=-1, keepdims=True), 0.0) for h in hs]
    for l in range(GLA_LEVELS):
        upper = (row & (1 << l)) != 0
        same_block = (t_i >> (l + 1)) == (u_i >> (l + 1))
        ql = [jnp.where(upper, q[h] * jnp.exp(cums[h][(1 + 2 * l) * C:(2 + 2 * l) * C]), 0.0).astype(BF16)
              for h in hs]
        kl = [jnp.where(upper, 0.0, k[h] * jnp.exp(cums[h][(2 + 2 * l) * C:(3 + 2 * l) * C])).astype(BF16)
              for h in hs]
        a = [a[h] + jnp.where(same_block, _dot_nt(ql[h], kl[h]), 0.0) for h in hs]

    s_t = [s_ref[h] for h in hs]
    qe = [(q[h] * jnp.exp(b[h])).astype(BF16) for h in hs]
    kr = [(k[h] * jnp.exp(b_last[h] - b[h])).astype(BF16) for h in hs]
    o = [_dot_nt(qe[h], s_t[h].astype(BF16)) + _dot(a[h].astype(BF16), v[h]) for h in hs]
    for h in hs:
        s_ref[h] = s_t[h] * jnp.exp(b_last[h]) + _dot_tn(v[h], kr[h])
    for h in hs:
        zg = r_ref[:, h * hv:(h + 1) * hv] + rb_ref[:, h * hv:(h + 1) * hv]
        o_ref[:, h * hv:(h + 1) * hv] = (_rms(o[h]) * ng_ref[...] * (zg * _sigmoid(zg))).astype(BF16)


def _gla_recurrence(q, k, v, g, r, r_b, norm_g, *, batch, hk, hv, heads_per_step):
    t = q.shape[0]
    heads = q.shape[1] // hk
    hb = heads_per_step
    chunks = t // (batch * CHUNK)
    sel = _gla_select_matrices()
    k_spec = pl.BlockSpec((CHUNK, hb * hk), lambda b, h, c: (b * chunks + c, h))
    v_spec = pl.BlockSpec((CHUNK, hb * hv), lambda b, h, c: (b * chunks + c, h))
    return pl.pallas_call(
        functools.partial(_gla_kernel, heads=hb, hk=hk, hv=hv),
        grid=(batch, heads // hb, chunks),
        in_specs=[pl.BlockSpec(sel.shape, lambda b, h, c: (0, 0)), k_spec, k_spec, v_spec, k_spec, v_spec,
                  pl.BlockSpec((1, hb * hv), lambda b, h, c: (0, h)),
                  pl.BlockSpec((1, hv), lambda b, h, c: (0, 0))],
        out_specs=v_spec,
        out_shape=jax.ShapeDtypeStruct((t, heads * hv), BF16),
        scratch_shapes=[pltpu.VMEM((hb, hv, hk), F32)],
        compiler_params=_cparams("parallel", "parallel", "arbitrary"), name="gla_recurrence",
    )(sel, q, k, v, g, r, r_b.reshape(1, heads * hv), norm_g.reshape(1, hv))


def _tiles(lp):
    return dict(
        ew=_pick(lp, (192, 128, 64)),
        lora=_pick(lp, (352, 192, 128, 64)),
        mm=_pick(lp, (1056, 704, 352, 192, 128, 64)),
        down=_pick(lp, (704, 1056, 352, 192, 128, 64)),
        ffn=_pick(lp, (2112, 1056, 704, 352, 192, 128, 64)),
    )


def _col_tile(n, cands=(512, 256, 128)):
    return _pick(n, cands)


def _rwkv_layer(h, g_pre, p, *, batch, lp, tiles):
    d = h.shape[1]
    xr, xw, xk, xv, xa, xg = _shift_mix(h, g_pre, p["mu"], lp=lp, tm=tiles["ew"])
    tn = _col_tile(d)
    proj = lambda x, w, name: _matmul(x, w, tm=tiles["mm"], tn=tn, name=name)
    r = proj(xr, p["wr"], "rw_r")
    k = proj(xk, p["wk"], "rw_k")
    v = proj(xv, p["wv"], "rw_v")
    lora = functools.partial(_lora, tm=tiles["lora"])
    w_raw = lora(xw, p["w1"], p["w2"], p["w0"], act1=jnp.tanh, act2=None, name="rw_decay")
    a = lora(xa, p["a1"], p["a2"], p["a0"], act1=None, act2=_sigmoid, name="rw_rate")
    gate = lora(xg, p["g1"], p["g2"], jnp.zeros((d,), F32), act1=_sigmoid, act2=None, name="rw_gate")
    yo = _rwkv_recurrence(r, k, v, w_raw, a, gate, p["k_k"], p["k_a"], p["r_k"].reshape(-1),
                          p["lnx_g"], p["lnx_b"], batch=batch, groups_per_step=16)
    return _matmul(yo, p["wo"], tm=tiles["mm"], tn=tn, out_dtype=BF16, name="rw_o")


def _log_gate(x):
    return (jnp.minimum(x, 0.0) - jnp.log(1.0 + jnp.exp(-jnp.abs(x)))) * (1.0 / GLA_GATE_TAU)


def _gla_layer(pre, p, *, batch, tiles):
    d = pre.shape[1]
    dk = p["a2"].shape[1]
    dv = p["wo"].shape[0]
    heads = max(4, d // 512)
    hk, hv = dk // heads, dv // heads
    mm = functools.partial(_matmul, pre, p["w_in"], tm=tiles["mm"], tn=_col_tile(dk))
    q_scale = hk ** -0.5
    q = mm(cols=(0, dk), epilogue=lambda acc: acc * q_scale, name="gla_q")
    k = mm(cols=(dk, 2 * dk), name="gla_k")
    v = mm(cols=(2 * dk, 2 * dk + dv), name="gla_v")
    r = mm(cols=(2 * dk + dv, 2 * dk + 2 * dv), name="gla_r")
    g = _lora(pre, p["a1"], p["a2"], p["a_b"], act1=None, act2=_log_gate, tm=tiles["lora"], name="gla_gate")
    og = _gla_recurrence(q, k, v, g, r, p["r_b"], p["norm_g"], batch=batch, hk=hk, hv=hv, heads_per_step=4)
    return _matmul(og, p["wo"], tm=tiles["mm"], tn=_col_tile(d), out_dtype=BF16, name="gla_o")


def _ffn(pre, w_up, w_gate, w_down, layer, conv_w, conv_b, *, lp, tiles):
    f = w_up.shape[2]
    act, w_down_bf = _ffn_in(pre, w_up, w_gate, w_down, layer, conv_w, conv_b,
                             lp=lp, tm=tiles["ffn"], tn=_col_tile(f, (256, 128)))
    return _matmul(act, w_down_bf, tm=tiles["down"], tn=_col_tile(w_down.shape[2], (256, 128)),
                   out_dtype=BF16, name="ffn_down")


def kernel(x, meta, norm_g, rw_mu, rw_w0, rw_w1, rw_w2, rw_a0, rw_a1, rw_a2, rw_g1, rw_g2, rw_k_k, rw_k_a, rw_r_k, rw_wr, rw_wk, rw_wv, rw_wo, rw_lnx_g, rw_lnx_b, gla_w_in, gla_a1, gla_a2, gla_a_b, gla_r_b, gla_norm_g, gla_wo, ffn_up, ffn_gate, ffn_conv, ffn_conv_b, ffn_down):
    batch, seq, d = x.shape
    n_meta = meta.shape[0]
    depth = norm_g.shape[0]
    length = n_meta + seq
    lp = -(-length // CHUNK) * CHUNK
    m = jnp.broadcast_to(meta.astype(x.dtype)[None], (batch, n_meta, d))
    h = jnp.concatenate([m, x, jnp.zeros((batch, lp - length, d), x.dtype)], axis=1).reshape(batch * lp, d)
    tiles = _tiles(lp)

    pre = None
    for i in range(depth):
        j = i // 2
        if i % 2 == 0:
            p = dict(mu=rw_mu[j], w0=rw_w0[j], w1=rw_w1[j], w2=rw_w2[j], a0=rw_a0[j], a1=rw_a1[j],
                     a2=rw_a2[j], g1=rw_g1[j], g2=rw_g2[j], k_k=rw_k_k[j], k_a=rw_k_a[j], r_k=rw_r_k[j],
                     wr=rw_wr[j], wk=rw_wk[j], wv=rw_wv[j], wo=rw_wo[j], lnx_g=rw_lnx_g[j],
                     lnx_b=rw_lnx_b[j])
            mix = _rwkv_layer(h, norm_g[i, 0], p, batch=batch, lp=lp, tiles=tiles)
        else:
            p = dict(w_in=gla_w_in[j], a1=gla_a1[j], a2=gla_a2[j], a_b=gla_a_b[j], r_b=gla_r_b[j],
                     norm_g=gla_norm_g[j], wo=gla_wo[j])
            mix = _gla_layer(pre, p, batch=batch, tiles=tiles)
        h, pre = _post_norm(h, mix, norm_g[i, 1], norm_g[i, 2], tm=tiles["ew"])
        f = _ffn(pre, ffn_up, ffn_gate, ffn_down, i, ffn_conv[i], ffn_conv_b[i], lp=lp, tiles=tiles)
        g_next = norm_g[i + 1, 0] if i + 1 < depth else None
        h, pre = _post_norm(h, f, norm_g[i, 3], g_next, tm=tiles["ew"])
    return h.reshape(batch, lp, d)[:, n_meta:length]
```

```python
import functools

import numpy as np
import jax
import jax.numpy as jnp
from jax import lax
from jax.experimental import pallas as pl
from jax.experimental.pallas import tpu as pltpu

F32 = jnp.float32
BF16 = jnp.bfloat16

NORM_EPS = 1e-6
RW_LNX_EPS = 64e-5
RW_HEAD = 64
DECAY_SCALE = float(np.exp(-0.5))
GLA_GATE_TAU = 16.0
CHUNK = 64
LANES = 128
GROUP = LANES
HEADS_PER_GROUP = GROUP // RW_HEAD
VMEM_LIMIT = 56 * 1024 * 1024


def _cparams(*sem):
    return pltpu.CompilerParams(dimension_semantics=sem, vmem_limit_bytes=VMEM_LIMIT)


def _pick(n, cands):
    for c in cands:
        if n % c == 0:
            return c
    raise ValueError(f"no tile for {n} in {cands}")


def _rms(x):
    return x * lax.rsqrt(jnp.mean(x * x, axis=-1, keepdims=True) + NORM_EPS)


def _sigmoid(z):
    return 1.0 / (1.0 + jnp.exp(-z))


def _dot(a, b):
    return jnp.dot(a, b, preferred_element_type=F32)


def _dot_nt(a, b):
    return lax.dot_general(a, b, (((1,), (1,)), ((), ())), preferred_element_type=F32)


def _dot_tn(a, b):
    return lax.dot_general(a, b, (((0,), (0,)), ((), ())), preferred_element_type=F32)


def _split(x):
    hi = x.astype(BF16)
    lo = (x - hi.astype(F32)).astype(BF16)
    return hi, lo


def _mm_kernel(x_ref, w_ref, *rest, epilogue, n_extra):
    extra = rest[:n_extra]
    o_ref = rest[n_extra]
    acc = _dot(x_ref[...], w_ref[...].astype(BF16))
    if epilogue is not None:
        acc = epilogue(acc, *[e[...] for e in extra])
    o_ref[...] = acc.astype(o_ref.dtype)


def _matmul(x, w, *, tm, tn, cols=None, out_dtype=F32, epilogue=None, extras=(), name):
    t, k = x.shape
    c0, c1 = (0, w.shape[1]) if cols is None else cols
    n = c1 - c0
    off = c0 // tn
    assert c0 % tn == 0 and n % tn == 0
    grid = (t // tm, n // tn)
    in_specs = [pl.BlockSpec((tm, k), lambda i, j: (i, 0)),
                pl.BlockSpec((k, tn), lambda i, j: (0, j + off))]
    in_specs += [pl.BlockSpec((1, tn), lambda i, j: (0, j)) for _ in extras]
    return pl.pallas_call(
        functools.partial(_mm_kernel, epilogue=epilogue, n_extra=len(extras)),
        grid=grid, in_specs=in_specs,
        out_specs=pl.BlockSpec((tm, tn), lambda i, j: (i, j)),
        out_shape=jax.ShapeDtypeStruct((t, n), out_dtype),
        compiler_params=_cparams("parallel", "arbitrary"), name=name,
    )(x, w, *extras)


def _lora_kernel(x_ref, w1_ref, w2_ref, b_ref, o_ref, *, act1, act2):
    t = _dot(x_ref[...], w1_ref[...])
    if act1 is not None:
        t = act1(t)
    y = _dot(t.astype(BF16), w2_ref[...]) + b_ref[...]
    if act2 is not None:
        y = act2(y)
    o_ref[...] = y


def _pad_rank(w1, w2):
    r = w1.shape[1]
    rp = -(-r // LANES) * LANES
    return (jnp.pad(w1, ((0, 0), (0, rp - r))).astype(BF16),
            jnp.pad(w2, ((0, rp - r), (0, 0))).astype(BF16))


def _lora(x, w1, w2, bias, *, act1, act2, tm, name):
    t, k = x.shape
    w1, w2 = _pad_rank(w1, w2)
    rp = w1.shape[1]
    n = w2.shape[1]
    return pl.pallas_call(
        functools.partial(_lora_kernel, act1=act1, act2=act2),
        grid=(t // tm,),
        in_specs=[pl.BlockSpec((tm, k), lambda i: (i, 0)),
                  pl.BlockSpec((k, rp), lambda i: (0, 0)),
                  pl.BlockSpec((rp, n), lambda i: (0, 0)),
                  pl.BlockSpec((1, n), lambda i: (0, 0))],
        out_specs=pl.BlockSpec((tm, n), lambda i: (i, 0)),
        out_shape=jax.ShapeDtypeStruct((t, n), F32),
        compiler_params=_cparams("parallel"), name=name,
    )(x, w1, w2, bias.reshape(1, n).astype(F32))


def _shift_mix_kernel(h_ref, halo_ref, g_ref, mu_ref, *out_refs, blocks_per_seq):
    i = pl.program_id(0)
    g = g_ref[...]
    pre = _rms(h_ref[...]) * g
    halo = (_rms(halo_ref[...]) * g)[7:8, :]
    halo = jnp.where(i % blocks_per_seq == 0, 0.0, halo)
    row = lax.broadcasted_iota(jnp.int32, pre.shape, 0)
    prev = jnp.where(row == 0, halo, pltpu.roll(pre, 1, 0))
    xx = prev - pre
    for m, o_ref in enumerate(out_refs):
        o_ref[...] = (pre + xx * mu_ref[m:m + 1, :]).astype(BF16)


def _shift_mix(h, g, mu, *, lp, tm):
    t, d = h.shape
    nmix = mu.shape[0]
    row_spec = pl.BlockSpec((tm, d), lambda i: (i, 0))
    return pl.pallas_call(
        functools.partial(_shift_mix_kernel, blocks_per_seq=lp // tm),
        grid=(t // tm,),
        in_specs=[row_spec,
                  pl.BlockSpec((8, d), lambda i: (jnp.maximum(i * (tm // 8) - 1, 0), 0)),
                  pl.BlockSpec((1, d), lambda i: (0, 0)),
                  pl.BlockSpec((nmix, d), lambda i: (0, 0))],
        out_specs=[row_spec] * nmix,
        out_shape=[jax.ShapeDtypeStruct((t, d), BF16)] * nmix,
        compiler_params=_cparams("parallel"), name="rw_shift_mix",
    )(h, h, g.reshape(1, d), mu)


def _rwkv_kernel(r_ref, k_ref, v_ref, w_ref, a_ref, gate_ref, kk_ref, ka_ref, rk_ref, lg_ref, lb_ref,
                 o_ref, s_ref, *, groups):
    c_id = pl.program_id(2)

    @pl.when(c_id == 0)
    def _():
        s_ref[...] = jnp.zeros_like(s_ref)

    C, W = CHUNK, GROUP
    row = lax.broadcasted_iota(jnp.int32, (C, W), 0)
    lane = lax.broadcasted_iota(jnp.int32, (C, W), 1)
    lane_in = lane & (RW_HEAD - 1)
    lane_head = lane >> 6
    tril_incl = lane_in <= row
    tril_strict = lane_in < row
    eye = jnp.where(lane_in == row, 1.0, 0.0)
    diag_block = (lane_in >> 4) == (row >> 4)
    bd_mask = (lax.broadcasted_iota(jnp.int32, (HEADS_PER_GROUP * C, W), 0) >> 6) == \
              (lax.broadcasted_iota(jnp.int32, (HEADS_PER_GROUP * C, W), 1) >> 6)
    ones_bd = jnp.where(bd_mask, 1.0, 0.0).astype(BF16)
    ones_bd2 = jnp.concatenate([ones_bd, ones_bd], axis=0)
    ltri = jnp.where(lax.broadcasted_iota(jnp.int32, (C, C), 1) <=
                     lax.broadcasted_iota(jnp.int32, (C, C), 0), 1.0, 0.0).astype(BF16)
    gs = range(groups)
    cols = lambda ref, g: ref[:, g * W:(g + 1) * W]

    def bd(x):
        return jnp.concatenate([x] * HEADS_PER_GROUP, axis=0) * ones_bd

    def seg_sum(x):
        return _dot(jnp.concatenate(_split(x), axis=1), ones_bd2)

    def head_mm(lhs, ys):
        top = jnp.concatenate([bd(ys[0]), bd(ys[1])], axis=1)
        res = _dot(jnp.concatenate([jnp.concatenate(x, axis=1) for x in lhs], axis=0),
                   jnp.concatenate([top, top], axis=0))
        return [res[i * C:(i + 1) * C, :W] + res[i * C:(i + 1) * C, W:] for i in range(len(lhs))]

    r = [cols(r_ref, g) for g in gs]
    k = [cols(k_ref, g) for g in gs]
    v_f = [cols(v_ref, g) for g in gs]
    v = [v_f[g].astype(BF16) for g in gs]
    a = [cols(a_ref, g) for g in gs]
    ld = [-DECAY_SCALE * _sigmoid(cols(w_ref, g)) for g in gs]
    kk = [k[g] * cols(kk_ref, g) for g in gs]
    kk = [kk[g] * lax.rsqrt(jnp.maximum(seg_sum(kk[g] * kk[g]), 1e-24)) for g in gs]
    kh = [k[g] * (1.0 + (a[g] - 1.0) * cols(ka_ref, g)) for g in gs]
    rb = [kk[g] * a[g] for g in gs]

    cum2 = [_dot(ltri, jnp.concatenate(_split(ld[g]), axis=1)) for g in gs]
    cum = [cum2[g][:, :W] + cum2[g][:, W:] for g in gs]
    c_last = [cum[g][C - 1:C, :] for g in gs]
    e_nc = [jnp.exp(-cum[g]) for g in gs]
    e_rem = [jnp.exp(c_last[g] - cum[g]) for g in gs]
    lhs = [jnp.concatenate([-kk[g] * jnp.exp(cum[g] - ld[g]), r[g] * jnp.exp(cum[g])], axis=0).astype(BF16)
           for g in gs]
    gram = [_dot_nt(lhs[g], jnp.concatenate([bd((rb[g] * e_nc[g]).astype(BF16)),
                                             bd((kh[g] * e_nc[g]).astype(BF16))], axis=0)) for g in gs]
    n_ab = [jnp.where(tril_strict, gram[g][:C, :W], 0.0) for g in gs]
    b_rb = [jnp.where(tril_incl, gram[g][C:, :W], 0.0).astype(BF16) for g in gs]
    ak_rk = [jnp.concatenate([jnp.where(tril_strict, gram[g][:C, W:], 0.0),
                              jnp.where(tril_incl, gram[g][C:, W:], 0.0)], axis=0).astype(BF16) for g in gs]

    nd = [jnp.where(diag_block, n_ab[g], 0.0) for g in gs]
    no_s = [_split(n_ab[g] - nd[g]) for g in gs]
    pd = [eye + nd[g] for g in gs]
    x_s = [_split(nd[g]) for g in gs]
    x_s = [_split(head_mm([x_s[g]], x_s[g])[0]) for g in gs]
    for _ in range(2):
        prod = [head_mm([x_s[g], _split(pd[g])], x_s[g]) for g in gs]
        pd = [pd[g] + prod[g][1] for g in gs]
        x_s = [_split(prod[g][0]) for g in gs]
    pd = [pd[g] + head_mm([_split(pd[g])], x_s[g])[0] for g in gs]
    pd_s = [_split(pd[g]) for g in gs]
    m = [head_mm([pd_s[g]], no_s[g])[0] for g in gs]
    m_s = [_split(m[g]) for g in gs]
    m2_s = [_split(head_mm([m_s[g]], m_s[g])[0]) for g in gs]
    q = [eye + m[g] for g in gs]
    q_s = [_split(q[g] + head_mm([_split(q[g])], m2_s[g])[0]) for g in gs]
    p_s = [_split(head_mm([q_s[g]], pd_s[g])[0]) for g in gs]

    s_t = [s_ref[g] for g in gs]
    zz = [_dot(jnp.concatenate([lhs[g], ak_rk[g]], axis=1),
               jnp.concatenate([bd(s_t[g].astype(BF16)), bd(v[g])], axis=0)) for g in gs]
    sa = [head_mm([p_s[g]], _split(zz[g][:C]))[0].astype(BF16) for g in gs]
    y = [zz[g][C:] + _dot(b_rb[g], bd(sa[g])) for g in gs]

    yc = [y[g] - seg_sum(y[g]) * (1.0 / RW_HEAD) for g in gs]
    var = [seg_sum(yc[g] * yc[g]) * (1.0 / RW_HEAD) for g in gs]
    bonus = [seg_sum(r[g] * kh[g] * cols(rk_ref, g)) * v_f[g] for g in gs]
    for g in gs:
        yn = yc[g] * lax.rsqrt(var[g] + RW_LNX_EPS) * cols(lg_ref, g) + cols(lb_ref, g)
        o_ref[:, g * W:(g + 1) * W] = ((yn + bonus[g]) * cols(gate_ref, g)).astype(BF16)

    upd_l = [jnp.concatenate([rb[g] * e_rem[g], kh[g] * e_rem[g]], axis=0).astype(BF16) for g in gs]
    upd_r = [jnp.concatenate([sa[g], v[g]], axis=0) for g in gs]
    full = [_dot_tn(upd_l[g], upd_r[g]) for g in gs]
    c_col = [seg_sum(eye * c_last[g]) for g in gs]
    for g in gs:
        gain = jnp.zeros((C, W), F32)
        for h in range(HEADS_PER_GROUP):
            gain = gain + jnp.where(lane_head == h, full[g][h * C:(h + 1) * C, :], 0.0)
        s_ref[g] = jnp.exp(c_col[g]) * s_t[g] + gain


def _rwkv_recurrence(r, k, v, w_raw, a, gate, k_k, k_a, r_k, lnx_g, lnx_b, *, batch, groups_per_step):
    t, d = r.shape
    chunks = t // (batch * CHUNK)
    gb = min(groups_per_step, d // GROUP)
    wb = gb * GROUP
    seq_spec = pl.BlockSpec((CHUNK, wb), lambda b, g, c: (b * chunks + c, g))
    par_spec = pl.BlockSpec((1, wb), lambda b, g, c: (0, g))
    par = lambda p: p.reshape(1, d)
    return pl.pallas_call(
        functools.partial(_rwkv_kernel, groups=gb),
        grid=(batch, d // wb, chunks),
        in_specs=[seq_spec] * 6 + [par_spec] * 5,
        out_specs=seq_spec,
        out_shape=jax.ShapeDtypeStruct((t, d), BF16),
        scratch_shapes=[pltpu.VMEM((gb, CHUNK, GROUP), F32)],
        compiler_params=_cparams("parallel", "parallel", "arbitrary"), name="rwkv_recurrence",
    )(r, k, v, w_raw, a, gate, par(k_k), par(k_a), par(r_k), par(lnx_g), par(lnx_b))


def _post_norm_kernel(h_ref, m_ref, gp_ref, *rest):
    h = h_ref[...] + _rms(m_ref[...].astype(F32)) * gp_ref[...]
    if len(rest) == 1:
        rest[0][...] = h
    else:
        gn_ref, h_out, p_out = rest
        h_out[...] = h
        p_out[...] = (_rms(h) * gn_ref[...]).astype(BF16)


def _post_norm(h, m, g_post, g_next, *, tm):
    t, d = h.shape
    row_spec = pl.BlockSpec((tm, d), lambda i: (i, 0))
    par_spec = pl.BlockSpec((1, d), lambda i: (0, 0))
    h_shape = jax.ShapeDtypeStruct((t, d), F32)
    if g_next is None:
        return pl.pallas_call(
            _post_norm_kernel, grid=(t // tm,),
            in_specs=[row_spec, row_spec, par_spec], out_specs=row_spec, out_shape=h_shape,
            compiler_params=_cparams("parallel"), name="post_norm_last",
        )(h, m, g_post.reshape(1, d)), None
    return pl.pallas_call(
        _post_norm_kernel, grid=(t // tm,),
        in_specs=[row_spec, row_spec, par_spec, par_spec],
        out_specs=[row_spec, row_spec],
        out_shape=[h_shape, jax.ShapeDtypeStruct((t, d), BF16)],
        compiler_params=_cparams("parallel"), name="post_norm",
    )(h, m, g_post.reshape(1, d), g_next.reshape(1, d))


def _ffn_in_kernel(x_ref, halo_ref, wu_ref, wg_ref, cw_ref, cb_ref, wd_ref, o_ref, wd_out, *,
                   blocks_per_seq, halo_rows, sub):
    i = pl.program_id(0)
    wd_out[...] = wd_ref[...].astype(BF16)
    wu = wu_ref[...].astype(BF16)
    wg = wg_ref[...].astype(BF16)
    zh = _dot(halo_ref[...], wg)
    zh = jnp.where(i % blocks_per_seq == 0, 0.0, zh)
    prev1, prev2 = zh[halo_rows - 1:halo_rows, :], zh[halo_rows - 2:halo_rows - 1, :]
    row = lax.broadcasted_iota(jnp.int32, (sub, wu.shape[1]), 0)
    n_sub = x_ref.shape[0] // sub

    def project(s):
        xs = x_ref[s * sub:(s + 1) * sub, :]
        return _dot(xs, wu), _dot(xs, wg)

    cur = project(0)
    for s in range(n_sub):
        nxt = project(s + 1) if s + 1 < n_sub else None
        u, z = cur
        z1 = jnp.where(row == 0, prev1, pltpu.roll(z, 1, 0))
        z2 = pltpu.roll(z, 2, 0)
        z2 = jnp.where(row == 1, prev1, z2)
        z2 = jnp.where(row == 0, prev2, z2)
        zc = z * cw_ref[2:3, :] + z1 * cw_ref[1:2, :] + z2 * cw_ref[0:1, :] + cb_ref[...]
        o_ref[s * sub:(s + 1) * sub, :] = (zc * _sigmoid(zc) * u).astype(BF16)
        prev1, prev2 = z[sub - 1:sub, :], z[sub - 2:sub - 1, :]
        cur = nxt


def _ffn_in(x, w_up, w_gate, w_down, layer, conv_w, conv_b, *, lp, tm, tn):
    t, d = x.shape
    f = w_up.shape[2]
    hr = 16
    ni, nj = t // tm, f // tn
    slab = f // (ni * nj)
    assert slab * ni * nj == f and slab % hr == 0
    w_spec = pl.BlockSpec((None, d, tn), lambda i, j: (layer, 0, j))
    return pl.pallas_call(
        functools.partial(_ffn_in_kernel, blocks_per_seq=lp // tm, halo_rows=hr,
                          sub=_pick(tm, (352, 96, 64))),
        grid=(ni, nj),
        in_specs=[pl.BlockSpec((tm, d), lambda i, j: (i, 0), pipeline_mode=pl.Buffered(1)),
                  pl.BlockSpec((hr, d), lambda i, j: (jnp.maximum(i * (tm // hr) - 1, 0), 0)),
                  w_spec, w_spec,
                  pl.BlockSpec((conv_w.shape[0], tn), lambda i, j: (0, j)),
                  pl.BlockSpec((1, tn), lambda i, j: (0, j)),
                  pl.BlockSpec((None, slab, d), lambda i, j: (layer, i * nj + j, 0))],
        out_specs=[pl.BlockSpec((tm, tn), lambda i, j: (i, j)),
                   pl.BlockSpec((slab, d), lambda i, j: (i * nj + j, 0))],
        out_shape=[jax.ShapeDtypeStruct((t, f), BF16), jax.ShapeDtypeStruct((f, d), BF16)],
        compiler_params=_cparams("parallel", "arbitrary"), name="ffn_in",
    )(x, x, w_up, w_gate, conv_w, conv_b.reshape(1, f), w_down)


GLA_LEVELS = 6


def _gla_select_matrices():
    t_i = np.arange(CHUNK)[:, None]
    u_i = np.arange(CHUNK)[None, :]
    mats = [u_i <= t_i]
    for l in range(GLA_LEVELS):
        half = 1 << l
        p = (t_i & ~(2 * half - 1)) + (half - 1)
        mats.append((u_i > p) & (u_i <= t_i))
        mats.append((u_i > t_i) & (u_i <= p))
    sel = np.concatenate(mats, axis=0).astype(np.float32)
    return jnp.asarray(np.concatenate([sel, sel], axis=1), dtype=BF16)


def _gla_kernel(sel_ref, q_ref, k_ref, v_ref, g_ref, r_ref, rb_ref, ng_ref, o_ref, s_ref, *, heads, hk, hv):
    c_id = pl.program_id(2)

    @pl.when(c_id == 0)
    def _():
        s_ref[...] = jnp.zeros_like(s_ref)

    C = CHUNK
    t_i = lax.broadcasted_iota(jnp.int32, (C, C), 0)
    u_i = lax.broadcasted_iota(jnp.int32, (C, C), 1)
    row = lax.broadcasted_iota(jnp.int32, (C, 1), 0)
    eye = t_i == u_i
    hs = range(heads)
    sel = sel_ref[...]

    q = [q_ref[:, h * hk:(h + 1) * hk] for h in hs]
    k = [k_ref[:, h * hk:(h + 1) * hk] for h in hs]
    v = [v_ref[:, h * hv:(h + 1) * hv].astype(BF16) for h in hs]
    g_s = [_split(g_ref[:, h * hk:(h + 1) * hk]) for h in hs]
    cums = [_dot(sel, jnp.concatenate(g_s[h], axis=0)) for h in hs]
    b = [cums[h][:C] for h in hs]
    b_last = [b[h][C - 1:C, :] for h in hs]

    a = [jnp.where(eye, jnp.sum(q[h] * k[h], axis=-1, keepdims=True), 0.0) for h in hs]
    for l in range(GLA_LEVELS):
        upper = (row & (1 << l)) != 0
        same_block = (t_i >> (l + 1)) == (u_i >> (l + 1))
        ql = [jnp.where(upper, q[h] * jnp.exp(cums[h][(1 + 2 * l) * C:(2 + 2 * l) * C]), 0.0).astype(BF16)
              for h in hs]
        kl = [jnp.where(upper, 0.0, k[h] * jnp.exp(cums[h][(2 + 2 * l) * C:(3 + 2 * l) * C])).astype(BF16)
              for h in hs]
        a = [a[h] + jnp.where(same_block, _dot_nt(ql[h], kl[h]), 0.0) for h in hs]

    s_t = [s_ref[h] for h in hs]
    qe = [(q[h] * jnp.exp(b[h])).astype(BF16) for h in hs]
    kr = [(k[h] * jnp.exp(b_last[h] - b[h])).astype(BF16) for h in hs]
    o = [_dot_nt(qe[h], s_t[h].astype(BF16)) + _dot(a[h].astype(BF16), v[h]) for h in hs]
    for h in hs:
        s_ref[h] = s_t[h] * jnp.exp(b_last[h]) + _dot_tn(v[h], kr[h])
    for h in hs:
        zg = r_ref[:, h * hv:(h + 1) * hv] + rb_ref[:, h * hv:(h + 1) * hv]
        o_ref[:, h * hv:(h + 1) * hv] = (_rms(o[h]) * ng_ref[...] * (zg * _sigmoid(zg))).astype(BF16)


def _gla_recurrence(q, k, v, g, r, r_b, norm_g, *, batch, hk, hv, heads_per_step):
    t = q.shape[0]
    heads = q.shape[1] // hk
    hb = heads_per_step
    chunks = t // (batch * CHUNK)
    sel = _gla_select_matrices()
    k_spec = pl.BlockSpec((CHUNK, hb * hk), lambda b, h, c: (b * chunks + c, h))
    v_spec = pl.BlockSpec((CHUNK, hb * hv), lambda b, h, c: (b * chunks + c, h))
    return pl.pallas_call(
        functools.partial(_gla_kernel, heads=hb, hk=hk, hv=hv),
        grid=(batch, heads // hb, chunks),
        in_specs=[pl.BlockSpec(sel.shape, lambda b, h, c: (0, 0)), k_spec, k_spec, v_spec, k_spec, v_spec,
                  pl.BlockSpec((1, hb * hv), lambda b, h, c: (0, h)),
                  pl.BlockSpec((1, hv), lambda b, h, c: (0, 0))],
        out_specs=v_spec,
        out_shape=jax.ShapeDtypeStruct((t, heads * hv), BF16),
        scratch_shapes=[pltpu.VMEM((hb, hv, hk), F32)],
        compiler_params=_cparams("parallel", "parallel", "arbitrary"), name="gla_recurrence",
    )(sel, q, k, v, g, r, r_b.reshape(1, heads * hv), norm_g.reshape(1, hv))


def _tiles(lp):
    return dict(
        ew=_pick(lp, (192, 128, 64)),
        lora=_pick(lp, (352, 192, 128, 64)),
        mm=_pick(lp, (1056, 704, 352, 192, 128, 64)),
        down=_pick(lp, (704, 1056, 352, 192, 128, 64)),
        ffn=_pick(lp, (2112, 1056, 704, 352, 192, 128, 64)),
    )


def _col_tile(n, cands=(512, 256, 128)):
    return _pick(n, cands)


def _rwkv_layer(h, g_pre, p, *, batch, lp, tiles):
    d = h.shape[1]
    xr, xw, xk, xv, xa, xg = _shift_mix(h, g_pre, p["mu"], lp=lp, tm=tiles["ew"])
    tn = _col_tile(d)
    proj = lambda x, w, name: _matmul(x, w, tm=tiles["mm"], tn=tn, name=name)
    r = proj(xr, p["wr"], "rw_r")
    k = proj(xk, p["wk"], "rw_k")
    v = proj(xv, p["wv"], "rw_v")
    lora = functools.partial(_lora, tm=tiles["lora"])
    w_raw = lora(xw, p["w1"], p["w2"], p["w0"], act1=jnp.tanh, act2=None, name="rw_decay")
    a = lora(xa, p["a1"], p["a2"], p["a0"], act1=None, act2=_sigmoid, name="rw_rate")
    gate = lora(xg, p["g1"], p["g2"], jnp.zeros((d,), F32), act1=_sigmoid, act2=None, name="rw_gate")
    yo = _rwkv_recurrence(r, k, v, w_raw, a, gate, p["k_k"], p["k_a"], p["r_k"].reshape(-1),
                          p["lnx_g"], p["lnx_b"], batch=batch, groups_per_step=16)
    return _matmul(yo, p["wo"], tm=tiles["mm"], tn=tn, out_dtype=BF16, name="rw_o")


def _log_gate(x):
    return (jnp.minimum(x, 0.0) - jnp.log(1.0 + jnp.exp(-jnp.abs(x)))) * (1.0 / GLA_GATE_TAU)


def _gla_layer(pre, p, *, batch, tiles):
    d = pre.shape[1]
    dk = p["a2"].shape[1]
    dv = p["wo"].shape[0]
    heads = max(4, d // 512)
    hk, hv = dk // heads, dv // heads
    mm = functools.partial(_matmul, pre, p["w_in"], tm=tiles["mm"], tn=_col_tile(dk))
    q_scale = hk ** -0.5
    q = mm(cols=(0, dk), epilogue=lambda acc: acc * q_scale, name="gla_q")
    k = mm(cols=(dk, 2 * dk), name="gla_k")
    v = mm(cols=(2 * dk, 2 * dk + dv), name="gla_v")
    r = mm(cols=(2 * dk + dv, 2 * dk + 2 * dv), name="gla_r")
    g = _lora(pre, p["a1"], p["a2"], p["a_b"], act1=None, act2=_log_gate, tm=tiles["lora"], name="gla_gate")
    og = _gla_recurrence(q, k, v, g, r, p["r_b"], p["norm_g"], batch=batch, hk=hk, hv=hv, heads_per_step=4)
    return _matmul(og, p["wo"], tm=tiles["mm"], tn=_col_tile(d), out_dtype=BF16, name="gla_o")


def _ffn(pre, w_up, w_gate, w_down, layer, conv_w, conv_b, *, lp, tiles):
    f = w_up.shape[2]
    act, w_down_bf = _ffn_in(pre, w_up, w_gate, w_down, layer, conv_w, conv_b,
                             lp=lp, tm=tiles["ffn"], tn=_col_tile(f, (256, 128)))
    return _matmul(act, w_down_bf, tm=tiles["down"], tn=_col_tile(w_down.shape[2], (256, 128)),
                   out_dtype=BF16, name="ffn_down")


def kernel(x, meta, norm_g, rw_mu, rw_w0, rw_w1, rw_w2, rw_a0, rw_a1, rw_a2, rw_g1, rw_g2, rw_k_k, rw_k_a, rw_r_k, rw_wr, rw_wk, rw_wv, rw_wo, rw_lnx_g, rw_lnx_b, gla_w_in, gla_a1, gla_a2, gla_a_b, gla_r_b, gla_norm_g, gla_wo, ffn_up, ffn_gate, ffn_conv, ffn_conv_b, ffn_down):
    batch, seq, d = x.shape
    n_meta = meta.shape[0]
    depth = norm_g.shape[0]
    length = n_meta + seq
    lp = -(-length // CHUNK) * CHUNK
    m = jnp.broadcast_to(meta.astype(x.dtype)[None], (batch, n_meta, d))
    h = jnp.concatenate([m, x, jnp.zeros((batch, lp - length, d), x.dtype)], axis=1).reshape(batch * lp, d)
    tiles = _tiles(lp)

    pre = None
    for i in range(depth):
        j = i // 2
        if i % 2 == 0:
            p = dict(mu=rw_mu[j], w0=rw_w0[j], w1=rw_w1[j], w2=rw_w2[j], a0=rw_a0[j], a1=rw_a1[j],
                     a2=rw_a2[j], g1=rw_g1[j], g2=rw_g2[j], k_k=rw_k_k[j], k_a=rw_k_a[j], r_k=rw_r_k[j],
                     wr=rw_wr[j], wk=rw_wk[j], wv=rw_wv[j], wo=rw_wo[j], lnx_g=rw_lnx_g[j],
                     lnx_b=rw_lnx_b[j])
            mix = _rwkv_layer(h, norm_g[i, 0], p, batch=batch, lp=lp, tiles=tiles)
        else:
            p = dict(w_in=gla_w_in[j], a1=gla_a1[j], a2=gla_a2[j], a_b=gla_a_b[j], r_b=gla_r_b[j],
                     norm_g=gla_norm_g[j], wo=gla_wo[j])
            mix = _gla_layer(pre, p, batch=batch, tiles=tiles)
        h, pre = _post_norm(h, mix, norm_g[i, 1], norm_g[i, 2], tm=tiles["ew"])
        f = _ffn(pre, ffn_up, ffn_gate, ffn_down, i, ffn_conv[i], ffn_conv_b[i], lp=lp, tiles=tiles)
        g_next = norm_g[i + 1, 0] if i + 1 < depth else None
        h, pre = _post_norm(h, f, norm_g[i, 3], g_next, tm=tiles["ew"])
    return h.reshape(batch, lp, d)[:, n_meta:length]
```

```python
import functools

import numpy as np
import jax
import jax.numpy as jnp
from jax import lax
from jax.experimental import pallas as pl
from jax.experimental.pallas import tpu as pltpu

F32 = jnp.float32
BF16 = jnp.bfloat16

NORM_EPS = 1e-6
RW_LNX_EPS = 64e-5
RW_HEAD = 64
DECAY_SCALE = float(np.exp(-0.5))
GLA_GATE_TAU = 16.0
CHUNK = 64
LANES = 128
GROUP = LANES
HEADS_PER_GROUP = GROUP // RW_HEAD
VMEM_LIMIT = 56 * 1024 * 1024


def _cparams(*sem):
    return pltpu.CompilerParams(dimension_semantics=sem, vmem_limit_bytes=VMEM_LIMIT)


def _pick(n, cands):
    for c in cands:
        if n % c == 0:
            return c
    raise ValueError(f"no tile for {n} in {cands}")


def _rms(x):
    return x * lax.rsqrt(jnp.mean(x * x, axis=-1, keepdims=True) + NORM_EPS)


def _sigmoid(z):
    return 1.0 / (1.0 + jnp.exp(-z))


def _dot(a, b):
    return jnp.dot(a, b, preferred_element_type=F32)


def _dot_nt(a, b):
    return lax.dot_general(a, b, (((1,), (1,)), ((), ())), preferred_element_type=F32)


def _dot_tn(a, b):
    return lax.dot_general(a, b, (((0,), (0,)), ((), ())), preferred_element_type=F32)


def _split(x):
    hi = x.astype(BF16)
    lo = (x - hi.astype(F32)).astype(BF16)
    return hi, lo


def _mm_kernel(x_ref, w_ref, *rest, epilogue, n_extra):
    extra = rest[:n_extra]
    o_ref = rest[n_extra]
    acc = _dot(x_ref[...], w_ref[...].astype(BF16))
    if epilogue is not None:
        acc = epilogue(acc, *[e[...] for e in extra])
    o_ref[...] = acc.astype(o_ref.dtype)


def _matmul(x, w, *, tm, tn, cols=None, out_dtype=F32, epilogue=None, extras=(), name):
    t, k = x.shape
    c0, c1 = (0, w.shape[1]) if cols is None else cols
    n = c1 - c0
    off = c0 // tn
    assert c0 % tn == 0 and n % tn == 0
    grid = (t // tm, n // tn)
    in_specs = [pl.BlockSpec((tm, k), lambda i, j: (i, 0)),
                pl.BlockSpec((k, tn), lambda i, j: (0, j + off))]
    in_specs += [pl.BlockSpec((1, tn), lambda i, j: (0, j)) for _ in extras]
    return pl.pallas_call(
        functools.partial(_mm_kernel, epilogue=epilogue, n_extra=len(extras)),
        grid=grid, in_specs=in_specs,
        out_specs=pl.BlockSpec((tm, tn), lambda i, j: (i, j)),
        out_shape=jax.ShapeDtypeStruct((t, n), out_dtype),
        compiler_params=_cparams("parallel", "arbitrary"), name=name,
    )(x, w, *extras)


def _lora_kernel(x_ref, w1_ref, w2_ref, b_ref, o_ref, *, act1, act2):
    t = _dot(x_ref[...], w1_ref[...])
    if act1 is not None:
        t = act1(t)
    y = _dot(t.astype(BF16), w2_ref[...]) + b_ref[...]
    if act2 is not None:
        y = act2(y)
    o_ref[...] = y


def _pad_rank(w1, w2):
    r = w1.shape[1]
    rp = -(-r // LANES) * LANES
    return (jnp.pad(w1, ((0, 0), (0, rp - r))).astype(BF16),
            jnp.pad(w2, ((0, rp - r), (0, 0))).astype(BF16))


def _lora(x, w1, w2, bias, *, act1, act2, tm, name):
    t, k = x.shape
    w1, w2 = _pad_rank(w1, w2)
    rp = w1.shape[1]
    n = w2.shape[1]
    return pl.pallas_call(
        functools.partial(_lora_kernel, act1=act1, act2=act2),
        grid=(t // tm,),
        in_specs=[pl.BlockSpec((tm, k), lambda i: (i, 0)),
                  pl.BlockSpec((k, rp), lambda i: (0, 0)),
                  pl.BlockSpec((rp, n), lambda i: (0, 0)),
                  pl.BlockSpec((1, n), lambda i: (0, 0))],
        out_specs=pl.BlockSpec((tm, n), lambda i: (i, 0)),
        out_shape=jax.ShapeDtypeStruct((t, n), F32),
        compiler_params=_cparams("parallel"), name=name,
    )(x, w1, w2, bias.reshape(1, n).astype(F32))


def _shift_mix_kernel(h_ref, halo_ref, g_ref, mu_ref, *out_refs, blocks_per_seq):
    i = pl.program_id(0)
    g = g_ref[...]
    pre = _rms(h_ref[...]) * g
    halo = (_rms(halo_ref[...]) * g)[7:8, :]
    halo = jnp.where(i % blocks_per_seq == 0, 0.0, halo)
    row = lax.broadcasted_iota(jnp.int32, pre.shape, 0)
    prev = jnp.where(row == 0, halo, pltpu.roll(pre, 1, 0))
    xx = prev - pre
    for m, o_ref in enumerate(out_refs):
        o_ref[...] = (pre + xx * mu_ref[m:m + 1, :]).astype(BF16)


def _shift_mix(h, g, mu, *, lp, tm):
    t, d = h.shape
    nmix = mu.shape[0]
    row_spec = pl.BlockSpec((tm, d), lambda i: (i, 0))
    return pl.pallas_call(
        functools.partial(_shift_mix_kernel, blocks_per_seq=lp // tm),
        grid=(t // tm,),
        in_specs=[row_spec,
                  pl.BlockSpec((8, d), lambda i: (jnp.maximum(i * (tm // 8) - 1, 0), 0)),
                  pl.BlockSpec((1, d), lambda i: (0, 0)),
                  pl.BlockSpec((nmix, d), lambda i: (0, 0))],
        out_specs=[row_spec] * nmix,
        out_shape=[jax.ShapeDtypeStruct((t, d), BF16)] * nmix,
        compiler_params=_cparams("parallel"), name="rw_shift_mix",
    )(h, h, g.reshape(1, d), mu)


def _rwkv_kernel(r_ref, k_ref, v_ref, w_ref, a_ref, gate_ref, kk_ref, ka_ref, rk_ref, lg_ref, lb_ref,
                 o_ref, s_ref, *, groups):
    c_id = pl.program_id(2)

    @pl.when(c_id == 0)
    def _():
        s_ref[...] = jnp.zeros_like(s_ref)

    C, W = CHUNK, GROUP
    row = lax.broadcasted_iota(jnp.int32, (C, W), 0)
    lane = lax.broadcasted_iota(jnp.int32, (C, W), 1)
    lane_in = lane & (RW_HEAD - 1)
    lane_head = lane >> 6
    tril_incl = lane_in <= row
    tril_strict = lane_in < row
    eye = jnp.where(lane_in == row, 1.0, 0.0)
    diag_block = (lane_in >> 4) == (row >> 4)
    bd_mask = (lax.broadcasted_iota(jnp.int32, (HEADS_PER_GROUP * C, W), 0) >> 6) == \
              (lax.broadcasted_iota(jnp.int32, (HEADS_PER_GROUP * C, W), 1) >> 6)
    ones_bd = jnp.where(bd_mask, 1.0, 0.0).astype(BF16)
    ones_bd2 = jnp.concatenate([ones_bd, ones_bd], axis=0)
    ltri = jnp.where(lax.broadcasted_iota(jnp.int32, (C, C), 1) <=
                     lax.broadcasted_iota(jnp.int32, (C, C), 0), 1.0, 0.0).astype(BF16)
    gs = range(groups)
    cols = lambda ref, g: ref[:, g * W:(g + 1) * W]

    def bd(x):
        return jnp.concatenate([x] * HEADS_PER_GROUP, axis=0) * ones_bd

    def seg_sum(x):
        return _dot(jnp.concatenate(_split(x), axis=1), ones_bd2)

    def head_mm(lhs, ys):
        top = jnp.concatenate([bd(ys[0]), bd(ys[1])], axis=1)
        res = _dot(jnp.concatenate([jnp.concatenate(x, axis=1) for x in lhs], axis=0),
                   jnp.concatenate([top, top], axis=0))
        return [res[i * C:(i + 1) * C, :W] + res[i * C:(i + 1) * C, W:] for i in range(len(lhs))]

    k = [cols(k_ref, g) for g in gs]
    v = [cols(v_ref, g).astype(BF16) for g in gs]
    a = [cols(a_ref, g) for g in gs]
    ld = [-DECAY_SCALE * _sigmoid(cols(w_ref, g)) for g in gs]
    kk = [k[g] * cols(kk_ref, g) for g in gs]
    kk = [kk[g] * lax.rsqrt(jnp.maximum(seg_sum(kk[g] * kk[g]), 1e-24)) for g in gs]
    kh = [k[g] * (1.0 + (a[g] - 1.0) * cols(ka_ref, g)) for g in gs]
    rb = [kk[g] * a[g] for g in gs]

    cum2 = [_dot(ltri, jnp.concatenate(_split(ld[g]), axis=1)) for g in gs]
    cum = [cum2[g][:, :W] + cum2[g][:, W:] for g in gs]
    c_last = [cum[g][C - 1:C, :] for g in gs]
    e_nc = [jnp.exp(-cum[g]) for g in gs]
    lhs = [jnp.concatenate([-kk[g] * jnp.exp(cum[g] - ld[g]), cols(r_ref, g) * jnp.exp(cum[g])],
                           axis=0).astype(BF16) for g in gs]
    b_hat = [(rb[g] * e_nc[g]).astype(BF16) for g in gs]
    k_hat = [(kh[g] * e_nc[g]).astype(BF16) for g in gs]
    gram = [_dot_nt(lhs[g], jnp.concatenate([bd(b_hat[g]), bd(k_hat[g])], axis=0)) for g in gs]
    n_ab = [jnp.where(tril_strict, gram[g][:C, :W], 0.0) for g in gs]
    b_rb = [jnp.where(tril_incl, gram[g][C:, :W], 0.0).astype(BF16) for g in gs]
    ak_rk = [jnp.concatenate([jnp.where(tril_strict, gram[g][:C, W:], 0.0),
                              jnp.where(tril_incl, gram[g][C:, W:], 0.0)], axis=0).astype(BF16) for g in gs]

    nd = [jnp.where(diag_block, n_ab[g], 0.0) for g in gs]
    no_s = [_split(n_ab[g] - nd[g]) for g in gs]
    pd = [eye + nd[g] for g in gs]
    x_s = [_split(nd[g]) for g in gs]
    x_s = [_split(head_mm([x_s[g]], x_s[g])[0]) for g in gs]
    for _ in range(2):
        prod = [head_mm([x_s[g], _split(pd[g])], x_s[g]) for g in gs]
        pd = [pd[g] + prod[g][1] for g in gs]
        x_s = [_split(prod[g][0]) for g in gs]
    pd = [pd[g] + head_mm([_split(pd[g])], x_s[g])[0] for g in gs]
    pd_s = [_split(pd[g]) for g in gs]
    m = [head_mm([pd_s[g]], no_s[g])[0] for g in gs]
    m_s = [_split(m[g]) for g in gs]
    m2_s = [_split(head_mm([m_s[g]], m_s[g])[0]) for g in gs]
    q = [eye + m[g] for g in gs]
    q_s = [_split(q[g] + head_mm([_split(q[g])], m2_s[g])[0]) for g in gs]
    p_s = [_split(head_mm([q_s[g]], pd_s[g])[0]) for g in gs]

    s_t = [s_ref[g] for g in gs]
    zz = [_dot(jnp.concatenate([lhs[g], ak_rk[g]], axis=1),
               jnp.concatenate([bd(s_t[g].astype(BF16)), bd(v[g])], axis=0)) for g in gs]
    sa = [head_mm([p_s[g]], _split(zz[g][:C]))[0].astype(BF16) for g in gs]
    y = [zz[g][C:] + _dot(b_rb[g], bd(sa[g])) for g in gs]

    yc = [y[g] - seg_sum(y[g]) * (1.0 / RW_HEAD) for g in gs]
    var = [seg_sum(yc[g] * yc[g]) * (1.0 / RW_HEAD) for g in gs]
    kh = [cols(k_ref, g) * (1.0 + (cols(a_ref, g) - 1.0) * cols(ka_ref, g)) for g in gs]
    bonus = [seg_sum(cols(r_ref, g) * kh[g] * cols(rk_ref, g)) * cols(v_ref, g) for g in gs]
    for g in gs:
        yn = yc[g] * lax.rsqrt(var[g] + RW_LNX_EPS) * cols(lg_ref, g) + cols(lb_ref, g)
        o_ref[:, g * W:(g + 1) * W] = ((yn + bonus[g]) * cols(gate_ref, g)).astype(BF16)

    full = [_dot_tn(jnp.concatenate([b_hat[g], k_hat[g]], axis=0), jnp.concatenate([sa[g], v[g]], axis=0))
            for g in gs]
    c_col = [seg_sum(eye * c_last[g]) for g in gs]
    for g in gs:
        gain = s_t[g]
        for h in range(HEADS_PER_GROUP):
            gain = gain + jnp.where(lane_head == h, full[g][h * C:(h + 1) * C, :], 0.0)
        s_ref[g] = jnp.exp(c_col[g]) * gain


def _rwkv_recurrence(r, k, v, w_raw, a, gate, k_k, k_a, r_k, lnx_g, lnx_b, *, batch, groups_per_step):
    t, d = r.shape
    chunks = t // (batch * CHUNK)
    gb = min(groups_per_step, d // GROUP)
    wb = gb * GROUP
    seq_spec = pl.BlockSpec((CHUNK, wb), lambda b, g, c: (b * chunks + c, g))
    par_spec = pl.BlockSpec((1, wb), lambda b, g, c: (0, g))
    par = lambda p: p.reshape(1, d)
    return pl.pallas_call(
        functools.partial(_rwkv_kernel, groups=gb),
        grid=(batch, d // wb, chunks),
        in_specs=[seq_spec] * 6 + [par_spec] * 5,
        out_specs=seq_spec,
        out_shape=jax.ShapeDtypeStruct((t, d), BF16),
        scratch_shapes=[pltpu.VMEM((gb, CHUNK, GROUP), F32)],
        compiler_params=_cparams("parallel", "parallel", "arbitrary"), name="rwkv_recurrence",
    )(r, k, v, w_raw, a, gate, par(k_k), par(k_a), par(r_k), par(lnx_g), par(lnx_b))


def _post_norm_kernel(h_ref, m_ref, gp_ref, *rest):
    h = h_ref[...] + _rms(m_ref[...].astype(F32)) * gp_ref[...]
    if len(rest) == 1:
        rest[0][...] = h
    else:
        gn_ref, h_out, p_out = rest
        h_out[...] = h
        p_out[...] = (_rms(h) * gn_ref[...]).astype(BF16)


def _post_norm_final(h, m, g_post, *, batch, lp, start, seq, tm):
    d = h.shape[1]
    align = 16
    assert lp % align == 0 and start % align == 0 and tm % align == 0
    in_spec = pl.BlockSpec((pl.Element(tm), pl.Element(d)),
                           lambda b, i: (pl.multiple_of(b * lp + start + i * tm, align), 0))
    return pl.pallas_call(
        _post_norm_kernel, grid=(batch, seq // tm),
        in_specs=[in_spec, in_spec, pl.BlockSpec((1, d), lambda b, i: (0, 0))],
        out_specs=pl.BlockSpec((None, tm, d), lambda b, i: (b, i, 0)),
        out_shape=jax.ShapeDtypeStruct((batch, seq, d), F32),
        compiler_params=_cparams("parallel", "parallel"), name="post_norm_final",
    )(h, m, g_post.reshape(1, d))


def _post_norm(h, m, g_post, g_next, *, tm):
    t, d = h.shape
    row_spec = pl.BlockSpec((tm, d), lambda i: (i, 0))
    par_spec = pl.BlockSpec((1, d), lambda i: (0, 0))
    h_shape = jax.ShapeDtypeStruct((t, d), F32)
    return pl.pallas_call(
        _post_norm_kernel, grid=(t // tm,),
        in_specs=[row_spec, row_spec, par_spec, par_spec],
        out_specs=[row_spec, row_spec],
        out_shape=[h_shape, jax.ShapeDtypeStruct((t, d), BF16)],
        compiler_params=_cparams("parallel"), name="post_norm",
    )(h, m, g_post.reshape(1, d), g_next.reshape(1, d))


def _ffn_in_kernel(x_ref, halo_ref, wu_ref, wg_ref, cw_ref, cb_ref, wd_ref, o_ref, wd_out, *,
                   blocks_per_seq, halo_rows, sub):
    i = pl.program_id(0)
    wd_out[...] = wd_ref[...].astype(BF16)
    wu = wu_ref[...].astype(BF16)
    wg = wg_ref[...].astype(BF16)
    zh = _dot(halo_ref[...], wg)
    zh = jnp.where(i % blocks_per_seq == 0, 0.0, zh)
    prev1, prev2 = zh[halo_rows - 1:halo_rows, :], zh[halo_rows - 2:halo_rows - 1, :]
    row = lax.broadcasted_iota(jnp.int32, (sub, wu.shape[1]), 0)
    n_sub = x_ref.shape[0] // sub

    def project(s):
        xs = x_ref[s * sub:(s + 1) * sub, :]
        return _dot(xs, wu), _dot(xs, wg)

    cur = project(0)
    for s in range(n_sub):
        nxt = project(s + 1) if s + 1 < n_sub else None
        u, z = cur
        z1 = jnp.where(row == 0, prev1, pltpu.roll(z, 1, 0))
        z2 = pltpu.roll(z, 2, 0)
        z2 = jnp.where(row == 1, prev1, z2)
        z2 = jnp.where(row == 0, prev2, z2)
        zc = z * cw_ref[2:3, :] + z1 * cw_ref[1:2, :] + z2 * cw_ref[0:1, :] + cb_ref[...]
        o_ref[s * sub:(s + 1) * sub, :] = (zc * _sigmoid(zc) * u).astype(BF16)
        prev1, prev2 = z[sub - 1:sub, :], z[sub - 2:sub - 1, :]
        cur = nxt


def _ffn_in(x, w_up, w_gate, w_down, layer, conv_w, conv_b, *, lp, tm, tn):
    t, d = x.shape
    f = w_up.shape[2]
    hr = 16
    ni, nj = t // tm, f // tn
    slab = f // (ni * nj)
    assert slab * ni * nj == f and slab % hr == 0
    w_spec = pl.BlockSpec((None, d, tn), lambda i, j: (layer, 0, j))
    return pl.pallas_call(
        functools.partial(_ffn_in_kernel, blocks_per_seq=lp // tm, halo_rows=hr,
                          sub=_pick(tm, (352, 96, 64))),
        grid=(ni, nj),
        in_specs=[pl.BlockSpec((tm, d), lambda i, j: (i, 0), pipeline_mode=pl.Buffered(1)),
                  pl.BlockSpec((hr, d), lambda i, j: (jnp.maximum(i * (tm // hr) - 1, 0), 0)),
                  w_spec, w_spec,
                  pl.BlockSpec((conv_w.shape[0], tn), lambda i, j: (0, j)),
                  pl.BlockSpec((1, tn), lambda i, j: (0, j)),
                  pl.BlockSpec((None, slab, d), lambda i, j: (layer, i * nj + j, 0))],
        out_specs=[pl.BlockSpec((tm, tn), lambda i, j: (i, j)),
                   pl.BlockSpec((slab, d), lambda i, j: (i * nj + j, 0))],
        out_shape=[jax.ShapeDtypeStruct((t, f), BF16), jax.ShapeDtypeStruct((f, d), BF16)],
        compiler_params=_cparams("parallel", "arbitrary"), name="ffn_in",
    )(x, x, w_up, w_gate, conv_w, conv_b.reshape(1, f), w_down)


GLA_LEVELS = 6


def _gla_select_matrices():
    t_i = np.arange(CHUNK)[:, None]
    u_i = np.arange(CHUNK)[None, :]
    mats = [u_i <= t_i]
    for l in range(GLA_LEVELS):
        half = 1 << l
        p = (t_i & ~(2 * half - 1)) + (half - 1)
        mats.append((u_i > p) & (u_i <= t_i))
        mats.append((u_i > t_i) & (u_i <= p))
    sel = np.concatenate(mats, axis=0).astype(np.float32)
    return jnp.asarray(np.concatenate([sel, sel], axis=1), dtype=BF16)


def _gla_kernel(sel_ref, q_ref, k_ref, v_ref, g_ref, r_ref, rb_ref, ng_ref, o_ref, s_ref, *, heads, hk, hv):
    c_id = pl.program_id(2)

    @pl.when(c_id == 0)
    def _():
        s_ref[...] = jnp.zeros_like(s_ref)

    C = CHUNK
    t_i = lax.broadcasted_iota(jnp.int32, (C, C), 0)
    u_i = lax.broadcasted_iota(jnp.int32, (C, C), 1)
    row = lax.broadcasted_iota(jnp.int32, (C, 1), 0)
    eye = t_i == u_i
    hs = range(heads)
    sel = sel_ref[...]

    q = [q_ref[:, h * hk:(h + 1) * hk] for h in hs]
    k = [k_ref[:, h * hk:(h + 1) * hk] for h in hs]
    v = [v_ref[:, h * hv:(h + 1) * hv].astype(BF16) for h in hs]
    g_s = [_split(g_ref[:, h * hk:(h + 1) * hk]) for h in hs]
    cums = [_dot(sel, jnp.concatenate(g_s[h], axis=0)) for h in hs]
    b = [cums[h][:C] for h in hs]
    b_last = [b[h][C - 1:C, :] for h in hs]

    a = [jnp.where(eye, jnp.sum(q[h] * k[h], axis=-1, keepdims=True), 0.0) for h in hs]
    for l in range(GLA_LEVELS):
        upper = (row & (1 << l)) != 0
        same_block = (t_i >> (l + 1)) == (u_i >> (l + 1))
        ql = [jnp.where(upper, q[h] * jnp.exp(cums[h][(1 + 2 * l) * C:(2 + 2 * l) * C]), 0.0).astype(BF16)
              for h in hs]
        kl = [jnp.where(upper, 0.0, k[h] * jnp.exp(cums[h][(2 + 2 * l) * C:(3 + 2 * l) * C])).astype(BF16)
              for h in hs]
        a = [a[h] + jnp.where(same_block, _dot_nt(ql[h], kl[h]), 0.0) for h in hs]

    s_t = [s_ref[h] for h in hs]
    qe = [(q[h] * jnp.exp(b[h])).astype(BF16) for h in hs]
    kr = [(k[h] * jnp.exp(b_last[h] - b[h])).astype(BF16) for h in hs]
    o = [_dot_nt(qe[h], s_t[h].astype(BF16)) + _dot(a[h].astype(BF16), v[h]) for h in hs]
    for h in hs:
        s_ref[h] = s_t[h] * jnp.exp(b_last[h]) + _dot_tn(v[h], kr[h])
    for h in hs:
        zg = r_ref[:, h * hv:(h + 1) * hv] + rb_ref[:, h * hv:(h + 1) * hv]
        o_ref[:, h * hv:(h + 1) * hv] = (_rms(o[h]) * ng_ref[...] * (zg * _sigmoid(zg))).astype(BF16)


def _gla_recurrence(q, k, v, g, r, r_b, norm_g, *, batch, hk, hv, heads_per_step):
    t = q.shape[0]
    heads = q.shape[1] // hk
    hb = heads_per_step
    chunks = t // (batch * CHUNK)
    sel = _gla_select_matrices()
    k_spec = pl.BlockSpec((CHUNK, hb * hk), lambda b, h, c: (b * chunks + c, h))
    v_spec = pl.BlockSpec((CHUNK, hb * hv), lambda b, h, c: (b * chunks + c, h))
    return pl.pallas_call(
        functools.partial(_gla_kernel, heads=hb, hk=hk, hv=hv),
        grid=(batch, heads // hb, chunks),
        in_specs=[pl.BlockSpec(sel.shape, lambda b, h, c: (0, 0)), k_spec, k_spec, v_spec, k_spec, v_spec,
                  pl.BlockSpec((1, hb * hv), lambda b, h, c: (0, h)),
                  pl.BlockSpec((1, hv), lambda b, h, c: (0, 0))],
        out_specs=v_spec,
        out_shape=jax.ShapeDtypeStruct((t, heads * hv), BF16),
        scratch_shapes=[pltpu.VMEM((hb, hv, hk), F32)],
        compiler_params=_cparams("parallel", "parallel", "arbitrary"), name="gla_recurrence",
    )(sel, q, k, v, g, r, r_b.reshape(1, heads * hv), norm_g.reshape(1, hv))


def _tiles(lp):
    return dict(
        ew=_pick(lp, (192, 128, 64)),
        lora=_pick(lp, (352, 192, 128, 64)),
        mm=_pick(lp, (1056, 704, 352, 192, 128, 64)),
        down=_pick(lp, (704, 1056, 352, 192, 128, 64)),
        ffn=_pick(lp, (2112, 1056, 704, 352, 192, 128, 64)),
    )


def _col_tile(n, cands=(512, 256, 128)):
    return _pick(n, cands)


def _rwkv_layer(h, g_pre, p, *, batch, lp, tiles):
    d = h.shape[1]
    xr, xw, xk, xv, xa, xg = _shift_mix(h, g_pre, p["mu"], lp=lp, tm=tiles["ew"])
    tn = _col_tile(d)
    proj = lambda x, w, name: _matmul(x, w, tm=tiles["mm"], tn=tn, name=name)
    r = proj(xr, p["wr"], "rw_r")
    k = proj(xk, p["wk"], "rw_k")
    v = proj(xv, p["wv"], "rw_v")
    lora = functools.partial(_lora, tm=tiles["lora"])
    w_raw = lora(xw, p["w1"], p["w2"], p["w0"], act1=jnp.tanh, act2=None, name="rw_decay")
    a = lora(xa, p["a1"], p["a2"], p["a0"], act1=None, act2=_sigmoid, name="rw_rate")
    gate = lora(xg, p["g1"], p["g2"], jnp.zeros((d,), F32), act1=_sigmoid, act2=None, name="rw_gate")
    yo = _rwkv_recurrence(r, k, v, w_raw, a, gate, p["k_k"], p["k_a"], p["r_k"].reshape(-1),
                          p["lnx_g"], p["lnx_b"], batch=batch, groups_per_step=16)
    return _matmul(yo, p["wo"], tm=tiles["mm"], tn=tn, out_dtype=BF16, name="rw_o")


def _log_gate(x):
    return (jnp.minimum(x, 0.0) - jnp.log(1.0 + jnp.exp(-jnp.abs(x)))) * (1.0 / GLA_GATE_TAU)


def _gla_layer(pre, p, *, batch, tiles):
    d = pre.shape[1]
    dk = p["a2"].shape[1]
    dv = p["wo"].shape[0]
    heads = max(4, d // 512)
    hk, hv = dk // heads, dv // heads
    mm = functools.partial(_matmul, pre, p["w_in"], tm=tiles["mm"], tn=_col_tile(dk))
    q_scale = hk ** -0.5
    q = mm(cols=(0, dk), epilogue=lambda acc: acc * q_scale, name="gla_q")
    k = mm(cols=(dk, 2 * dk), name="gla_k")
    v = mm(cols=(2 * dk, 2 * dk + dv), name="gla_v")
    r = mm(cols=(2 * dk + dv, 2 * dk + 2 * dv), name="gla_r")
    g = _lora(pre, p["a1"], p["a2"], p["a_b"], act1=None, act2=_log_gate, tm=tiles["lora"], name="gla_gate")
    og = _gla_recurrence(q, k, v, g, r, p["r_b"], p["norm_g"], batch=batch, hk=hk, hv=hv, heads_per_step=4)
    return _matmul(og, p["wo"], tm=tiles["mm"], tn=_col_tile(d), out_dtype=BF16, name="gla_o")


def _ffn(pre, w_up, w_gate, w_down, layer, conv_w, conv_b, *, lp, tiles):
    f = w_up.shape[2]
    act, w_down_bf = _ffn_in(pre, w_up, w_gate, w_down, layer, conv_w, conv_b,
                             lp=lp, tm=tiles["ffn"], tn=_col_tile(f, (256, 128)))
    return _matmul(act, w_down_bf, tm=tiles["down"], tn=_col_tile(w_down.shape[2], (256, 128)),
                   out_dtype=BF16, name="ffn_down")


def kernel(x, meta, norm_g, rw_mu, rw_w0, rw_w1, rw_w2, rw_a0, rw_a1, rw_a2, rw_g1, rw_g2, rw_k_k, rw_k_a, rw_r_k, rw_wr, rw_wk, rw_wv, rw_wo, rw_lnx_g, rw_lnx_b, gla_w_in, gla_a1, gla_a2, gla_a_b, gla_r_b, gla_norm_g, gla_wo, ffn_up, ffn_gate, ffn_conv, ffn_conv_b, ffn_down):
    batch, seq, d = x.shape
    n_meta = meta.shape[0]
    depth = norm_g.shape[0]
    length = n_meta + seq
    lp = -(-length // CHUNK) * CHUNK
    m = jnp.broadcast_to(meta.astype(x.dtype)[None], (batch, n_meta, d))
    h = jnp.concatenate([m, x, jnp.zeros((batch, lp - length, d), x.dtype)], axis=1).reshape(batch * lp, d)
    tiles = _tiles(lp)

    pre = None
    for i in range(depth):
        j = i // 2
        if i % 2 == 0:
            p = dict(mu=rw_mu[j], w0=rw_w0[j], w1=rw_w1[j], w2=rw_w2[j], a0=rw_a0[j], a1=rw_a1[j],
                     a2=rw_a2[j], g1=rw_g1[j], g2=rw_g2[j], k_k=rw_k_k[j], k_a=rw_k_a[j], r_k=rw_r_k[j],
                     wr=rw_wr[j], wk=rw_wk[j], wv=rw_wv[j], wo=rw_wo[j], lnx_g=rw_lnx_g[j],
                     lnx_b=rw_lnx_b[j])
            mix = _rwkv_layer(h, norm_g[i, 0], p, batch=batch, lp=lp, tiles=tiles)
        else:
            p = dict(w_in=gla_w_in[j], a1=gla_a1[j], a2=gla_a2[j], a_b=gla_a_b[j], r_b=gla_r_b[j],
                     norm_g=gla_norm_g[j], wo=gla_wo[j])
            mix = _gla_layer(pre, p, batch=batch, tiles=tiles)
        h, pre = _post_norm(h, mix, norm_g[i, 1], norm_g[i, 2], tm=tiles["ew"])
        f = _ffn(pre, ffn_up, ffn_gate, ffn_down, i, ffn_conv[i], ffn_conv_b[i], lp=lp, tiles=tiles)
        if i + 1 < depth:
            h, pre = _post_norm(h, f, norm_g[i, 3], norm_g[i + 1, 0], tm=tiles["ew"])
    return _post_norm_final(h, f, norm_g[depth - 1, 3], batch=batch, lp=lp, start=n_meta, seq=seq,
                            tm=_pick(seq, (256, 128, 64)))
```

```python
import functools

import numpy as np
import jax
import jax.numpy as jnp
from jax import lax
from jax.experimental import pallas as pl
from jax.experimental.pallas import tpu as pltpu

F32 = jnp.float32
BF16 = jnp.bfloat16

NORM_EPS = 1e-6
RW_LNX_EPS = 64e-5
RW_HEAD = 64
RW_HEAD_SHIFT = RW_HEAD.bit_length() - 1
INV_BLOCK_SHIFT = 4
DECAY_SCALE = float(np.exp(-0.5))
GLA_GATE_TAU = 16.0
CHUNK = 64
LANES = 128
BF16_SUBLANES = 16
GROUP = LANES
HEADS_PER_GROUP = GROUP // RW_HEAD
VMEM_LIMIT = 56 * 1024 * 1024


def _cparams(*sem):
    return pltpu.CompilerParams(dimension_semantics=sem, vmem_limit_bytes=VMEM_LIMIT)


def _pick(n, cands):
    for c in cands:
        if n % c == 0:
            return c
    raise ValueError(f"no tile for {n} in {cands}")


def _rms(x):
    return x * lax.rsqrt(jnp.mean(x * x, axis=-1, keepdims=True) + NORM_EPS)


def _sigmoid(z):
    return 1.0 / (1.0 + jnp.exp(-z))


def _dot(a, b):
    return jnp.dot(a, b, preferred_element_type=F32)


def _dot_nt(a, b):
    return lax.dot_general(a, b, (((1,), (1,)), ((), ())), preferred_element_type=F32)


def _dot_tn(a, b):
    return lax.dot_general(a, b, (((0,), (0,)), ((), ())), preferred_element_type=F32)


def _split(x):
    hi = x.astype(BF16)
    lo = (x - hi.astype(F32)).astype(BF16)
    return hi, lo


def _mm_kernel(x_ref, w_ref, o_ref):
    acc = _dot(x_ref[...], w_ref[...].astype(BF16))
    o_ref[...] = acc.astype(o_ref.dtype)


def _matmul(x, w, *, tm, tn, out_dtype=F32, name):
    t, k = x.shape
    n = w.shape[1]
    return pl.pallas_call(
        _mm_kernel, grid=(t // tm, n // tn),
        in_specs=[pl.BlockSpec((tm, k), lambda i, j: (i, 0)),
                  pl.BlockSpec((k, tn), lambda i, j: (0, j))],
        out_specs=pl.BlockSpec((tm, tn), lambda i, j: (i, j)),
        out_shape=jax.ShapeDtypeStruct((t, n), out_dtype),
        compiler_params=_cparams("parallel", "arbitrary"), name=name,
    )(x, w)


def _lora_kernel(x_ref, w1_ref, w2_ref, b_ref, o_ref, *, act1, act2):
    t = _dot(x_ref[...], w1_ref[...])
    if act1 is not None:
        t = act1(t)
    y = _dot(t.astype(BF16), w2_ref[...]) + b_ref[...]
    if act2 is not None:
        y = act2(y)
    o_ref[...] = y


def _pad_rank(w1, w2):
    r = w1.shape[1]
    rp = -(-r // LANES) * LANES
    return (jnp.pad(w1, ((0, 0), (0, rp - r))).astype(BF16),
            jnp.pad(w2, ((0, rp - r), (0, 0))).astype(BF16))


def _lora(x, w1, w2, bias, *, act1, act2, tm, name):
    t, k = x.shape
    w1, w2 = _pad_rank(w1, w2)
    rp = w1.shape[1]
    n = w2.shape[1]
    return pl.pallas_call(
        functools.partial(_lora_kernel, act1=act1, act2=act2),
        grid=(t // tm,),
        in_specs=[pl.BlockSpec((tm, k), lambda i: (i, 0)),
                  pl.BlockSpec((k, rp), lambda i: (0, 0)),
                  pl.BlockSpec((rp, n), lambda i: (0, 0)),
                  pl.BlockSpec((1, n), lambda i: (0, 0))],
        out_specs=pl.BlockSpec((tm, n), lambda i: (i, 0)),
        out_shape=jax.ShapeDtypeStruct((t, n), F32),
        compiler_params=_cparams("parallel"), name=name,
    )(x, w1, w2, bias.reshape(1, n).astype(F32))


def _rw_input_kernel(h_ref, halo_ref, g_ref, mu_ref, w1_ref, w2_ref, a1_ref, a2_ref, g1_ref, g2_ref,
                     w0_ref, a0_ref, xr_ref, xk_ref, xv_ref, wraw_ref, a_ref, gate_ref, *, blocks_per_seq):
    i = pl.program_id(0)
    g = g_ref[...]
    pre = _rms(h_ref[...]) * g
    halo = (_rms(halo_ref[...]) * g)[7:8, :]
    halo = jnp.where(i % blocks_per_seq == 0, 0.0, halo)
    row = lax.broadcasted_iota(jnp.int32, pre.shape, 0)
    prev = jnp.where(row == 0, halo, pltpu.roll(pre, 1, 0))
    xx = prev - pre
    mix = lambda m: (pre + xx * mu_ref[m:m + 1, :]).astype(BF16)
    xr_ref[...] = mix(0)
    xk_ref[...] = mix(2)
    xv_ref[...] = mix(3)
    wraw_ref[...] = w0_ref[...] + _dot(jnp.tanh(_dot(mix(1), w1_ref[...])).astype(BF16), w2_ref[...])
    a_ref[...] = _sigmoid(a0_ref[...] + _dot(_dot(mix(4), a1_ref[...]).astype(BF16), a2_ref[...]))
    gate_ref[...] = _dot(_sigmoid(_dot(mix(5), g1_ref[...])).astype(BF16), g2_ref[...])


def _rw_input(h, g, p, *, lp, tm):
    t, d = h.shape
    row_spec = pl.BlockSpec((tm, d), lambda i: (i, 0))
    const = lambda arr: pl.BlockSpec(arr.shape, lambda i: (0, 0), pipeline_mode=pl.Buffered(1))
    lora_w = [w for pair in (("w1", "w2"), ("a1", "a2"), ("g1", "g2")) for w in _pad_rank(p[pair[0]], p[pair[1]])]
    vec = lambda a: a.reshape(1, d)
    consts = [vec(g), p["mu"], *lora_w, vec(p["w0"]), vec(p["a0"])]
    return pl.pallas_call(
        functools.partial(_rw_input_kernel, blocks_per_seq=lp // tm),
        grid=(t // tm,),
        in_specs=[row_spec, pl.BlockSpec((8, d), lambda i: (jnp.maximum(i * (tm // 8) - 1, 0), 0))]
                 + [const(c) for c in consts],
        out_specs=[row_spec] * 6,
        out_shape=[jax.ShapeDtypeStruct((t, d), BF16)] * 3 + [jax.ShapeDtypeStruct((t, d), F32)] * 3,
        compiler_params=_cparams("parallel"), name="rw_input",
    )(h, h, *consts)


def _rwkv_kernel(r_ref, k_ref, v_ref, w_ref, a_ref, gate_ref, kk_ref, ka_ref, rk_ref, lg_ref, lb_ref,
                 o_ref, s_ref, *, groups):
    c_id = pl.program_id(2)

    @pl.when(c_id == 0)
    def _():
        s_ref[...] = jnp.zeros_like(s_ref)

    C, W = CHUNK, GROUP
    row = lax.broadcasted_iota(jnp.int32, (C, W), 0)
    lane = lax.broadcasted_iota(jnp.int32, (C, W), 1)
    lane_in = lane & (RW_HEAD - 1)
    lane_head = lane >> RW_HEAD_SHIFT
    tril_incl = lane_in <= row
    tril_strict = lane_in < row
    eye = jnp.where(lane_in == row, 1.0, 0.0)
    diag_block = (lane_in >> INV_BLOCK_SHIFT) == (row >> INV_BLOCK_SHIFT)
    bd_mask = (lax.broadcasted_iota(jnp.int32, (HEADS_PER_GROUP * C, W), 0) >> RW_HEAD_SHIFT) == \
              (lax.broadcasted_iota(jnp.int32, (HEADS_PER_GROUP * C, W), 1) >> RW_HEAD_SHIFT)
    ones_bd = jnp.where(bd_mask, 1.0, 0.0).astype(BF16)
    ones_bd2 = jnp.concatenate([ones_bd, ones_bd], axis=0)
    ltri = jnp.where(lax.broadcasted_iota(jnp.int32, (C, C), 1) <=
                     lax.broadcasted_iota(jnp.int32, (C, C), 0), 1.0, 0.0).astype(BF16)
    gs = range(groups)
    cols = lambda ref, g: ref[:, g * W:(g + 1) * W]

    def bd(x):
        return jnp.concatenate([x] * HEADS_PER_GROUP, axis=0) * ones_bd

    def seg_sum(x):
        return _dot(jnp.concatenate(_split(x), axis=1), ones_bd2)

    def head_mm(lhs, ys):
        top = jnp.concatenate([bd(ys[0]), bd(ys[1])], axis=1)
        res = _dot(jnp.concatenate([jnp.concatenate(x, axis=1) for x in lhs], axis=0),
                   jnp.concatenate([top, top], axis=0))
        return [res[i * C:(i + 1) * C, :W] + res[i * C:(i + 1) * C, W:] for i in range(len(lhs))]

    k = [cols(k_ref, g) for g in gs]
    v = [cols(v_ref, g).astype(BF16) for g in gs]
    a = [cols(a_ref, g) for g in gs]
    ld = [-DECAY_SCALE * _sigmoid(cols(w_ref, g)) for g in gs]
    kk = [k[g] * cols(kk_ref, g) for g in gs]
    kk = [kk[g] * lax.rsqrt(jnp.maximum(seg_sum(kk[g] * kk[g]), 1e-24)) for g in gs]
    kh = [k[g] * (1.0 + (a[g] - 1.0) * cols(ka_ref, g)) for g in gs]
    rb = [kk[g] * a[g] for g in gs]

    cum2 = [_dot(ltri, jnp.concatenate(_split(ld[g]), axis=1)) for g in gs]
    cum = [cum2[g][:, :W] + cum2[g][:, W:] for g in gs]
    c_last = [cum[g][C - 1:C, :] for g in gs]
    e_nc = [jnp.exp(-cum[g]) for g in gs]
    lhs = [jnp.concatenate([-kk[g] * jnp.exp(cum[g] - ld[g]), cols(r_ref, g) * jnp.exp(cum[g])],
                           axis=0).astype(BF16) for g in gs]
    b_hat = [(rb[g] * e_nc[g]).astype(BF16) for g in gs]
    k_hat = [(kh[g] * e_nc[g]).astype(BF16) for g in gs]
    gram = [_dot_nt(lhs[g], jnp.concatenate([bd(b_hat[g]), bd(k_hat[g])], axis=0)) for g in gs]
    n_ab = [jnp.where(tril_strict, gram[g][:C, :W], 0.0) for g in gs]
    b_rb = [jnp.where(tril_incl, gram[g][C:, :W], 0.0).astype(BF16) for g in gs]
    ak_rk = [jnp.concatenate([jnp.where(tril_strict, gram[g][:C, W:], 0.0),
                              jnp.where(tril_incl, gram[g][C:, W:], 0.0)], axis=0).astype(BF16) for g in gs]

    assert CHUNK == 4 << INV_BLOCK_SHIFT and INV_BLOCK_SHIFT == 4
    nd = [jnp.where(diag_block, n_ab[g], 0.0) for g in gs]
    no_s = [_split(n_ab[g] - nd[g]) for g in gs]
    pd = [eye + nd[g] for g in gs]
    x_s = [_split(nd[g]) for g in gs]
    x_s = [_split(head_mm([x_s[g]], x_s[g])[0]) for g in gs]
    for _ in range(2):
        prod = [head_mm([x_s[g], _split(pd[g])], x_s[g]) for g in gs]
        pd = [pd[g] + prod[g][1] for g in gs]
        x_s = [_split(prod[g][0]) for g in gs]
    pd = [pd[g] + head_mm([_split(pd[g])], x_s[g])[0] for g in gs]
    pd_s = [_split(pd[g]) for g in gs]
    m = [head_mm([pd_s[g]], no_s[g])[0] for g in gs]
    m_s = [_split(m[g]) for g in gs]
    m2_s = [_split(head_mm([m_s[g]], m_s[g])[0]) for g in gs]
    q = [eye + m[g] for g in gs]
    q_s = [_split(q[g] + head_mm([_split(q[g])], m2_s[g])[0]) for g in gs]
    p_s = [_split(head_mm([q_s[g]], pd_s[g])[0]) for g in gs]

    s_t = [s_ref[g] for g in gs]
    zz = [_dot(jnp.concatenate([lhs[g], ak_rk[g]], axis=1),
               jnp.concatenate([bd(s_t[g].astype(BF16)), bd(v[g])], axis=0)) for g in gs]
    sa = [head_mm([p_s[g]], _split(zz[g][:C]))[0].astype(BF16) for g in gs]
    y = [zz[g][C:] + _dot(b_rb[g], bd(sa[g])) for g in gs]

    yc = [y[g] - seg_sum(y[g]) * (1.0 / RW_HEAD) for g in gs]
    var = [seg_sum(yc[g] * yc[g]) * (1.0 / RW_HEAD) for g in gs]
    kh = [cols(k_ref, g) * (1.0 + (cols(a_ref, g) - 1.0) * cols(ka_ref, g)) for g in gs]
    bonus = [seg_sum(cols(r_ref, g) * kh[g] * cols(rk_ref, g)) * cols(v_ref, g) for g in gs]
    for g in gs:
        yn = yc[g] * lax.rsqrt(var[g] + RW_LNX_EPS) * cols(lg_ref, g) + cols(lb_ref, g)
        o_ref[:, g * W:(g + 1) * W] = ((yn + bonus[g]) * cols(gate_ref, g)).astype(BF16)

    full = [_dot_tn(jnp.concatenate([b_hat[g], k_hat[g]], axis=0), jnp.concatenate([sa[g], v[g]], axis=0))
            for g in gs]
    c_col = [seg_sum(eye * c_last[g]) for g in gs]
    for g in gs:
        gain = s_t[g]
        for h in range(HEADS_PER_GROUP):
            gain = gain + jnp.where(lane_head == h, full[g][h * C:(h + 1) * C, :], 0.0)
        s_ref[g] = jnp.exp(c_col[g]) * gain


def _rwkv_recurrence(r, k, v, w_raw, a, gate, k_k, k_a, r_k, lnx_g, lnx_b, *, batch, groups_per_step):
    t, d = r.shape
    chunks = t // (batch * CHUNK)
    gb = min(groups_per_step, d // GROUP)
    wb = gb * GROUP
    seq_spec = pl.BlockSpec((CHUNK, wb), lambda b, g, c: (b * chunks + c, g))
    par_spec = pl.BlockSpec((1, wb), lambda b, g, c: (0, g))
    par = lambda p: p.reshape(1, d)
    return pl.pallas_call(
        functools.partial(_rwkv_kernel, groups=gb),
        grid=(batch, d // wb, chunks),
        in_specs=[seq_spec] * 6 + [par_spec] * 5,
        out_specs=seq_spec,
        out_shape=jax.ShapeDtypeStruct((t, d), BF16),
        scratch_shapes=[pltpu.VMEM((gb, CHUNK, GROUP), F32)],
        compiler_params=_cparams("parallel", "parallel", "arbitrary"), name="rwkv_recurrence",
    )(r, k, v, w_raw, a, gate, par(k_k), par(k_a), par(r_k), par(lnx_g), par(lnx_b))


def _post_norm_kernel(h_ref, m_ref, gp_ref, *rest):
    h = h_ref[...] + _rms(m_ref[...].astype(F32)) * gp_ref[...]
    if len(rest) == 1:
        rest[0][...] = h
    else:
        gn_ref, h_out, p_out = rest
        h_out[...] = h
        p_out[...] = (_rms(h) * gn_ref[...]).astype(BF16)


def _post_norm_final(h, m, g_post, *, batch, lp, start, seq, tm):
    d = h.shape[1]
    align = BF16_SUBLANES
    assert lp % align == 0 and start % align == 0 and tm % align == 0
    in_spec = pl.BlockSpec((pl.Element(tm), pl.Element(d)),
                           lambda b, i: (pl.multiple_of(b * lp + start + i * tm, align), 0))
    return pl.pallas_call(
        _post_norm_kernel, grid=(batch, seq // tm),
        in_specs=[in_spec, in_spec, pl.BlockSpec((1, d), lambda b, i: (0, 0))],
        out_specs=pl.BlockSpec((None, tm, d), lambda b, i: (b, i, 0)),
        out_shape=jax.ShapeDtypeStruct((batch, seq, d), F32),
        compiler_params=_cparams("parallel", "parallel"), name="post_norm_final",
    )(h, m, g_post.reshape(1, d))


def _post_norm(h, m, g_post, g_next, *, tm):
    t, d = h.shape
    row_spec = pl.BlockSpec((tm, d), lambda i: (i, 0))
    par_spec = pl.BlockSpec((1, d), lambda i: (0, 0))
    h_shape = jax.ShapeDtypeStruct((t, d), F32)
    return pl.pallas_call(
        _post_norm_kernel, grid=(t // tm,),
        in_specs=[row_spec, row_spec, par_spec, par_spec],
        out_specs=[row_spec, row_spec],
        out_shape=[h_shape, jax.ShapeDtypeStruct((t, d), BF16)],
        compiler_params=_cparams("parallel"), name="post_norm",
    )(h, m, g_post.reshape(1, d), g_next.reshape(1, d))


def _ffn_in_kernel(x_ref, halo_ref, wu_ref, wg_ref, cw_ref, cb_ref, wd_ref, o_ref, wd_out, *,
                   blocks_per_seq, halo_rows, sub):
    i = pl.program_id(0)
    wd_out[...] = wd_ref[...].astype(BF16)
    wu = wu_ref[...].astype(BF16)
    wg = wg_ref[...].astype(BF16)
    zh = _dot(halo_ref[...], wg)
    zh = jnp.where(i % blocks_per_seq == 0, 0.0, zh)
    prev1, prev2 = zh[halo_rows - 1:halo_rows, :], zh[halo_rows - 2:halo_rows - 1, :]
    row = lax.broadcasted_iota(jnp.int32, (sub, wu.shape[1]), 0)
    n_sub = x_ref.shape[0] // sub

    def project(s):
        xs = x_ref[s * sub:(s + 1) * sub, :]
        return _dot(xs, wu), _dot(xs, wg)

    cur = project(0)
    for s in range(n_sub):
        nxt = project(s + 1) if s + 1 < n_sub else None
        u, z = cur
        z1 = jnp.where(row == 0, prev1, pltpu.roll(z, 1, 0))
        z2 = pltpu.roll(z, 2, 0)
        z2 = jnp.where(row == 1, prev1, z2)
        z2 = jnp.where(row == 0, prev2, z2)
        zc = z * cw_ref[2:3, :] + z1 * cw_ref[1:2, :] + z2 * cw_ref[0:1, :] + cb_ref[...]
        o_ref[s * sub:(s + 1) * sub, :] = (zc * _sigmoid(zc) * u).astype(BF16)
        prev1, prev2 = z[sub - 1:sub, :], z[sub - 2:sub - 1, :]
        cur = nxt


def _ffn_in(x, w_up, w_gate, w_down, layer, conv_w, conv_b, *, lp, tm, tn):
    t, d = x.shape
    f = w_up.shape[2]
    hr = BF16_SUBLANES
    ni, nj = t // tm, f // tn
    slab = f // (ni * nj)
    assert slab * ni * nj == f and slab % hr == 0
    w_spec = pl.BlockSpec((None, d, tn), lambda i, j: (layer, 0, j))
    return pl.pallas_call(
        functools.partial(_ffn_in_kernel, blocks_per_seq=lp // tm, halo_rows=hr,
                          sub=_pick(tm, (352, 96, 64))),
        grid=(ni, nj),
        in_specs=[pl.BlockSpec((tm, d), lambda i, j: (i, 0), pipeline_mode=pl.Buffered(1)),
                  pl.BlockSpec((hr, d), lambda i, j: (jnp.maximum(i * (tm // hr) - 1, 0), 0)),
                  w_spec, w_spec,
                  pl.BlockSpec((conv_w.shape[0], tn), lambda i, j: (0, j)),
                  pl.BlockSpec((1, tn), lambda i, j: (0, j)),
                  pl.BlockSpec((None, slab, d), lambda i, j: (layer, i * nj + j, 0))],
        out_specs=[pl.BlockSpec((tm, tn), lambda i, j: (i, j)),
                   pl.BlockSpec((slab, d), lambda i, j: (i * nj + j, 0))],
        out_shape=[jax.ShapeDtypeStruct((t, f), BF16), jax.ShapeDtypeStruct((f, d), BF16)],
        compiler_params=_cparams("parallel", "arbitrary"), name="ffn_in",
    )(x, x, w_up, w_gate, conv_w, conv_b.reshape(1, f), w_down)


GLA_LEVELS = 6


def _gla_select_matrices():
    t_i = np.arange(CHUNK)[:, None]
    u_i = np.arange(CHUNK)[None, :]
    mats = [u_i <= t_i]
    for l in range(GLA_LEVELS):
        half = 1 << l
        p = (t_i & ~(2 * half - 1)) + (half - 1)
        mats.append((u_i > p) & (u_i <= t_i))
        mats.append((u_i > t_i) & (u_i <= p))
    sel = np.concatenate(mats, axis=0).astype(np.float32)
    return jnp.asarray(np.concatenate([sel, sel], axis=1), dtype=BF16)


def _gla_kernel(sel_ref, q_ref, k_ref, v_ref, g_ref, r_ref, rb_ref, ng_ref, o_ref, s_ref, *, heads, hk, hv):
    c_id = pl.program_id(2)

    @pl.when(c_id == 0)
    def _():
        s_ref[...] = jnp.zeros_like(s_ref)

    C = CHUNK
    t_i = lax.broadcasted_iota(jnp.int32, (C, C), 0)
    u_i = lax.broadcasted_iota(jnp.int32, (C, C), 1)
    row = lax.broadcasted_iota(jnp.int32, (C, 1), 0)
    eye = t_i == u_i
    hs = range(heads)
    sel = sel_ref[...]

    q = [q_ref[:, h * hk:(h + 1) * hk] * (hk ** -0.5) for h in hs]
    k = [k_ref[:, h * hk:(h + 1) * hk] for h in hs]
    v = [v_ref[:, h * hv:(h + 1) * hv].astype(BF16) for h in hs]
    g_s = [_split(g_ref[:, h * hk:(h + 1) * hk]) for h in hs]
    cums = [_dot(sel, jnp.concatenate(g_s[h], axis=0)) for h in hs]
    b = [cums[h][:C] for h in hs]
    b_last = [b[h][C - 1:C, :] for h in hs]

    a = [jnp.where(eye, jnp.sum(q[h] * k[h], axis=-1, keepdims=True), 0.0) for h in hs]
    for l in range(GLA_LEVELS):
        upper = (row & (1 << l)) != 0
        same_block = (t_i >> (l + 1)) == (u_i >> (l + 1))
        ql = [jnp.where(upper, q[h] * jnp.exp(cums[h][(1 + 2 * l) * C:(2 + 2 * l) * C]), 0.0).astype(BF16)
              for h in hs]
        kl = [jnp.where(upper, 0.0, k[h] * jnp.exp(cums[h][(2 + 2 * l) * C:(3 + 2 * l) * C])).astype(BF16)
              for h in hs]
        a = [a[h] + jnp.where(same_block, _dot_nt(ql[h], kl[h]), 0.0) for h in hs]

    s_t = [s_ref[h] for h in hs]
    qe = [(q[h] * jnp.exp(b[h])).astype(BF16) for h in hs]
    kr = [(k[h] * jnp.exp(b_last[h] - b[h])).astype(BF16) for h in hs]
    o = [_dot_nt(qe[h], s_t[h].astype(BF16)) + _dot(a[h].astype(BF16), v[h]) for h in hs]
    for h in hs:
        s_ref[h] = s_t[h] * jnp.exp(b_last[h]) + _dot_tn(v[h], kr[h])
    for h in hs:
        zg = r_ref[:, h * hv:(h + 1) * hv] + rb_ref[:, h * hv:(h + 1) * hv]
        o_ref[:, h * hv:(h + 1) * hv] = (_rms(o[h]) * ng_ref[...] * (zg * _sigmoid(zg))).astype(BF16)


def _gla_recurrence(proj, g, r_b, norm_g, *, batch, heads, hk, hv, heads_per_step):
    t = proj.shape[0]
    hb = heads_per_step
    chunks = t // (batch * CHUNK)
    dk, dv = heads * hk, heads * hv
    sel = _gla_select_matrices()

    def cols(width, start):
        assert start % (hb * width) == 0
        off = start // (hb * width)
        return pl.BlockSpec((CHUNK, hb * width), lambda b, h, c: (b * chunks + c, h + off))

    return pl.pallas_call(
        functools.partial(_gla_kernel, heads=hb, hk=hk, hv=hv),
        grid=(batch, heads // hb, chunks),
        in_specs=[pl.BlockSpec(sel.shape, lambda b, h, c: (0, 0)),
                  cols(hk, 0), cols(hk, dk), cols(hv, 2 * dk),
                  pl.BlockSpec((CHUNK, hb * hk), lambda b, h, c: (b * chunks + c, h)),
                  cols(hv, 2 * dk + dv),
                  pl.BlockSpec((1, hb * hv), lambda b, h, c: (0, h)),
                  pl.BlockSpec((1, hv), lambda b, h, c: (0, 0))],
        out_specs=pl.BlockSpec((CHUNK, hb * hv), lambda b, h, c: (b * chunks + c, h)),
        out_shape=jax.ShapeDtypeStruct((t, dv), BF16),
        scratch_shapes=[pltpu.VMEM((hb, hv, hk), F32)],
        compiler_params=_cparams("parallel", "parallel", "arbitrary"), name="gla_recurrence",
    )(sel, proj, proj, proj, g, proj, r_b.reshape(1, dv), norm_g.reshape(1, hv))


def _tiles(lp):
    return dict(
        ew=_pick(lp, (192, 128, 64)),
        norm=_pick(lp, (352, 192, 128, 64)),
        lora=_pick(lp, (352, 192, 128, 64)),
        mm=_pick(lp, (1056, 704, 352, 192, 128, 64)),
        down=_pick(lp, (704, 1056, 352, 192, 128, 64)),
        ffn=_pick(lp, (2112, 1056, 704, 352, 192, 128, 64)),
        rw_groups=16,
        gla_heads=4,
    )


def _col_tile(n, cands=(512, 256, 128)):
    return _pick(n, cands)


def _rwkv_layer(h, g_pre, p, *, batch, lp, tiles):
    d = h.shape[1]
    xr, xk, xv, w_raw, a, gate = _rw_input(h, g_pre, p, lp=lp, tm=tiles["ew"])
    tn = _col_tile(d)
    proj = lambda x, w, name: _matmul(x, w, tm=tiles["mm"], tn=tn, name=name)
    r = proj(xr, p["wr"], "rw_r")
    k = proj(xk, p["wk"], "rw_k")
    v = proj(xv, p["wv"], "rw_v")
    yo = _rwkv_recurrence(r, k, v, w_raw, a, gate, p["k_k"], p["k_a"], p["r_k"].reshape(-1),
                          p["lnx_g"], p["lnx_b"], batch=batch, groups_per_step=tiles["rw_groups"])
    return _matmul(yo, p["wo"], tm=tiles["mm"], tn=tn, out_dtype=BF16, name="rw_o")


def _log_gate(x):
    return (jnp.minimum(x, 0.0) - jnp.log(1.0 + jnp.exp(-jnp.abs(x)))) * (1.0 / GLA_GATE_TAU)


def _gla_layer(pre, p, *, batch, tiles):
    d = pre.shape[1]
    dk = p["a2"].shape[1]
    dv = p["wo"].shape[0]
    heads = max(4, d // 512)
    hk, hv = dk // heads, dv // heads
    proj = _matmul(pre, p["w_in"], tm=tiles["mm"], tn=_col_tile(dk), name="gla_in")
    g = _lora(pre, p["a1"], p["a2"], p["a_b"], act1=None, act2=_log_gate, tm=tiles["lora"], name="gla_gate")
    og = _gla_recurrence(proj, g, p["r_b"], p["norm_g"], batch=batch, heads=heads, hk=hk, hv=hv,
                         heads_per_step=min(tiles["gla_heads"], heads))
    return _matmul(og, p["wo"], tm=tiles["mm"], tn=_col_tile(d), out_dtype=BF16, name="gla_o")


def _ffn(pre, w_up, w_gate, w_down, layer, conv_w, conv_b, *, lp, tiles):
    f = w_up.shape[2]
    act, w_down_bf = _ffn_in(pre, w_up, w_gate, w_down, layer, conv_w, conv_b,
                             lp=lp, tm=tiles["ffn"], tn=_col_tile(f, (256, 128)))
    return _matmul(act, w_down_bf, tm=tiles["down"], tn=_col_tile(w_down.shape[2], (256, 128)),
                   out_dtype=BF16, name="ffn_down")


def kernel(x, meta, norm_g, rw_mu, rw_w0, rw_w1, rw_w2, rw_a0, rw_a1, rw_a2, rw_g1, rw_g2, rw_k_k, rw_k_a, rw_r_k, rw_wr, rw_wk, rw_wv, rw_wo, rw_lnx_g, rw_lnx_b, gla_w_in, gla_a1, gla_a2, gla_a_b, gla_r_b, gla_norm_g, gla_wo, ffn_up, ffn_gate, ffn_conv, ffn_conv_b, ffn_down):
    batch, seq, d = x.shape
    n_meta = meta.shape[0]
    depth = norm_g.shape[0]
    length = n_meta + seq
    lp = -(-length // CHUNK) * CHUNK
    m = jnp.broadcast_to(meta.astype(x.dtype)[None], (batch, n_meta, d))
    h = jnp.concatenate([m, x, jnp.zeros((batch, lp - length, d), x.dtype)], axis=1).reshape(batch * lp, d)
    tiles = _tiles(lp)

    pre = None
    for i in range(depth):
        j = i // 2
        if i % 2 == 0:
            p = dict(mu=rw_mu[j], w0=rw_w0[j], w1=rw_w1[j], w2=rw_w2[j], a0=rw_a0[j], a1=rw_a1[j],
                     a2=rw_a2[j], g1=rw_g1[j], g2=rw_g2[j], k_k=rw_k_k[j], k_a=rw_k_a[j], r_k=rw_r_k[j],
                     wr=rw_wr[j], wk=rw_wk[j], wv=rw_wv[j], wo=rw_wo[j], lnx_g=rw_lnx_g[j],
                     lnx_b=rw_lnx_b[j])
            mix = _rwkv_layer(h, norm_g[i, 0], p, batch=batch, lp=lp, tiles=tiles)
        else:
            p = dict(w_in=gla_w_in[j], a1=gla_a1[j], a2=gla_a2[j], a_b=gla_a_b[j], r_b=gla_r_b[j],
                     norm_g=gla_norm_g[j], wo=gla_wo[j])
            mix = _gla_layer(pre, p, batch=batch, tiles=tiles)
        h, pre = _post_norm(h, mix, norm_g[i, 1], norm_g[i, 2], tm=tiles["norm"])
        f = _ffn(pre, ffn_up, ffn_gate, ffn_down, i, ffn_conv[i], ffn_conv_b[i], lp=lp, tiles=tiles)
        if i + 1 < depth:
            h, pre = _post_norm(h, f, norm_g[i, 3], norm_g[i + 1, 0], tm=tiles["norm"])
    return _post_norm_final(h, f, norm_g[depth - 1, 3], batch=batch, lp=lp, start=n_meta, seq=seq,
                            tm=_pick(seq, (256, 128, 64)))
```

```python
import functools

import numpy as np
import jax
import jax.numpy as jnp
from jax import lax
from jax.experimental import pallas as pl
from jax.experimental.pallas import tpu as pltpu

F32 = jnp.float32
BF16 = jnp.bfloat16

NORM_EPS = 1e-6
RW_LNX_EPS = 64e-5
RW_HEAD = 64
RW_HEAD_SHIFT = RW_HEAD.bit_length() - 1
INV_BLOCK_SHIFT = 4
DECAY_SCALE = float(np.exp(-0.5))
GLA_GATE_TAU = 16.0
CHUNK = 64
LANES = 128
BF16_SUBLANES = 16
GROUP = LANES
HEADS_PER_GROUP = GROUP // RW_HEAD
VMEM_LIMIT = 56 * 1024 * 1024


def _cparams(*sem):
    return pltpu.CompilerParams(dimension_semantics=sem, vmem_limit_bytes=VMEM_LIMIT)


def _pick(n, cands):
    for c in cands:
        if n % c == 0:
            return c
    raise ValueError(f"no tile for {n} in {cands}")


def _rms(x):
    return x * lax.rsqrt(jnp.mean(x * x, axis=-1, keepdims=True) + NORM_EPS)


def _sigmoid(z):
    return 1.0 / (1.0 + jnp.exp(-z))


def _dot(a, b):
    return jnp.dot(a, b, preferred_element_type=F32)


def _dot_nt(a, b):
    return lax.dot_general(a, b, (((1,), (1,)), ((), ())), preferred_element_type=F32)


def _dot_tn(a, b):
    return lax.dot_general(a, b, (((0,), (0,)), ((), ())), preferred_element_type=F32)


def _split(x):
    hi = x.astype(BF16)
    lo = (x - hi.astype(F32)).astype(BF16)
    return hi, lo


def _mm_kernel(x_ref, w_ref, o_ref):
    acc = _dot(x_ref[...], w_ref[...].astype(BF16))
    o_ref[...] = acc.astype(o_ref.dtype)


def _matmul(x, w, *, tm, tn, out_dtype=F32, name):
    t, k = x.shape
    n = w.shape[1]
    return pl.pallas_call(
        _mm_kernel, grid=(t // tm, n // tn),
        in_specs=[pl.BlockSpec((tm, k), lambda i, j: (i, 0)),
                  pl.BlockSpec((k, tn), lambda i, j: (0, j))],
        out_specs=pl.BlockSpec((tm, tn), lambda i, j: (i, j)),
        out_shape=jax.ShapeDtypeStruct((t, n), out_dtype),
        compiler_params=_cparams("parallel", "arbitrary"), name=name,
    )(x, w)


def _pad_rank(w1, w2):
    r = w1.shape[1]
    rp = -(-r // LANES) * LANES
    return (jnp.pad(w1, ((0, 0), (0, rp - r))).astype(BF16),
            jnp.pad(w2, ((0, rp - r), (0, 0))).astype(BF16))


def _rw_input_kernel(h_ref, halo_ref, g_ref, mu_ref, w1_ref, w2_ref, a1_ref, a2_ref, g1_ref, g2_ref,
                     w0_ref, a0_ref, xr_ref, xk_ref, xv_ref, wraw_ref, a_ref, gate_ref, *, blocks_per_seq):
    i = pl.program_id(0)
    g = g_ref[...]
    pre = _rms(h_ref[...]) * g
    halo = (_rms(halo_ref[...]) * g)[7:8, :]
    halo = jnp.where(i % blocks_per_seq == 0, 0.0, halo)
    row = lax.broadcasted_iota(jnp.int32, pre.shape, 0)
    prev = jnp.where(row == 0, halo, pltpu.roll(pre, 1, 0))
    xx = prev - pre
    mix = lambda m: (pre + xx * mu_ref[m:m + 1, :]).astype(BF16)
    xr_ref[...] = mix(0)
    xk_ref[...] = mix(2)
    xv_ref[...] = mix(3)
    wraw_ref[...] = w0_ref[...] + _dot(jnp.tanh(_dot(mix(1), w1_ref[...])).astype(BF16), w2_ref[...])
    a_ref[...] = _sigmoid(a0_ref[...] + _dot(_dot(mix(4), a1_ref[...]).astype(BF16), a2_ref[...]))
    gate_ref[...] = _dot(_sigmoid(_dot(mix(5), g1_ref[...])).astype(BF16), g2_ref[...])


def _rw_input(h, g, p, *, lp, tm):
    t, d = h.shape
    row_spec = pl.BlockSpec((tm, d), lambda i: (i, 0))
    const = lambda arr: pl.BlockSpec(arr.shape, lambda i: (0, 0), pipeline_mode=pl.Buffered(1))
    lora_w = [w for pair in (("w1", "w2"), ("a1", "a2"), ("g1", "g2")) for w in _pad_rank(p[pair[0]], p[pair[1]])]
    vec = lambda a: a.reshape(1, d)
    consts = [vec(g), p["mu"], *lora_w, vec(p["w0"]), vec(p["a0"])]
    return pl.pallas_call(
        functools.partial(_rw_input_kernel, blocks_per_seq=lp // tm),
        grid=(t // tm,),
        in_specs=[row_spec, pl.BlockSpec((8, d), lambda i: (jnp.maximum(i * (tm // 8) - 1, 0), 0))]
                 + [const(c) for c in consts],
        out_specs=[row_spec] * 6,
        out_shape=[jax.ShapeDtypeStruct((t, d), BF16)] * 3 + [jax.ShapeDtypeStruct((t, d), F32)] * 3,
        compiler_params=_cparams("parallel"), name="rw_input",
    )(h, h, *consts)


def _rwkv_kernel(r_ref, k_ref, v_ref, w_ref, a_ref, gate_ref, kk_ref, ka_ref, rk_ref, lg_ref, lb_ref,
                 o_ref, s_ref, *, groups):
    c_id = pl.program_id(2)

    @pl.when(c_id == 0)
    def _():
        s_ref[...] = jnp.zeros_like(s_ref)

    C, W = CHUNK, GROUP
    row = lax.broadcasted_iota(jnp.int32, (C, W), 0)
    lane = lax.broadcasted_iota(jnp.int32, (C, W), 1)
    lane_in = lane & (RW_HEAD - 1)
    lane_head = lane >> RW_HEAD_SHIFT
    tril_incl = lane_in <= row
    tril_strict = lane_in < row
    eye = jnp.where(lane_in == row, 1.0, 0.0)
    diag_block = (lane_in >> INV_BLOCK_SHIFT) == (row >> INV_BLOCK_SHIFT)
    bd_mask = (lax.broadcasted_iota(jnp.int32, (HEADS_PER_GROUP * C, W), 0) >> RW_HEAD_SHIFT) == \
              (lax.broadcasted_iota(jnp.int32, (HEADS_PER_GROUP * C, W), 1) >> RW_HEAD_SHIFT)
    ones_bd = jnp.where(bd_mask, 1.0, 0.0).astype(BF16)
    ones_bd2 = jnp.concatenate([ones_bd, ones_bd], axis=0)
    ltri = jnp.where(lax.broadcasted_iota(jnp.int32, (C, C), 1) <=
                     lax.broadcasted_iota(jnp.int32, (C, C), 0), 1.0, 0.0).astype(BF16)
    gs = range(groups)
    cols = lambda ref, g: ref[:, g * W:(g + 1) * W]

    def bd(x):
        return jnp.concatenate([x] * HEADS_PER_GROUP, axis=0) * ones_bd

    def seg_sum(x):
        return _dot(jnp.concatenate(_split(x), axis=1), ones_bd2)

    def head_mm(lhs, ys):
        top = jnp.concatenate([bd(ys[0]), bd(ys[1])], axis=1)
        res = _dot(jnp.concatenate([jnp.concatenate(x, axis=1) for x in lhs], axis=0),
                   jnp.concatenate([top, top], axis=0))
        return [res[i * C:(i + 1) * C, :W] + res[i * C:(i + 1) * C, W:] for i in range(len(lhs))]

    k = [cols(k_ref, g) for g in gs]
    v = [cols(v_ref, g).astype(BF16) for g in gs]
    a = [cols(a_ref, g) for g in gs]
    ld = [-DECAY_SCALE * _sigmoid(cols(w_ref, g)) for g in gs]
    kk = [k[g] * cols(kk_ref, g) for g in gs]
    kk = [kk[g] * lax.rsqrt(jnp.maximum(seg_sum(kk[g] * kk[g]), 1e-24)) for g in gs]
    kh = [k[g] * (1.0 + (a[g] - 1.0) * cols(ka_ref, g)) for g in gs]
    rb = [kk[g] * a[g] for g in gs]

    cum2 = [_dot(ltri, jnp.concatenate(_split(ld[g]), axis=1)) for g in gs]
    cum = [cum2[g][:, :W] + cum2[g][:, W:] for g in gs]
    c_last = [cum[g][C - 1:C, :] for g in gs]
    e_nc = [jnp.exp(-cum[g]) for g in gs]
    lhs = [jnp.concatenate([-kk[g] * jnp.exp(cum[g] - ld[g]), cols(r_ref, g) * jnp.exp(cum[g])],
                           axis=0).astype(BF16) for g in gs]
    b_hat = [(rb[g] * e_nc[g]).astype(BF16) for g in gs]
    k_hat = [(kh[g] * e_nc[g]).astype(BF16) for g in gs]
    gram = [_dot_nt(lhs[g], jnp.concatenate([bd(b_hat[g]), bd(k_hat[g])], axis=0)) for g in gs]
    n_ab = [jnp.where(tril_strict, gram[g][:C, :W], 0.0) for g in gs]
    b_rb = [jnp.where(tril_incl, gram[g][C:, :W], 0.0).astype(BF16) for g in gs]
    ak_rk = [jnp.concatenate([jnp.where(tril_strict, gram[g][:C, W:], 0.0),
                              jnp.where(tril_incl, gram[g][C:, W:], 0.0)], axis=0).astype(BF16) for g in gs]

    assert CHUNK == 4 << INV_BLOCK_SHIFT and INV_BLOCK_SHIFT == 4
    nd = [jnp.where(diag_block, n_ab[g], 0.0) for g in gs]
    no_s = [_split(n_ab[g] - nd[g]) for g in gs]
    pd = [eye + nd[g] for g in gs]
    x_s = [_split(nd[g]) for g in gs]
    x_s = [_split(head_mm([x_s[g]], x_s[g])[0]) for g in gs]
    for _ in range(2):
        prod = [head_mm([x_s[g], _split(pd[g])], x_s[g]) for g in gs]
        pd = [pd[g] + prod[g][1] for g in gs]
        x_s = [_split(prod[g][0]) for g in gs]
    pd = [pd[g] + head_mm([_split(pd[g])], x_s[g])[0] for g in gs]
    pd_s = [_split(pd[g]) for g in gs]
    m = [head_mm([pd_s[g]], no_s[g])[0] for g in gs]
    m_s = [_split(m[g]) for g in gs]
    m2_s = [_split(head_mm([m_s[g]], m_s[g])[0]) for g in gs]
    q = [eye + m[g] for g in gs]
    q_s = [_split(q[g] + head_mm([_split(q[g])], m2_s[g])[0]) for g in gs]
    p_s = [_split(head_mm([q_s[g]], pd_s[g])[0]) for g in gs]

    s_t = [s_ref[g] for g in gs]
    zz = [_dot(jnp.concatenate([lhs[g], ak_rk[g]], axis=1),
               jnp.concatenate([bd(s_t[g].astype(BF16)), bd(v[g])], axis=0)) for g in gs]
    sa = [head_mm([p_s[g]], _split(zz[g][:C]))[0].astype(BF16) for g in gs]
    y = [zz[g][C:] + _dot(b_rb[g], bd(sa[g])) for g in gs]

    yc = [y[g] - seg_sum(y[g]) * (1.0 / RW_HEAD) for g in gs]
    var = [seg_sum(yc[g] * yc[g]) * (1.0 / RW_HEAD) for g in gs]
    kh = [cols(k_ref, g) * (1.0 + (cols(a_ref, g) - 1.0) * cols(ka_ref, g)) for g in gs]
    bonus = [seg_sum(cols(r_ref, g) * kh[g] * cols(rk_ref, g)) * cols(v_ref, g) for g in gs]
    for g in gs:
        yn = yc[g] * lax.rsqrt(var[g] + RW_LNX_EPS) * cols(lg_ref, g) + cols(lb_ref, g)
        o_ref[:, g * W:(g + 1) * W] = ((yn + bonus[g]) * cols(gate_ref, g)).astype(BF16)

    full = [_dot_tn(jnp.concatenate([b_hat[g], k_hat[g]], axis=0), jnp.concatenate([sa[g], v[g]], axis=0))
            for g in gs]
    c_col = [seg_sum(eye * c_last[g]) for g in gs]
    for g in gs:
        gain = s_t[g]
        for h in range(HEADS_PER_GROUP):
            gain = gain + jnp.where(lane_head == h, full[g][h * C:(h + 1) * C, :], 0.0)
        s_ref[g] = jnp.exp(c_col[g]) * gain


def _rwkv_recurrence(r, k, v, w_raw, a, gate, k_k, k_a, r_k, lnx_g, lnx_b, *, batch, groups_per_step):
    t, d = r.shape
    chunks = t // (batch * CHUNK)
    gb = min(groups_per_step, d // GROUP)
    wb = gb * GROUP
    seq_spec = pl.BlockSpec((CHUNK, wb), lambda b, g, c: (b * chunks + c, g))
    par_spec = pl.BlockSpec((1, wb), lambda b, g, c: (0, g))
    par = lambda p: p.reshape(1, d)
    return pl.pallas_call(
        functools.partial(_rwkv_kernel, groups=gb),
        grid=(batch, d // wb, chunks),
        in_specs=[seq_spec] * 6 + [par_spec] * 5,
        out_specs=seq_spec,
        out_shape=jax.ShapeDtypeStruct((t, d), BF16),
        scratch_shapes=[pltpu.VMEM((gb, CHUNK, GROUP), F32)],
        compiler_params=_cparams("parallel", "parallel", "arbitrary"), name="rwkv_recurrence",
    )(r, k, v, w_raw, a, gate, par(k_k), par(k_a), par(r_k), par(lnx_g), par(lnx_b))


def _log_gate(x):
    return (jnp.minimum(x, 0.0) - jnp.log(1.0 + jnp.exp(-jnp.abs(x)))) * (1.0 / GLA_GATE_TAU)


def _post_norm_kernel(h_ref, m_ref, gp_ref, *rest):
    h = h_ref[...] + _rms(m_ref[...].astype(F32)) * gp_ref[...]
    if len(rest) == 1:
        rest[0][...] = h
        return
    gn_ref, *gate_refs, h_out, p_out = rest if len(rest) == 3 else rest[:-1]
    pre = (_rms(h) * gn_ref[...]).astype(BF16)
    h_out[...] = h
    p_out[...] = pre
    if gate_refs:
        a1_ref, a2_ref, ab_ref = gate_refs
        rest[-1][...] = _log_gate(_dot(_dot(pre, a1_ref[...]).astype(BF16), a2_ref[...]) + ab_ref[...])


def _post_norm_final(h, m, g_post, *, batch, lp, start, seq, tm):
    d = h.shape[1]
    align = BF16_SUBLANES
    assert lp % align == 0 and start % align == 0 and tm % align == 0
    in_spec = pl.BlockSpec((pl.Element(tm), pl.Element(d)),
                           lambda b, i: (pl.multiple_of(b * lp + start + i * tm, align), 0))
    return pl.pallas_call(
        _post_norm_kernel, grid=(batch, seq // tm),
        in_specs=[in_spec, in_spec, pl.BlockSpec((1, d), lambda b, i: (0, 0))],
        out_specs=pl.BlockSpec((None, tm, d), lambda b, i: (b, i, 0)),
        out_shape=jax.ShapeDtypeStruct((batch, seq, d), F32),
        compiler_params=_cparams("parallel", "parallel"), name="post_norm_final",
    )(h, m, g_post.reshape(1, d))


def _post_norm(h, m, g_post, g_next, *, tm, gate=None):
    t, d = h.shape
    row_spec = pl.BlockSpec((tm, d), lambda i: (i, 0))
    par_spec = pl.BlockSpec((1, d), lambda i: (0, 0))
    in_specs = [row_spec, row_spec, par_spec, par_spec]
    out_specs = [row_spec, row_spec]
    out_shape = [jax.ShapeDtypeStruct((t, d), F32), jax.ShapeDtypeStruct((t, d), BF16)]
    args = [h, m, g_post.reshape(1, d), g_next.reshape(1, d)]
    if gate is not None:
        a1, a2 = _pad_rank(gate[0], gate[1])
        dk = a2.shape[1]
        args += [a1, a2, gate[2].reshape(1, dk)]
        in_specs += [pl.BlockSpec(a.shape, lambda i: (0, 0)) for a in args[4:]]
        out_specs.append(pl.BlockSpec((tm, dk), lambda i: (i, 0)))
        out_shape.append(jax.ShapeDtypeStruct((t, dk), F32))
    return pl.pallas_call(
        _post_norm_kernel, grid=(t // tm,),
        in_specs=in_specs, out_specs=out_specs, out_shape=out_shape,
        compiler_params=_cparams("parallel"), name="post_norm",
    )(*args)


def _ffn_in_kernel(x_ref, halo_ref, wu_ref, wg_ref, cw_ref, cb_ref, wd_ref, o_ref, wd_out, *,
                   blocks_per_seq, halo_rows, subs):
    i = pl.program_id(0)
    wd_out[...] = wd_ref[...].astype(BF16)
    wu = wu_ref[...].astype(BF16)
    wg = wg_ref[...].astype(BF16)
    zh = _dot(halo_ref[...], wg)
    zh = jnp.where(i % blocks_per_seq == 0, 0.0, zh)
    prev1, prev2 = zh[halo_rows - 1:halo_rows, :], zh[halo_rows - 2:halo_rows - 1, :]
    starts = [sum(subs[:s]) for s in range(len(subs))]

    def project(s):
        xs = x_ref[starts[s]:starts[s] + subs[s], :]
        return _dot(xs, wu), _dot(xs, wg)

    cur = project(0)
    for s, sub in enumerate(subs):
        nxt = project(s + 1) if s + 1 < len(subs) else None
        u, z = cur
        row = lax.broadcasted_iota(jnp.int32, z.shape, 0)
        z1 = jnp.where(row == 0, prev1, pltpu.roll(z, 1, 0))
        z2 = pltpu.roll(z, 2, 0)
        z2 = jnp.where(row == 1, prev1, z2)
        z2 = jnp.where(row == 0, prev2, z2)
        zc = z * cw_ref[2:3, :] + z1 * cw_ref[1:2, :] + z2 * cw_ref[0:1, :] + cb_ref[...]
        o_ref[starts[s]:starts[s] + sub, :] = (zc * _sigmoid(zc) * u).astype(BF16)
        prev1, prev2 = z[sub - 1:sub, :], z[sub - 2:sub - 1, :]
        cur = nxt


def _sub_blocks(tm):
    sub = _pick(tm, (352, 96, 64))
    return (sub,) * (tm // sub)


def _ffn_in(x, w_up, w_gate, w_down, layer, conv_w, conv_b, *, lp, tm, tn):
    t, d = x.shape
    f = w_up.shape[2]
    hr = BF16_SUBLANES
    ni, nj = t // tm, f // tn
    slab = f // (ni * nj)
    assert slab * ni * nj == f and slab % hr == 0
    w_spec = pl.BlockSpec((None, d, tn), lambda i, j: (layer, 0, j))
    return pl.pallas_call(
        functools.partial(_ffn_in_kernel, blocks_per_seq=lp // tm, halo_rows=hr,
                          subs=_sub_blocks(tm)),
        grid=(ni, nj),
        in_specs=[pl.BlockSpec((tm, d), lambda i, j: (i, 0), pipeline_mode=pl.Buffered(1)),
                  pl.BlockSpec((hr, d), lambda i, j: (jnp.maximum(i * (tm // hr) - 1, 0), 0)),
                  w_spec, w_spec,
                  pl.BlockSpec((conv_w.shape[0], tn), lambda i, j: (0, j)),
                  pl.BlockSpec((1, tn), lambda i, j: (0, j)),
                  pl.BlockSpec((None, slab, d), lambda i, j: (layer, i * nj + j, 0))],
        out_specs=[pl.BlockSpec((tm, tn), lambda i, j: (i, j)),
                   pl.BlockSpec((slab, d), lambda i, j: (i * nj + j, 0))],
        out_shape=[jax.ShapeDtypeStruct((t, f), BF16), jax.ShapeDtypeStruct((f, d), BF16)],
        compiler_params=_cparams("parallel", "arbitrary"), name="ffn_in",
    )(x, x, w_up, w_gate, conv_w, conv_b.reshape(1, f), w_down)


GLA_LEVELS = 6


def _gla_select_matrices():
    t_i = np.arange(CHUNK)[:, None]
    u_i = np.arange(CHUNK)[None, :]
    mats = [u_i <= t_i]
    for l in range(GLA_LEVELS):
        half = 1 << l
        p = (t_i & ~(2 * half - 1)) + (half - 1)
        mats.append(np.where((t_i & half) != 0, (u_i > p) & (u_i <= t_i), (u_i > t_i) & (u_i <= p)))
    sel = np.concatenate(mats, axis=0).astype(np.float32)
    return jnp.asarray(np.concatenate([sel, sel], axis=1), dtype=BF16)


def _gla_kernel(sel_ref, q_ref, k_ref, v_ref, g_ref, r_ref, rb_ref, ng_ref, o_ref, s_ref, *, heads, hk, hv):
    c_id = pl.program_id(2)

    @pl.when(c_id == 0)
    def _():
        s_ref[...] = jnp.zeros_like(s_ref)

    C = CHUNK
    t_i = lax.broadcasted_iota(jnp.int32, (C, C), 0)
    u_i = lax.broadcasted_iota(jnp.int32, (C, C), 1)
    row = lax.broadcasted_iota(jnp.int32, (C, 1), 0)
    eye = t_i == u_i
    hs = range(heads)
    sel = sel_ref[...]

    q = [q_ref[:, h * hk:(h + 1) * hk] * (hk ** -0.5) for h in hs]
    k = [k_ref[:, h * hk:(h + 1) * hk] for h in hs]
    v = [v_ref[:, h * hv:(h + 1) * hv].astype(BF16) for h in hs]
    g_s = [_split(g_ref[:, h * hk:(h + 1) * hk]) for h in hs]
    cums = [_dot(sel, jnp.concatenate(g_s[h], axis=0)) for h in hs]
    b = [cums[h][:C] for h in hs]
    b_last = [b[h][C - 1:C, :] for h in hs]

    a = [jnp.where(eye, jnp.sum(q[h] * k[h], axis=-1, keepdims=True), 0.0) for h in hs]
    for l in range(GLA_LEVELS):
        upper = (row & (1 << l)) != 0
        same_block = (t_i >> (l + 1)) == (u_i >> (l + 1))
        qk = [jnp.where(upper, q[h], k[h]) * jnp.exp(cums[h][(1 + l) * C:(2 + l) * C]) for h in hs]
        ql = [jnp.where(upper, qk[h], 0.0).astype(BF16) for h in hs]
        kl = [jnp.where(upper, 0.0, qk[h]).astype(BF16) for h in hs]
        a = [a[h] + jnp.where(same_block, _dot_nt(ql[h], kl[h]), 0.0) for h in hs]

    s_t = [s_ref[h] for h in hs]
    qe = [(q[h] * jnp.exp(b[h])).astype(BF16) for h in hs]
    kr = [(k[h] * jnp.exp(b_last[h] - b[h])).astype(BF16) for h in hs]
    o = [_dot_nt(qe[h], s_t[h].astype(BF16)) + _dot(a[h].astype(BF16), v[h]) for h in hs]
    for h in hs:
        s_ref[h] = s_t[h] * jnp.exp(b_last[h]) + _dot_tn(v[h], kr[h])
    for h in hs:
        zg = r_ref[:, h * hv:(h + 1) * hv] + rb_ref[:, h * hv:(h + 1) * hv]
        o_ref[:, h * hv:(h + 1) * hv] = (_rms(o[h]) * ng_ref[...] * (zg * _sigmoid(zg))).astype(BF16)


def _gla_recurrence(proj, g, r_b, norm_g, *, batch, heads, hk, hv, heads_per_step):
    t = proj.shape[0]
    hb = heads_per_step
    chunks = t // (batch * CHUNK)
    dk, dv = heads * hk, heads * hv
    sel = _gla_select_matrices()

    def cols(width, start):
        assert start % (hb * width) == 0
        off = start // (hb * width)
        return pl.BlockSpec((CHUNK, hb * width), lambda b, h, c: (b * chunks + c, h + off))

    return pl.pallas_call(
        functools.partial(_gla_kernel, heads=hb, hk=hk, hv=hv),
        grid=(batch, heads // hb, chunks),
        in_specs=[pl.BlockSpec(sel.shape, lambda b, h, c: (0, 0)),
                  cols(hk, 0), cols(hk, dk), cols(hv, 2 * dk),
                  pl.BlockSpec((CHUNK, hb * hk), lambda b, h, c: (b * chunks + c, h)),
                  cols(hv, 2 * dk + dv),
                  pl.BlockSpec((1, hb * hv), lambda b, h, c: (0, h)),
                  pl.BlockSpec((1, hv), lambda b, h, c: (0, 0))],
        out_specs=pl.BlockSpec((CHUNK, hb * hv), lambda b, h, c: (b * chunks + c, h)),
        out_shape=jax.ShapeDtypeStruct((t, dv), BF16),
        scratch_shapes=[pltpu.VMEM((hb, hv, hk), F32)],
        compiler_params=_cparams("parallel", "parallel", "arbitrary"), name="gla_recurrence",
    )(sel, proj, proj, proj, g, proj, r_b.reshape(1, dv), norm_g.reshape(1, hv))


def _tiles(lp):
    return dict(
        ew=_pick(lp, (192, 128, 64)),
        norm=_pick(lp, (352, 192, 128, 64)),
        mm=_pick(lp, (1056, 704, 352, 192, 128, 64)),
        down=_pick(lp, (704, 1056, 352, 192, 128, 64)),
        ffn=_pick(lp, (2112, 1056, 704, 352, 192, 128, 64)),
        rw_groups=16,
        gla_heads=4,
    )


def _col_tile(n, cands=(512, 256, 128)):
    return _pick(n, cands)


def _rwkv_layer(h, g_pre, p, *, batch, lp, tiles):
    d = h.shape[1]
    xr, xk, xv, w_raw, a, gate = _rw_input(h, g_pre, p, lp=lp, tm=tiles["ew"])
    tn = _col_tile(d)
    proj = lambda x, w, name: _matmul(x, w, tm=tiles["mm"], tn=tn, name=name)
    r = proj(xr, p["wr"], "rw_r")
    k = proj(xk, p["wk"], "rw_k")
    v = proj(xv, p["wv"], "rw_v")
    yo = _rwkv_recurrence(r, k, v, w_raw, a, gate, p["k_k"], p["k_a"], p["r_k"].reshape(-1),
                          p["lnx_g"], p["lnx_b"], batch=batch, groups_per_step=tiles["rw_groups"])
    return _matmul(yo, p["wo"], tm=tiles["mm"], tn=tn, out_dtype=BF16, name="rw_o")


def _gla_layer(pre, g, p, *, batch, tiles):
    d = pre.shape[1]
    dk = p["a2"].shape[1]
    dv = p["wo"].shape[0]
    heads = max(4, d // 512)
    hk, hv = dk // heads, dv // heads
    proj = _matmul(pre, p["w_in"], tm=tiles["mm"], tn=_col_tile(dk), name="gla_in")
    og = _gla_recurrence(proj, g, p["r_b"], p["norm_g"], batch=batch, heads=heads, hk=hk, hv=hv,
                         heads_per_step=min(tiles["gla_heads"], heads))
    return _matmul(og, p["wo"], tm=tiles["mm"], tn=_col_tile(d), out_dtype=BF16, name="gla_o")


def _ffn(pre, w_up, w_gate, w_down, layer, conv_w, conv_b, *, lp, tiles):
    f = w_up.shape[2]
    act, w_down_bf = _ffn_in(pre, w_up, w_gate, w_down, layer, conv_w, conv_b,
                             lp=lp, tm=tiles["ffn"], tn=_col_tile(f, (256, 128)))
    return _matmul(act, w_down_bf, tm=tiles["down"], tn=_col_tile(w_down.shape[2], (256, 128)),
                   out_dtype=BF16, name="ffn_down")


def kernel(x, meta, norm_g, rw_mu, rw_w0, rw_w1, rw_w2, rw_a0, rw_a1, rw_a2, rw_g1, rw_g2, rw_k_k, rw_k_a, rw_r_k, rw_wr, rw_wk, rw_wv, rw_wo, rw_lnx_g, rw_lnx_b, gla_w_in, gla_a1, gla_a2, gla_a_b, gla_r_b, gla_norm_g, gla_wo, ffn_up, ffn_gate, ffn_conv, ffn_conv_b, ffn_down):
    batch, seq, d = x.shape
    n_meta = meta.shape[0]
    depth = norm_g.shape[0]
    length = n_meta + seq
    lp = -(-length // CHUNK) * CHUNK
    m = jnp.broadcast_to(meta.astype(x.dtype)[None], (batch, n_meta, d))
    h = jnp.concatenate([m, x, jnp.zeros((batch, lp - length, d), x.dtype)], axis=1).reshape(batch * lp, d)
    tiles = _tiles(lp)

    pre = gates = None
    for i in range(depth):
        j = i // 2
        if i % 2 == 0:
            p = dict(mu=rw_mu[j], w0=rw_w0[j], w1=rw_w1[j], w2=rw_w2[j], a0=rw_a0[j], a1=rw_a1[j],
                     a2=rw_a2[j], g1=rw_g1[j], g2=rw_g2[j], k_k=rw_k_k[j], k_a=rw_k_a[j], r_k=rw_r_k[j],
                     wr=rw_wr[j], wk=rw_wk[j], wv=rw_wv[j], wo=rw_wo[j], lnx_g=rw_lnx_g[j],
                     lnx_b=rw_lnx_b[j])
            mix = _rwkv_layer(h, norm_g[i, 0], p, batch=batch, lp=lp, tiles=tiles)
        else:
            p = dict(w_in=gla_w_in[j], a1=gla_a1[j], a2=gla_a2[j], a_b=gla_a_b[j], r_b=gla_r_b[j],
                     norm_g=gla_norm_g[j], wo=gla_wo[j])
            mix = _gla_layer(pre, gates, p, batch=batch, tiles=tiles)
        h, pre = _post_norm(h, mix, norm_g[i, 1], norm_g[i, 2], tm=tiles["norm"])
        f = _ffn(pre, ffn_up, ffn_gate, ffn_down, i, ffn_conv[i], ffn_conv_b[i], lp=lp, tiles=tiles)
        if i + 1 < depth and (i + 1) % 2 == 1:
            jn = (i + 1) // 2
            h, pre, gates = _post_norm(h, f, norm_g[i, 3], norm_g[i + 1, 0], tm=tiles["norm"],
                                       gate=(gla_a1[jn], gla_a2[jn], gla_a_b[jn]))
        elif i + 1 < depth:
            h, pre = _post_norm(h, f, norm_g[i, 3], norm_g[i + 1, 0], tm=tiles["norm"])
    return _post_norm_final(h, f, norm_g[depth - 1, 3], batch=batch, lp=lp, start=n_meta, seq=seq,
                            tm=_pick(seq, (256, 128, 64)))
```

```python
import functools

import numpy as np
import jax
import jax.numpy as jnp
from jax import lax
from jax.experimental import pallas as pl
from jax.experimental.pallas import tpu as pltpu

F32 = jnp.float32
BF16 = jnp.bfloat16

NORM_EPS = 1e-6
RW_LNX_EPS = 64e-5
RW_HEAD = 64
RW_HEAD_SHIFT = RW_HEAD.bit_length() - 1
INV_BLOCK_SHIFT = 4
DECAY_SCALE = float(np.exp(-0.5))
GLA_GATE_TAU = 16.0
CHUNK = 64
LANES = 128
BF16_SUBLANES = 16
GROUP = LANES
HEADS_PER_GROUP = GROUP // RW_HEAD
VMEM_LIMIT = 56 * 1024 * 1024


def _cparams(*sem):
    return pltpu.CompilerParams(dimension_semantics=sem, vmem_limit_bytes=VMEM_LIMIT)


def _pick(n, cands):
    for c in cands:
        if n % c == 0:
            return c
    raise ValueError(f"no tile for {n} in {cands}")


def _rms(x):
    return x * lax.rsqrt(jnp.mean(x * x, axis=-1, keepdims=True) + NORM_EPS)


def _sigmoid(z):
    return 1.0 / (1.0 + jnp.exp(-z))


def _dot(a, b):
    return jnp.dot(a, b, preferred_element_type=F32)


def _dot_nt(a, b):
    return lax.dot_general(a, b, (((1,), (1,)), ((), ())), preferred_element_type=F32)


def _dot_tn(a, b):
    return lax.dot_general(a, b, (((0,), (0,)), ((), ())), preferred_element_type=F32)


def _split(x):
    hi = x.astype(BF16)
    lo = (x - hi.astype(F32)).astype(BF16)
    return hi, lo


def _tail_rows(lp, length, tm):
    rows = length - (lp // tm - 1) * tm
    assert 0 < rows <= tm and rows % BF16_SUBLANES == 0
    return rows


def _on_row_tiles(body, tm, blocks_per_seq, tail_rows):
    if tail_rows == tm:
        body(tm)
    elif blocks_per_seq == 1:
        body(tail_rows)
    else:
        is_tail = pl.program_id(0) % blocks_per_seq == blocks_per_seq - 1
        pl.when(is_tail)(lambda: body(tail_rows))
        pl.when(jnp.logical_not(is_tail))(lambda: body(tm))


def _mm_kernel(x_ref, w_ref, o_ref, *, blocks_per_seq, tail_rows):
    w = w_ref[...].astype(BF16)
    tm = x_ref.shape[0]

    def body(rows):
        o_ref[:rows, :] = _dot(x_ref[:rows, :], w).astype(o_ref.dtype)
        if rows < tm:
            o_ref[rows:, :] = jnp.zeros((tm - rows, o_ref.shape[1]), o_ref.dtype)

    _on_row_tiles(body, tm, blocks_per_seq, tail_rows)


def _matmul(x, w, *, tm, tn, lp, length, out_dtype=F32, name):
    t, k = x.shape
    n = w.shape[1]
    return pl.pallas_call(
        functools.partial(_mm_kernel, blocks_per_seq=lp // tm, tail_rows=_tail_rows(lp, length, tm)),
        grid=(t // tm, n // tn),
        in_specs=[pl.BlockSpec((tm, k), lambda i, j: (i, 0)),
                  pl.BlockSpec((k, tn), lambda i, j: (0, j))],
        out_specs=pl.BlockSpec((tm, tn), lambda i, j: (i, j)),
        out_shape=jax.ShapeDtypeStruct((t, n), out_dtype),
        compiler_params=_cparams("parallel", "arbitrary"), name=name,
    )(x, w)


def _pad_rank(w1, w2):
    r = w1.shape[1]
    rp = -(-r // LANES) * LANES
    return (jnp.pad(w1, ((0, 0), (0, rp - r))).astype(BF16),
            jnp.pad(w2, ((0, rp - r), (0, 0))).astype(BF16))


def _rw_input_kernel(h_ref, halo_ref, g_ref, mu_ref, w1_ref, w2_ref, a1_ref, a2_ref, g1_ref, g2_ref,
                     w0_ref, a0_ref, xr_ref, xk_ref, xv_ref, wraw_ref, a_ref, gate_ref, *, blocks_per_seq):
    i = pl.program_id(0)
    g = g_ref[...]
    pre = _rms(h_ref[...]) * g
    halo = (_rms(halo_ref[...]) * g)[7:8, :]
    halo = jnp.where(i % blocks_per_seq == 0, 0.0, halo)
    row = lax.broadcasted_iota(jnp.int32, pre.shape, 0)
    prev = jnp.where(row == 0, halo, pltpu.roll(pre, 1, 0))
    xx = prev - pre
    mix = lambda m: (pre + xx * mu_ref[m:m + 1, :]).astype(BF16)
    xr_ref[...] = mix(0)
    xk_ref[...] = mix(2)
    xv_ref[...] = mix(3)
    wraw_ref[...] = w0_ref[...] + _dot(jnp.tanh(_dot(mix(1), w1_ref[...])).astype(BF16), w2_ref[...])
    a_ref[...] = _sigmoid(a0_ref[...] + _dot(_dot(mix(4), a1_ref[...]).astype(BF16), a2_ref[...]))
    gate_ref[...] = _dot(_sigmoid(_dot(mix(5), g1_ref[...])).astype(BF16), g2_ref[...])


def _rw_input(h, g, p, *, lp, tm):
    t, d = h.shape
    row_spec = pl.BlockSpec((tm, d), lambda i: (i, 0))
    const = lambda arr: pl.BlockSpec(arr.shape, lambda i: (0, 0), pipeline_mode=pl.Buffered(1))
    lora_w = [w for pair in (("w1", "w2"), ("a1", "a2"), ("g1", "g2")) for w in _pad_rank(p[pair[0]], p[pair[1]])]
    vec = lambda a: a.reshape(1, d)
    consts = [vec(g), p["mu"], *lora_w, vec(p["w0"]), vec(p["a0"])]
    return pl.pallas_call(
        functools.partial(_rw_input_kernel, blocks_per_seq=lp // tm),
        grid=(t // tm,),
        in_specs=[row_spec, pl.BlockSpec((8, d), lambda i: (jnp.maximum(i * (tm // 8) - 1, 0), 0))]
                 + [const(c) for c in consts],
        out_specs=[row_spec] * 6,
        out_shape=[jax.ShapeDtypeStruct((t, d), BF16)] * 3 + [jax.ShapeDtypeStruct((t, d), F32)] * 3,
        compiler_params=_cparams("parallel"), name="rw_input",
    )(h, h, *consts)


def _rwkv_kernel(r_ref, k_ref, v_ref, w_ref, a_ref, gate_ref, kk_ref, ka_ref, rk_ref, lg_ref, lb_ref,
                 o_ref, s_ref, *, groups):
    c_id = pl.program_id(2)

    @pl.when(c_id == 0)
    def _():
        s_ref[...] = jnp.zeros_like(s_ref)

    C, W = CHUNK, GROUP
    row = lax.broadcasted_iota(jnp.int32, (C, W), 0)
    lane = lax.broadcasted_iota(jnp.int32, (C, W), 1)
    lane_in = lane & (RW_HEAD - 1)
    lane_head = lane >> RW_HEAD_SHIFT
    tril_incl = lane_in <= row
    tril_strict = lane_in < row
    eye = jnp.where(lane_in == row, 1.0, 0.0)
    diag_block = (lane_in >> INV_BLOCK_SHIFT) == (row >> INV_BLOCK_SHIFT)
    bd_mask = (lax.broadcasted_iota(jnp.int32, (HEADS_PER_GROUP * C, W), 0) >> RW_HEAD_SHIFT) == \
              (lax.broadcasted_iota(jnp.int32, (HEADS_PER_GROUP * C, W), 1) >> RW_HEAD_SHIFT)
    ones_bd = jnp.where(bd_mask, 1.0, 0.0).astype(BF16)
    ones_bd2 = jnp.concatenate([ones_bd, ones_bd], axis=0)
    ltri = jnp.where(lax.broadcasted_iota(jnp.int32, (C, C), 1) <=
                     lax.broadcasted_iota(jnp.int32, (C, C), 0), 1.0, 0.0).astype(BF16)
    gs = range(groups)
    cols = lambda ref, g: ref[:, g * W:(g + 1) * W]

    def bd(x):
        return jnp.concatenate([x] * HEADS_PER_GROUP, axis=0) * ones_bd

    def seg_sum(x):
        return _dot(jnp.concatenate(_split(x), axis=1), ones_bd2)

    def head_mm(lhs, ys):
        top = jnp.concatenate([bd(ys[0]), bd(ys[1])], axis=1)
        res = _dot(jnp.concatenate([jnp.concatenate(x, axis=1) for x in lhs], axis=0),
                   jnp.concatenate([top, top], axis=0))
        return [res[i * C:(i + 1) * C, :W] + res[i * C:(i + 1) * C, W:] for i in range(len(lhs))]

    k = [cols(k_ref, g) for g in gs]
    v = [cols(v_ref, g).astype(BF16) for g in gs]
    a = [cols(a_ref, g) for g in gs]
    ld = [-DECAY_SCALE * _sigmoid(cols(w_ref, g)) for g in gs]
    kk = [k[g] * cols(kk_ref, g) for g in gs]
    kk = [kk[g] * lax.rsqrt(jnp.maximum(seg_sum(kk[g] * kk[g]), 1e-24)) for g in gs]
    kh = [k[g] * (1.0 + (a[g] - 1.0) * cols(ka_ref, g)) for g in gs]
    rb = [kk[g] * a[g] for g in gs]

    cum2 = [_dot(ltri, jnp.concatenate(_split(ld[g]), axis=1)) for g in gs]
    cum = [cum2[g][:, :W] + cum2[g][:, W:] for g in gs]
    c_last = [cum[g][C - 1:C, :] for g in gs]
    e_nc = [jnp.exp(-cum[g]) for g in gs]
    lhs = [jnp.concatenate([-kk[g] * jnp.exp(cum[g] - ld[g]), cols(r_ref, g) * jnp.exp(cum[g])],
                           axis=0).astype(BF16) for g in gs]
    b_hat = [(rb[g] * e_nc[g]).astype(BF16) for g in gs]
    k_hat = [(kh[g] * e_nc[g]).astype(BF16) for g in gs]
    gram = [_dot_nt(lhs[g], jnp.concatenate([bd(b_hat[g]), bd(k_hat[g])], axis=0)) for g in gs]
    n_ab = [jnp.where(tril_strict, gram[g][:C, :W], 0.0) for g in gs]
    b_rb = [jnp.where(tril_incl, gram[g][C:, :W], 0.0).astype(BF16) for g in gs]
    ak_rk = [jnp.concatenate([jnp.where(tril_strict, gram[g][:C, W:], 0.0),
                              jnp.where(tril_incl, gram[g][C:, W:], 0.0)], axis=0).astype(BF16) for g in gs]

    assert CHUNK == 4 << INV_BLOCK_SHIFT and INV_BLOCK_SHIFT == 4
    nd = [jnp.where(diag_block, n_ab[g], 0.0) for g in gs]
    no_s = [_split(n_ab[g] - nd[g]) for g in gs]
    pd = [eye + nd[g] for g in gs]
    x_s = [_split(nd[g]) for g in gs]
    x_s = [_split(head_mm([x_s[g]], x_s[g])[0]) for g in gs]
    for _ in range(2):
        prod = [head_mm([x_s[g], _split(pd[g])], x_s[g]) for g in gs]
        pd = [pd[g] + prod[g][1] for g in gs]
        x_s = [_split(prod[g][0]) for g in gs]
    pd = [pd[g] + head_mm([_split(pd[g])], x_s[g])[0] for g in gs]
    pd_s = [_split(pd[g]) for g in gs]
    m = [head_mm([pd_s[g]], no_s[g])[0] for g in gs]
    m_s = [_split(m[g]) for g in gs]
    m2_s = [_split(head_mm([m_s[g]], m_s[g])[0]) for g in gs]
    q = [eye + m[g] for g in gs]
    q_s = [_split(q[g] + head_mm([_split(q[g])], m2_s[g])[0]) for g in gs]
    p_s = [_split(head_mm([q_s[g]], pd_s[g])[0]) for g in gs]

    s_t = [s_ref[g] for g in gs]
    zz = [_dot(jnp.concatenate([lhs[g], ak_rk[g]], axis=1),
               jnp.concatenate([bd(s_t[g].astype(BF16)), bd(v[g])], axis=0)) for g in gs]
    sa = [head_mm([p_s[g]], _split(zz[g][:C]))[0].astype(BF16) for g in gs]
    y = [zz[g][C:] + _dot(b_rb[g], bd(sa[g])) for g in gs]

    yc = [y[g] - seg_sum(y[g]) * (1.0 / RW_HEAD) for g in gs]
    var = [seg_sum(yc[g] * yc[g]) * (1.0 / RW_HEAD) for g in gs]
    kh = [cols(k_ref, g) * (1.0 + (cols(a_ref, g) - 1.0) * cols(ka_ref, g)) for g in gs]
    bonus = [seg_sum(cols(r_ref, g) * kh[g] * cols(rk_ref, g)) * cols(v_ref, g) for g in gs]
    for g in gs:
        yn = yc[g] * lax.rsqrt(var[g] + RW_LNX_EPS) * cols(lg_ref, g) + cols(lb_ref, g)
        o_ref[:, g * W:(g + 1) * W] = ((yn + bonus[g]) * cols(gate_ref, g)).astype(BF16)

    full = [_dot_tn(jnp.concatenate([b_hat[g], k_hat[g]], axis=0), jnp.concatenate([sa[g], v[g]], axis=0))
            for g in gs]
    c_col = [seg_sum(eye * c_last[g]) for g in gs]
    for g in gs:
        gain = s_t[g]
        for h in range(HEADS_PER_GROUP):
            gain = gain + jnp.where(lane_head == h, full[g][h * C:(h + 1) * C, :], 0.0)
        s_ref[g] = jnp.exp(c_col[g]) * gain


def _rwkv_recurrence(r, k, v, w_raw, a, gate, k_k, k_a, r_k, lnx_g, lnx_b, *, batch, groups_per_step):
    t, d = r.shape
    chunks = t // (batch * CHUNK)
    gb = min(groups_per_step, d // GROUP)
    wb = gb * GROUP
    seq_spec = pl.BlockSpec((CHUNK, wb), lambda b, g, c: (b * chunks + c, g))
    par_spec = pl.BlockSpec((1, wb), lambda b, g, c: (0, g))
    par = lambda p: p.reshape(1, d)
    return pl.pallas_call(
        functools.partial(_rwkv_kernel, groups=gb),
        grid=(batch, d // wb, chunks),
        in_specs=[seq_spec] * 6 + [par_spec] * 5,
        out_specs=seq_spec,
        out_shape=jax.ShapeDtypeStruct((t, d), BF16),
        scratch_shapes=[pltpu.VMEM((gb, CHUNK, GROUP), F32)],
        compiler_params=_cparams("parallel", "parallel", "arbitrary"), name="rwkv_recurrence",
    )(r, k, v, w_raw, a, gate, par(k_k), par(k_a), par(r_k), par(lnx_g), par(lnx_b))


def _log_gate(x):
    return (jnp.minimum(x, 0.0) - jnp.log(1.0 + jnp.exp(-jnp.abs(x)))) * (1.0 / GLA_GATE_TAU)


def _post_norm_kernel(h_ref, m_ref, gp_ref, *rest):
    h = h_ref[...] + _rms(m_ref[...].astype(F32)) * gp_ref[...]
    if len(rest) == 1:
        rest[0][...] = h
        return
    gn_ref, *gate_refs, h_out, p_out = rest if len(rest) == 3 else rest[:-1]
    pre = (_rms(h) * gn_ref[...]).astype(BF16)
    h_out[...] = h
    p_out[...] = pre
    if gate_refs:
        a1_ref, a2_ref, ab_ref = gate_refs
        rest[-1][...] = _log_gate(_dot(_dot(pre, a1_ref[...]).astype(BF16), a2_ref[...]) + ab_ref[...])


def _post_norm_final(h, m, g_post, *, batch, lp, start, seq, tm):
    d = h.shape[1]
    align = BF16_SUBLANES
    assert lp % align == 0 and start % align == 0 and tm % align == 0
    in_spec = pl.BlockSpec((pl.Element(tm), pl.Element(d)),
                           lambda b, i: (pl.multiple_of(b * lp + start + i * tm, align), 0))
    return pl.pallas_call(
        _post_norm_kernel, grid=(batch, seq // tm),
        in_specs=[in_spec, in_spec, pl.BlockSpec((1, d), lambda b, i: (0, 0))],
        out_specs=pl.BlockSpec((None, tm, d), lambda b, i: (b, i, 0)),
        out_shape=jax.ShapeDtypeStruct((batch, seq, d), F32),
        compiler_params=_cparams("parallel", "parallel"), name="post_norm_final",
    )(h, m, g_post.reshape(1, d))


def _post_norm(h, m, g_post, g_next, *, tm, gate=None):
    t, d = h.shape
    row_spec = pl.BlockSpec((tm, d), lambda i: (i, 0))
    par_spec = pl.BlockSpec((1, d), lambda i: (0, 0))
    in_specs = [row_spec, row_spec, par_spec, par_spec]
    out_specs = [row_spec, row_spec]
    out_shape = [jax.ShapeDtypeStruct((t, d), F32), jax.ShapeDtypeStruct((t, d), BF16)]
    args = [h, m, g_post.reshape(1, d), g_next.reshape(1, d)]
    if gate is not None:
        a1, a2 = _pad_rank(gate[0], gate[1])
        dk = a2.shape[1]
        args += [a1, a2, gate[2].reshape(1, dk)]
        in_specs += [pl.BlockSpec(a.shape, lambda i: (0, 0)) for a in args[4:]]
        out_specs.append(pl.BlockSpec((tm, dk), lambda i: (i, 0)))
        out_shape.append(jax.ShapeDtypeStruct((t, dk), F32))
    return pl.pallas_call(
        _post_norm_kernel, grid=(t // tm,),
        in_specs=in_specs, out_specs=out_specs, out_shape=out_shape,
        compiler_params=_cparams("parallel"), name="post_norm",
    )(*args)


def _ffn_in_kernel(x_ref, halo_ref, wu_ref, wg_ref, cw_ref, cb_ref, wd_ref, o_ref, wd_out, *,
                   blocks_per_seq, tail_rows, halo_rows):
    i = pl.program_id(0)
    tm = x_ref.shape[0]
    wd_out[...] = wd_ref[...].astype(BF16)
    wu = wu_ref[...].astype(BF16)
    wg = wg_ref[...].astype(BF16)
    zh = _dot(halo_ref[...], wg)
    zh = jnp.where(i % blocks_per_seq == 0, 0.0, zh)

    def body(rows):
        sub = _pick(rows, (352, 688, 96, 48, 16))
        prev1, prev2 = zh[halo_rows - 1:halo_rows, :], zh[halo_rows - 2:halo_rows - 1, :]
        row = lax.broadcasted_iota(jnp.int32, (sub, wu.shape[1]), 0)

        def project(s):
            xs = x_ref[s * sub:(s + 1) * sub, :]
            return _dot(xs, wu), _dot(xs, wg)

        cur = project(0)
        for s in range(rows // sub):
            nxt = project(s + 1) if (s + 1) * sub < rows else None
            u, z = cur
            z1 = jnp.where(row == 0, prev1, pltpu.roll(z, 1, 0))
            z2 = pltpu.roll(z, 2, 0)
            z2 = jnp.where(row == 1, prev1, z2)
            z2 = jnp.where(row == 0, prev2, z2)
            zc = z * cw_ref[2:3, :] + z1 * cw_ref[1:2, :] + z2 * cw_ref[0:1, :] + cb_ref[...]
            o_ref[s * sub:(s + 1) * sub, :] = (zc * _sigmoid(zc) * u).astype(BF16)
            prev1, prev2 = z[sub - 1:sub, :], z[sub - 2:sub - 1, :]
            cur = nxt
        if rows < tm:
            o_ref[rows:, :] = jnp.zeros((tm - rows, o_ref.shape[1]), o_ref.dtype)

    _on_row_tiles(body, tm, blocks_per_seq, tail_rows)


def _ffn_in(x, w_up, w_gate, w_down, layer, conv_w, conv_b, *, lp, length, tm, tn):
    t, d = x.shape
    f = w_up.shape[2]
    hr = BF16_SUBLANES
    ni, nj = t // tm, f // tn
    slab = f // (ni * nj)
    assert slab * ni * nj == f and slab % hr == 0
    w_spec = pl.BlockSpec((None, d, tn), lambda i, j: (layer, 0, j))
    return pl.pallas_call(
        functools.partial(_ffn_in_kernel, blocks_per_seq=lp // tm, tail_rows=_tail_rows(lp, length, tm),
                          halo_rows=hr),
        grid=(ni, nj),
        in_specs=[pl.BlockSpec((tm, d), lambda i, j: (i, 0), pipeline_mode=pl.Buffered(1)),
                  pl.BlockSpec((hr, d), lambda i, j: (jnp.maximum(i * (tm // hr) - 1, 0), 0)),
                  w_spec, w_spec,
                  pl.BlockSpec((conv_w.shape[0], tn), lambda i, j: (0, j)),
                  pl.BlockSpec((1, tn), lambda i, j: (0, j)),
                  pl.BlockSpec((None, slab, d), lambda i, j: (layer, i * nj + j, 0))],
        out_specs=[pl.BlockSpec((tm, tn), lambda i, j: (i, j)),
                   pl.BlockSpec((slab, d), lambda i, j: (i * nj + j, 0))],
        out_shape=[jax.ShapeDtypeStruct((t, f), BF16), jax.ShapeDtypeStruct((f, d), BF16)],
        compiler_params=_cparams("parallel", "arbitrary"), name="ffn_in",
    )(x, x, w_up, w_gate, conv_w, conv_b.reshape(1, f), w_down)


GLA_LEVELS = 6


def _gla_select_matrices():
    t_i = np.arange(CHUNK)[:, None]
    u_i = np.arange(CHUNK)[None, :]
    mats = [u_i <= t_i]
    for l in range(GLA_LEVELS):
        half = 1 << l
        p = (t_i & ~(2 * half - 1)) + (half - 1)
        mats.append(np.where((t_i & half) != 0, (u_i > p) & (u_i <= t_i), (u_i > t_i) & (u_i <= p)))
    sel = np.concatenate(mats, axis=0).astype(np.float32)
    return jnp.asarray(np.concatenate([sel, sel], axis=1), dtype=BF16)


def _gla_kernel(sel_ref, q_ref, k_ref, v_ref, g_ref, r_ref, rb_ref, ng_ref, o_ref, s_ref, *, heads, hk, hv):
    c_id = pl.program_id(2)

    @pl.when(c_id == 0)
    def _():
        s_ref[...] = jnp.zeros_like(s_ref)

    C = CHUNK
    t_i = lax.broadcasted_iota(jnp.int32, (C, C), 0)
    u_i = lax.broadcasted_iota(jnp.int32, (C, C), 1)
    row = lax.broadcasted_iota(jnp.int32, (C, 1), 0)
    eye = t_i == u_i
    hs = range(heads)
    sel = sel_ref[...]

    q = [q_ref[:, h * hk:(h + 1) * hk] * (hk ** -0.5) for h in hs]
    k = [k_ref[:, h * hk:(h + 1) * hk] for h in hs]
    v = [v_ref[:, h * hv:(h + 1) * hv].astype(BF16) for h in hs]
    g_s = [_split(g_ref[:, h * hk:(h + 1) * hk]) for h in hs]
    cums = [_dot(sel, jnp.concatenate(g_s[h], axis=0)) for h in hs]
    b = [cums[h][:C] for h in hs]
    b_last = [b[h][C - 1:C, :] for h in hs]

    a = [jnp.where(eye, jnp.sum(q[h] * k[h], axis=-1, keepdims=True), 0.0) for h in hs]
    for l in range(GLA_LEVELS):
        upper = (row & (1 << l)) != 0
        same_block = (t_i >> (l + 1)) == (u_i >> (l + 1))
        qk = [jnp.where(upper, q[h], k[h]) * jnp.exp(cums[h][(1 + l) * C:(2 + l) * C]) for h in hs]
        ql = [jnp.where(upper, qk[h], 0.0).astype(BF16) for h in hs]
        kl = [jnp.where(upper, 0.0, qk[h]).astype(BF16) for h in hs]
        a = [a[h] + jnp.where(same_block, _dot_nt(ql[h], kl[h]), 0.0) for h in hs]

    s_t = [s_ref[h] for h in hs]
    qe = [(q[h] * jnp.exp(b[h])).astype(BF16) for h in hs]
    kr = [(k[h] * jnp.exp(b_last[h] - b[h])).astype(BF16) for h in hs]
    o = [_dot_nt(qe[h], s_t[h].astype(BF16)) + _dot(a[h].astype(BF16), v[h]) for h in hs]
    for h in hs:
        s_ref[h] = s_t[h] * jnp.exp(b_last[h]) + _dot_tn(v[h], kr[h])
    for h in hs:
        zg = r_ref[:, h * hv:(h + 1) * hv] + rb_ref[:, h * hv:(h + 1) * hv]
        o_ref[:, h * hv:(h + 1) * hv] = (_rms(o[h]) * ng_ref[...] * (zg * _sigmoid(zg))).astype(BF16)


def _gla_recurrence(proj, g, r_b, norm_g, *, batch, heads, hk, hv, heads_per_step):
    t = proj.shape[0]
    hb = heads_per_step
    chunks = t // (batch * CHUNK)
    dk, dv = heads * hk, heads * hv
    sel = _gla_select_matrices()

    def cols(width, start):
        assert start % (hb * width) == 0
        off = start // (hb * width)
        return pl.BlockSpec((CHUNK, hb * width), lambda b, h, c: (b * chunks + c, h + off))

    return pl.pallas_call(
        functools.partial(_gla_kernel, heads=hb, hk=hk, hv=hv),
        grid=(batch, heads // hb, chunks),
        in_specs=[pl.BlockSpec(sel.shape, lambda b, h, c: (0, 0)),
                  cols(hk, 0), cols(hk, dk), cols(hv, 2 * dk),
                  pl.BlockSpec((CHUNK, hb * hk), lambda b, h, c: (b * chunks + c, h)),
                  cols(hv, 2 * dk + dv),
                  pl.BlockSpec((1, hb * hv), lambda b, h, c: (0, h)),
                  pl.BlockSpec((1, hv), lambda b, h, c: (0, 0))],
        out_specs=pl.BlockSpec((CHUNK, hb * hv), lambda b, h, c: (b * chunks + c, h)),
        out_shape=jax.ShapeDtypeStruct((t, dv), BF16),
        scratch_shapes=[pltpu.VMEM((hb, hv, hk), F32)],
        compiler_params=_cparams("parallel", "parallel", "arbitrary"), name="gla_recurrence",
    )(sel, proj, proj, proj, g, proj, r_b.reshape(1, dv), norm_g.reshape(1, hv))


def _tiles(lp, length):
    return dict(
        lp=lp, length=length,
        ew=_pick(lp, (192, 128, 64)),
        norm=_pick(lp, (352, 192, 128, 64)),
        mm=_pick(lp, (1056, 704, 352, 192, 128, 64)),
        down=_pick(lp, (704, 1056, 352, 192, 128, 64)),
        ffn=_pick(lp, (2112, 1056, 704, 352, 192, 128, 64)),
        rw_groups=16,
        gla_heads=4,
    )


def _col_tile(n, cands=(512, 256, 128)):
    return _pick(n, cands)


def _seq_matmul(tiles, key):
    return functools.partial(_matmul, tm=tiles[key], lp=tiles["lp"], length=tiles["length"])


def _rwkv_layer(h, g_pre, p, *, batch, tiles):
    d = h.shape[1]
    xr, xk, xv, w_raw, a, gate = _rw_input(h, g_pre, p, lp=tiles["lp"], tm=tiles["ew"])
    mm = functools.partial(_seq_matmul(tiles, "mm"), tn=_col_tile(d))
    r = mm(xr, p["wr"], name="rw_r")
    k = mm(xk, p["wk"], name="rw_k")
    v = mm(xv, p["wv"], name="rw_v")
    yo = _rwkv_recurrence(r, k, v, w_raw, a, gate, p["k_k"], p["k_a"], p["r_k"].reshape(-1),
                          p["lnx_g"], p["lnx_b"], batch=batch, groups_per_step=tiles["rw_groups"])
    return mm(yo, p["wo"], out_dtype=BF16, name="rw_o")


def _gla_layer(pre, g, p, *, batch, tiles):
    d = pre.shape[1]
    dk = p["a2"].shape[1]
    dv = p["wo"].shape[0]
    heads = max(4, d // 512)
    hk, hv = dk // heads, dv // heads
    mm = _seq_matmul(tiles, "mm")
    proj = mm(pre, p["w_in"], tn=_col_tile(dk), name="gla_in")
    og = _gla_recurrence(proj, g, p["r_b"], p["norm_g"], batch=batch, heads=heads, hk=hk, hv=hv,
                         heads_per_step=min(tiles["gla_heads"], heads))
    return mm(og, p["wo"], tn=_col_tile(d), out_dtype=BF16, name="gla_o")


def _ffn(pre, w_up, w_gate, w_down, layer, conv_w, conv_b, *, tiles):
    f = w_up.shape[2]
    act, w_down_bf = _ffn_in(pre, w_up, w_gate, w_down, layer, conv_w, conv_b, lp=tiles["lp"],
                             length=tiles["length"], tm=tiles["ffn"], tn=_col_tile(f, (256, 128)))
    return _seq_matmul(tiles, "down")(act, w_down_bf, tn=_col_tile(w_down.shape[2], (256, 128)),
                                      out_dtype=BF16, name="ffn_down")


def kernel(x, meta, norm_g, rw_mu, rw_w0, rw_w1, rw_w2, rw_a0, rw_a1, rw_a2, rw_g1, rw_g2, rw_k_k, rw_k_a, rw_r_k, rw_wr, rw_wk, rw_wv, rw_wo, rw_lnx_g, rw_lnx_b, gla_w_in, gla_a1, gla_a2, gla_a_b, gla_r_b, gla_norm_g, gla_wo, ffn_up, ffn_gate, ffn_conv, ffn_conv_b, ffn_down):
    batch, seq, d = x.shape
    n_meta = meta.shape[0]
    depth = norm_g.shape[0]
    length = n_meta + seq
    lp = -(-length // CHUNK) * CHUNK
    m = jnp.broadcast_to(meta.astype(x.dtype)[None], (batch, n_meta, d))
    h = jnp.concatenate([m, x, jnp.zeros((batch, lp - length, d), x.dtype)], axis=1).reshape(batch * lp, d)
    tiles = _tiles(lp, length)

    pre = gates = None
    for i in range(depth):
        j = i // 2
        if i % 2 == 0:
            p = dict(mu=rw_mu[j], w0=rw_w0[j], w1=rw_w1[j], w2=rw_w2[j], a0=rw_a0[j], a1=rw_a1[j],
                     a2=rw_a2[j], g1=rw_g1[j], g2=rw_g2[j], k_k=rw_k_k[j], k_a=rw_k_a[j], r_k=rw_r_k[j],
                     wr=rw_wr[j], wk=rw_wk[j], wv=rw_wv[j], wo=rw_wo[j], lnx_g=rw_lnx_g[j],
                     lnx_b=rw_lnx_b[j])
            mix = _rwkv_layer(h, norm_g[i, 0], p, batch=batch, tiles=tiles)
        else:
            p = dict(w_in=gla_w_in[j], a1=gla_a1[j], a2=gla_a2[j], a_b=gla_a_b[j], r_b=gla_r_b[j],
                     norm_g=gla_norm_g[j], wo=gla_wo[j])
            mix = _gla_layer(pre, gates, p, batch=batch, tiles=tiles)
        h, pre = _post_norm(h, mix, norm_g[i, 1], norm_g[i, 2], tm=tiles["norm"])
        f = _ffn(pre, ffn_up, ffn_gate, ffn_down, i, ffn_conv[i], ffn_conv_b[i], tiles=tiles)
        if i + 1 < depth and (i + 1) % 2 == 1:
            jn = (i + 1) // 2
            h, pre, gates = _post_norm(h, f, norm_g[i, 3], norm_g[i + 1, 0], tm=tiles["norm"],
                                       gate=(gla_a1[jn], gla_a2[jn], gla_a_b[jn]))
        elif i + 1 < depth:
            h, pre = _post_norm(h, f, norm_g[i, 3], norm_g[i + 1, 0], tm=tiles["norm"])
    return _post_norm_final(h, f, norm_g[depth - 1, 3], batch=batch, lp=lp, start=n_meta, seq=seq,
                            tm=_pick(seq, (256, 128, 64)))
```

```python
import functools

import numpy as np
import jax
import jax.numpy as jnp
from jax import lax
from jax.experimental import pallas as pl
from jax.experimental.pallas import tpu as pltpu

F32 = jnp.float32
BF16 = jnp.bfloat16

NORM_EPS = 1e-6
RW_LNX_EPS = 64e-5
RW_HEAD = 64
RW_HEAD_SHIFT = RW_HEAD.bit_length() - 1
INV_BLOCK_SHIFT = 4
DECAY_SCALE = float(np.exp(-0.5))
GLA_GATE_TAU = 16.0
CHUNK = 64
LANES = 128
BF16_SUBLANES = 16
GROUP = LANES
HEADS_PER_GROUP = GROUP // RW_HEAD
VMEM_LIMIT = 56 * 1024 * 1024


def _cparams(*sem):
    return pltpu.CompilerParams(dimension_semantics=sem, vmem_limit_bytes=VMEM_LIMIT)


def _pick(n, cands):
    for c in cands:
        if n % c == 0:
            return c
    raise ValueError(f"no tile for {n} in {cands}")


def _rms(x):
    return x * lax.rsqrt(jnp.mean(x * x, axis=-1, keepdims=True) + NORM_EPS)


def _sigmoid(z):
    return 1.0 / (1.0 + jnp.exp(-z))


def _dot(a, b):
    return jnp.dot(a, b, preferred_element_type=F32)


def _dot_nt(a, b):
    return lax.dot_general(a, b, (((1,), (1,)), ((), ())), preferred_element_type=F32)


def _dot_tn(a, b):
    return lax.dot_general(a, b, (((0,), (0,)), ((), ())), preferred_element_type=F32)


def _split(x):
    hi = x.astype(BF16)
    lo = (x - hi.astype(F32)).astype(BF16)
    return hi, lo


def _mm_kernel(x_ref, w_ref, o_ref, *, rows):
    w = w_ref[...].astype(BF16)
    o_ref[:rows, :] = _dot(x_ref[:rows, :], w).astype(o_ref.dtype)
    if rows < o_ref.shape[0]:
        o_ref[rows:, :] = jnp.zeros((o_ref.shape[0] - rows, o_ref.shape[1]), o_ref.dtype)


def _matmul(x, w, *, tm, tn, rows=None, out_dtype=F32, name):
    t, k = x.shape
    n = w.shape[1]
    x_mode = {} if rows is None else dict(pipeline_mode=pl.Buffered(1))
    return pl.pallas_call(
        functools.partial(_mm_kernel, rows=tm if rows is None else rows), grid=(t // tm, n // tn),
        in_specs=[pl.BlockSpec((tm, k), lambda i, j: (i, 0), **x_mode),
                  pl.BlockSpec((k, tn), lambda i, j: (0, j))],
        out_specs=pl.BlockSpec((tm, tn), lambda i, j: (i, j)),
        out_shape=jax.ShapeDtypeStruct((t, n), out_dtype),
        compiler_params=_cparams("parallel", "arbitrary"), name=name,
    )(x, w)


def _pad_rank(w1, w2):
    r = w1.shape[1]
    rp = -(-r // LANES) * LANES
    return (jnp.pad(w1, ((0, 0), (0, rp - r))).astype(BF16),
            jnp.pad(w2, ((0, rp - r), (0, 0))).astype(BF16))


def _rw_input_kernel(h_ref, halo_ref, g_ref, mu_ref, w1_ref, w2_ref, a1_ref, a2_ref, g1_ref, g2_ref,
                     w0_ref, a0_ref, xr_ref, xk_ref, xv_ref, wraw_ref, a_ref, gate_ref, *, blocks_per_seq):
    i = pl.program_id(0)
    g = g_ref[...]
    pre = _rms(h_ref[...]) * g
    halo = (_rms(halo_ref[...]) * g)[7:8, :]
    halo = jnp.where(i % blocks_per_seq == 0, 0.0, halo)
    row = lax.broadcasted_iota(jnp.int32, pre.shape, 0)
    prev = jnp.where(row == 0, halo, pltpu.roll(pre, 1, 0))
    xx = prev - pre
    mix = lambda m: (pre + xx * mu_ref[m:m + 1, :]).astype(BF16)
    xr_ref[...] = mix(0)
    xk_ref[...] = mix(2)
    xv_ref[...] = mix(3)
    wraw_ref[...] = w0_ref[...] + _dot(jnp.tanh(_dot(mix(1), w1_ref[...])).astype(BF16), w2_ref[...])
    a_ref[...] = _sigmoid(a0_ref[...] + _dot(_dot(mix(4), a1_ref[...]).astype(BF16), a2_ref[...]))
    gate_ref[...] = _dot(_sigmoid(_dot(mix(5), g1_ref[...])).astype(BF16), g2_ref[...])


def _rw_input(h, g, p, *, lp, tm):
    t, d = h.shape
    row_spec = pl.BlockSpec((tm, d), lambda i: (i, 0))
    const = lambda arr: pl.BlockSpec(arr.shape, lambda i: (0, 0), pipeline_mode=pl.Buffered(1))
    lora_w = [w for pair in (("w1", "w2"), ("a1", "a2"), ("g1", "g2")) for w in _pad_rank(p[pair[0]], p[pair[1]])]
    vec = lambda a: a.reshape(1, d)
    consts = [vec(g), p["mu"], *lora_w, vec(p["w0"]), vec(p["a0"])]
    return pl.pallas_call(
        functools.partial(_rw_input_kernel, blocks_per_seq=lp // tm),
        grid=(t // tm,),
        in_specs=[row_spec, pl.BlockSpec((8, d), lambda i: (jnp.maximum(i * (tm // 8) - 1, 0), 0))]
                 + [const(c) for c in consts],
        out_specs=[row_spec] * 6,
        out_shape=[jax.ShapeDtypeStruct((t, d), BF16)] * 3 + [jax.ShapeDtypeStruct((t, d), F32)] * 3,
        compiler_params=_cparams("parallel"), name="rw_input",
    )(h, h, *consts)


def _rwkv_kernel(r_ref, k_ref, v_ref, w_ref, a_ref, gate_ref, kk_ref, ka_ref, rk_ref, lg_ref, lb_ref,
                 o_ref, s_ref, *, groups):
    c_id = pl.program_id(2)

    @pl.when(c_id == 0)
    def _():
        s_ref[...] = jnp.zeros_like(s_ref)

    C, W = CHUNK, GROUP
    row = lax.broadcasted_iota(jnp.int32, (C, W), 0)
    lane = lax.broadcasted_iota(jnp.int32, (C, W), 1)
    lane_in = lane & (RW_HEAD - 1)
    lane_head = lane >> RW_HEAD_SHIFT
    tril_incl = lane_in <= row
    tril_strict = lane_in < row
    eye = jnp.where(lane_in == row, 1.0, 0.0)
    diag_block = (lane_in >> INV_BLOCK_SHIFT) == (row >> INV_BLOCK_SHIFT)
    bd_mask = (lax.broadcasted_iota(jnp.int32, (HEADS_PER_GROUP * C, W), 0) >> RW_HEAD_SHIFT) == \
              (lax.broadcasted_iota(jnp.int32, (HEADS_PER_GROUP * C, W), 1) >> RW_HEAD_SHIFT)
    ones_bd = jnp.where(bd_mask, 1.0, 0.0).astype(BF16)
    ones_bd2 = jnp.concatenate([ones_bd, ones_bd], axis=0)
    ltri = jnp.where(lax.broadcasted_iota(jnp.int32, (C, C), 1) <=
                     lax.broadcasted_iota(jnp.int32, (C, C), 0), 1.0, 0.0).astype(BF16)
    gs = range(groups)
    cols = lambda ref, g: ref[:, g * W:(g + 1) * W]

    def bd(x):
        return jnp.concatenate([x] * HEADS_PER_GROUP, axis=0) * ones_bd

    def seg_sum(x):
        return _dot(jnp.concatenate(_split(x), axis=1), ones_bd2)

    def head_mm(lhs, ys):
        top = jnp.concatenate([bd(ys[0]), bd(ys[1])], axis=1)
        res = _dot(jnp.concatenate([jnp.concatenate(x, axis=1) for x in lhs], axis=0),
                   jnp.concatenate([top, top], axis=0))
        return [res[i * C:(i + 1) * C, :W] + res[i * C:(i + 1) * C, W:] for i in range(len(lhs))]

    k = [cols(k_ref, g) for g in gs]
    v = [cols(v_ref, g).astype(BF16) for g in gs]
    a = [cols(a_ref, g) for g in gs]
    ld = [-DECAY_SCALE * _sigmoid(cols(w_ref, g)) for g in gs]
    kk = [k[g] * cols(kk_ref, g) for g in gs]
    kk = [kk[g] * lax.rsqrt(jnp.maximum(seg_sum(kk[g] * kk[g]), 1e-24)) for g in gs]
    kh = [k[g] * (1.0 + (a[g] - 1.0) * cols(ka_ref, g)) for g in gs]
    rb = [kk[g] * a[g] for g in gs]

    cum2 = [_dot(ltri, jnp.concatenate(_split(ld[g]), axis=1)) for g in gs]
    cum = [cum2[g][:, :W] + cum2[g][:, W:] for g in gs]
    c_last = [cum[g][C - 1:C, :] for g in gs]
    e_nc = [jnp.exp(-cum[g]) for g in gs]
    lhs = [jnp.concatenate([-kk[g] * jnp.exp(cum[g] - ld[g]), cols(r_ref, g) * jnp.exp(cum[g])],
                           axis=0).astype(BF16) for g in gs]
    b_hat = [(rb[g] * e_nc[g]).astype(BF16) for g in gs]
    k_hat = [(kh[g] * e_nc[g]).astype(BF16) for g in gs]
    gram = [_dot_nt(lhs[g], jnp.concatenate([bd(b_hat[g]), bd(k_hat[g])], axis=0)) for g in gs]
    n_ab = [jnp.where(tril_strict, gram[g][:C, :W], 0.0) for g in gs]
    b_rb = [jnp.where(tril_incl, gram[g][C:, :W], 0.0).astype(BF16) for g in gs]
    ak_rk = [jnp.concatenate([jnp.where(tril_strict, gram[g][:C, W:], 0.0),
                              jnp.where(tril_incl, gram[g][C:, W:], 0.0)], axis=0).astype(BF16) for g in gs]

    assert CHUNK == 4 << INV_BLOCK_SHIFT and INV_BLOCK_SHIFT == 4
    nd = [jnp.where(diag_block, n_ab[g], 0.0) for g in gs]
    no_s = [_split(n_ab[g] - nd[g]) for g in gs]
    pd = [eye + nd[g] for g in gs]
    x_s = [_split(nd[g]) for g in gs]
    x_s = [_split(head_mm([x_s[g]], x_s[g])[0]) for g in gs]
    for _ in range(2):
        prod = [head_mm([x_s[g], _split(pd[g])], x_s[g]) for g in gs]
        pd = [pd[g] + prod[g][1] for g in gs]
        x_s = [_split(prod[g][0]) for g in gs]
    pd = [pd[g] + head_mm([_split(pd[g])], x_s[g])[0] for g in gs]
    pd_s = [_split(pd[g]) for g in gs]
    m = [head_mm([pd_s[g]], no_s[g])[0] for g in gs]
    m_s = [_split(m[g]) for g in gs]
    m2_s = [_split(head_mm([m_s[g]], m_s[g])[0]) for g in gs]
    q = [eye + m[g] for g in gs]
    q_s = [_split(q[g] + head_mm([_split(q[g])], m2_s[g])[0]) for g in gs]
    p_s = [_split(head_mm([q_s[g]], pd_s[g])[0]) for g in gs]

    s_t = [s_ref[g] for g in gs]
    zz = [_dot(jnp.concatenate([lhs[g], ak_rk[g]], axis=1),
               jnp.concatenate([bd(s_t[g].astype(BF16)), bd(v[g])], axis=0)) for g in gs]
    sa = [head_mm([p_s[g]], _split(zz[g][:C]))[0].astype(BF16) for g in gs]
    y = [zz[g][C:] + _dot(b_rb[g], bd(sa[g])) for g in gs]

    yc = [y[g] - seg_sum(y[g]) * (1.0 / RW_HEAD) for g in gs]
    var = [seg_sum(yc[g] * yc[g]) * (1.0 / RW_HEAD) for g in gs]
    kh = [cols(k_ref, g) * (1.0 + (cols(a_ref, g) - 1.0) * cols(ka_ref, g)) for g in gs]
    bonus = [seg_sum(cols(r_ref, g) * kh[g] * cols(rk_ref, g)) * cols(v_ref, g) for g in gs]
    for g in gs:
        yn = yc[g] * lax.rsqrt(var[g] + RW_LNX_EPS) * cols(lg_ref, g) + cols(lb_ref, g)
        o_ref[:, g * W:(g + 1) * W] = ((yn + bonus[g]) * cols(gate_ref, g)).astype(BF16)

    full = [_dot_tn(jnp.concatenate([b_hat[g], k_hat[g]], axis=0), jnp.concatenate([sa[g], v[g]], axis=0))
            for g in gs]
    c_col = [seg_sum(eye * c_last[g]) for g in gs]
    for g in gs:
        gain = s_t[g]
        for h in range(HEADS_PER_GROUP):
            gain = gain + jnp.where(lane_head == h, full[g][h * C:(h + 1) * C, :], 0.0)
        s_ref[g] = jnp.exp(c_col[g]) * gain


def _rwkv_recurrence(r, k, v, w_raw, a, gate, k_k, k_a, r_k, lnx_g, lnx_b, *, batch, groups_per_step):
    t, d = r.shape
    chunks = t // (batch * CHUNK)
    gb = min(groups_per_step, d // GROUP)
    wb = gb * GROUP
    seq_spec = pl.BlockSpec((CHUNK, wb), lambda b, g, c: (b * chunks + c, g))
    par_spec = pl.BlockSpec((1, wb), lambda b, g, c: (0, g))
    par = lambda p: p.reshape(1, d)
    return pl.pallas_call(
        functools.partial(_rwkv_kernel, groups=gb),
        grid=(batch, d // wb, chunks),
        in_specs=[seq_spec] * 6 + [par_spec] * 5,
        out_specs=seq_spec,
        out_shape=jax.ShapeDtypeStruct((t, d), BF16),
        scratch_shapes=[pltpu.VMEM((gb, CHUNK, GROUP), F32)],
        compiler_params=_cparams("parallel", "parallel", "arbitrary"), name="rwkv_recurrence",
    )(r, k, v, w_raw, a, gate, par(k_k), par(k_a), par(r_k), par(lnx_g), par(lnx_b))


def _log_gate(x):
    return (jnp.minimum(x, 0.0) - jnp.log(1.0 + jnp.exp(-jnp.abs(x)))) * (1.0 / GLA_GATE_TAU)


def _post_norm_kernel(h_ref, m_ref, gp_ref, *rest):
    h = h_ref[...] + _rms(m_ref[...].astype(F32)) * gp_ref[...]
    if len(rest) == 1:
        rest[0][...] = h
        return
    gn_ref, *gate_refs, h_out, p_out = rest if len(rest) == 3 else rest[:-1]
    pre = (_rms(h) * gn_ref[...]).astype(BF16)
    h_out[...] = h
    p_out[...] = pre
    if gate_refs:
        a1_ref, a2_ref, ab_ref = gate_refs
        rest[-1][...] = _log_gate(_dot(_dot(pre, a1_ref[...]).astype(BF16), a2_ref[...]) + ab_ref[...])


def _post_norm_final(h, m, g_post, *, batch, lp, start, seq, tm):
    d = h.shape[1]
    align = BF16_SUBLANES
    assert lp % align == 0 and start % align == 0 and tm % align == 0
    in_spec = pl.BlockSpec((pl.Element(tm), pl.Element(d)),
                           lambda b, i: (pl.multiple_of(b * lp + start + i * tm, align), 0))
    return pl.pallas_call(
        _post_norm_kernel, grid=(batch, seq // tm),
        in_specs=[in_spec, in_spec, pl.BlockSpec((1, d), lambda b, i: (0, 0))],
        out_specs=pl.BlockSpec((None, tm, d), lambda b, i: (b, i, 0)),
        out_shape=jax.ShapeDtypeStruct((batch, seq, d), F32),
        compiler_params=_cparams("parallel", "parallel"), name="post_norm_final",
    )(h, m, g_post.reshape(1, d))


def _post_norm(h, m, g_post, g_next, *, tm, gate=None):
    t, d = h.shape
    row_spec = pl.BlockSpec((tm, d), lambda i: (i, 0))
    par_spec = pl.BlockSpec((1, d), lambda i: (0, 0))
    in_specs = [row_spec, row_spec, par_spec, par_spec]
    out_specs = [row_spec, row_spec]
    out_shape = [jax.ShapeDtypeStruct((t, d), F32), jax.ShapeDtypeStruct((t, d), BF16)]
    args = [h, m, g_post.reshape(1, d), g_next.reshape(1, d)]
    if gate is not None:
        a1, a2 = _pad_rank(gate[0], gate[1])
        dk = a2.shape[1]
        args += [a1, a2, gate[2].reshape(1, dk)]
        in_specs += [pl.BlockSpec(a.shape, lambda i: (0, 0)) for a in args[4:]]
        out_specs.append(pl.BlockSpec((tm, dk), lambda i: (i, 0)))
        out_shape.append(jax.ShapeDtypeStruct((t, dk), F32))
    return pl.pallas_call(
        _post_norm_kernel, grid=(t // tm,),
        in_specs=in_specs, out_specs=out_specs, out_shape=out_shape,
        compiler_params=_cparams("parallel"), name="post_norm",
    )(*args)


def _ffn_in_kernel(x_ref, halo_ref, wu_ref, wg_ref, cw_ref, cb_ref, wd_ref, o_ref, wd_out, *,
                   blocks_per_seq, rows, halo_rows):
    i = pl.program_id(0)
    tm = x_ref.shape[0]
    wd_out[...] = wd_ref[...].astype(BF16)
    wu = wu_ref[...].astype(BF16)
    wg = wg_ref[...].astype(BF16)
    zh = _dot(halo_ref[...], wg)
    zh = jnp.where(i % blocks_per_seq == 0, 0.0, zh)
    sub = _pick(rows, (352, 688, 96, 48, 16))
    prev1, prev2 = zh[halo_rows - 1:halo_rows, :], zh[halo_rows - 2:halo_rows - 1, :]
    row = lax.broadcasted_iota(jnp.int32, (sub, wu.shape[1]), 0)

    def project(s):
        xs = x_ref[s * sub:(s + 1) * sub, :]
        return _dot(xs, wu), _dot(xs, wg)

    cur = project(0)
    for s in range(rows // sub):
        nxt = project(s + 1) if (s + 1) * sub < rows else None
        u, z = cur
        z1 = jnp.where(row == 0, prev1, pltpu.roll(z, 1, 0))
        z2 = pltpu.roll(z, 2, 0)
        z2 = jnp.where(row == 1, prev1, z2)
        z2 = jnp.where(row == 0, prev2, z2)
        zc = z * cw_ref[2:3, :] + z1 * cw_ref[1:2, :] + z2 * cw_ref[0:1, :] + cb_ref[...]
        o_ref[s * sub:(s + 1) * sub, :] = (zc * _sigmoid(zc) * u).astype(BF16)
        prev1, prev2 = z[sub - 1:sub, :], z[sub - 2:sub - 1, :]
        cur = nxt
    if rows < tm:
        o_ref[rows:, :] = jnp.zeros((tm - rows, o_ref.shape[1]), o_ref.dtype)


def _ffn_in(x, w_up, w_gate, w_down, layer, conv_w, conv_b, *, lp, length, tm, tn):
    t, d = x.shape
    f = w_up.shape[2]
    hr = BF16_SUBLANES
    ni, nj = t // tm, f // tn
    slab = f // (ni * nj)
    assert slab * ni * nj == f and slab % hr == 0
    w_spec = pl.BlockSpec((None, d, tn), lambda i, j: (layer, 0, j))
    return pl.pallas_call(
        functools.partial(_ffn_in_kernel, blocks_per_seq=lp // tm, halo_rows=hr,
                          rows=length if tm == lp and length % hr == 0 else tm),
        grid=(ni, nj),
        in_specs=[pl.BlockSpec((tm, d), lambda i, j: (i, 0), pipeline_mode=pl.Buffered(1)),
                  pl.BlockSpec((hr, d), lambda i, j: (jnp.maximum(i * (tm // hr) - 1, 0), 0)),
                  w_spec, w_spec,
                  pl.BlockSpec((conv_w.shape[0], tn), lambda i, j: (0, j)),
                  pl.BlockSpec((1, tn), lambda i, j: (0, j)),
                  pl.BlockSpec((None, slab, d), lambda i, j: (layer, i * nj + j, 0))],
        out_specs=[pl.BlockSpec((tm, tn), lambda i, j: (i, j)),
                   pl.BlockSpec((slab, d), lambda i, j: (i * nj + j, 0))],
        out_shape=[jax.ShapeDtypeStruct((t, f), BF16), jax.ShapeDtypeStruct((f, d), BF16)],
        compiler_params=_cparams("parallel", "arbitrary"), name="ffn_in",
    )(x, x, w_up, w_gate, conv_w, conv_b.reshape(1, f), w_down)


GLA_LEVELS = 6


def _gla_select_matrices():
    t_i = np.arange(CHUNK)[:, None]
    u_i = np.arange(CHUNK)[None, :]
    mats = [u_i <= t_i]
    for l in range(GLA_LEVELS):
        half = 1 << l
        p = (t_i & ~(2 * half - 1)) + (half - 1)
        mats.append(np.where((t_i & half) != 0, (u_i > p) & (u_i <= t_i), (u_i > t_i) & (u_i <= p)))
    sel = np.concatenate(mats, axis=0).astype(np.float32)
    return jnp.asarray(np.concatenate([sel, sel], axis=1), dtype=BF16)


def _gla_kernel(sel_ref, q_ref, k_ref, v_ref, g_ref, r_ref, rb_ref, ng_ref, o_ref, s_ref, *, heads, hk, hv):
    c_id = pl.program_id(2)

    @pl.when(c_id == 0)
    def _():
        s_ref[...] = jnp.zeros_like(s_ref)

    C = CHUNK
    t_i = lax.broadcasted_iota(jnp.int32, (C, C), 0)
    u_i = lax.broadcasted_iota(jnp.int32, (C, C), 1)
    row = lax.broadcasted_iota(jnp.int32, (C, 1), 0)
    eye = t_i == u_i
    hs = range(heads)
    sel = sel_ref[...]

    q = [q_ref[:, h * hk:(h + 1) * hk] * (hk ** -0.5) for h in hs]
    k = [k_ref[:, h * hk:(h + 1) * hk] for h in hs]
    v = [v_ref[:, h * hv:(h + 1) * hv].astype(BF16) for h in hs]
    g_s = [_split(g_ref[:, h * hk:(h + 1) * hk]) for h in hs]
    cums = [_dot(sel, jnp.concatenate(g_s[h], axis=0)) for h in hs]
    b = [cums[h][:C] for h in hs]
    b_last = [b[h][C - 1:C, :] for h in hs]

    a = [jnp.where(eye, jnp.sum(q[h] * k[h], axis=-1, keepdims=True), 0.0) for h in hs]
    for l in range(GLA_LEVELS):
        upper = (row & (1 << l)) != 0
        same_block = (t_i >> (l + 1)) == (u_i >> (l + 1))
        qk = [jnp.where(upper, q[h], k[h]) * jnp.exp(cums[h][(1 + l) * C:(2 + l) * C]) for h in hs]
        ql = [jnp.where(upper, qk[h], 0.0).astype(BF16) for h in hs]
        kl = [jnp.where(upper, 0.0, qk[h]).astype(BF16) for h in hs]
        a = [a[h] + jnp.where(same_block, _dot_nt(ql[h], kl[h]), 0.0) for h in hs]

    s_t = [s_ref[h] for h in hs]
    qe = [(q[h] * jnp.exp(b[h])).astype(BF16) for h in hs]
    kr = [(k[h] * jnp.exp(b_last[h] - b[h])).astype(BF16) for h in hs]
    o = [_dot_nt(qe[h], s_t[h].astype(BF16)) + _dot(a[h].astype(BF16), v[h]) for h in hs]
    for h in hs:
        s_ref[h] = s_t[h] * jnp.exp(b_last[h]) + _dot_tn(v[h], kr[h])
    for h in hs:
        zg = r_ref[:, h * hv:(h + 1) * hv] + rb_ref[:, h * hv:(h + 1) * hv]
        o_ref[:, h * hv:(h + 1) * hv] = (_rms(o[h]) * ng_ref[...] * (zg * _sigmoid(zg))).astype(BF16)


def _gla_recurrence(proj, g, r_b, norm_g, *, batch, heads, hk, hv, heads_per_step):
    t = proj.shape[0]
    hb = heads_per_step
    chunks = t // (batch * CHUNK)
    dk, dv = heads * hk, heads * hv
    sel = _gla_select_matrices()

    def cols(width, start):
        assert start % (hb * width) == 0
        off = start // (hb * width)
        return pl.BlockSpec((CHUNK, hb * width), lambda b, h, c: (b * chunks + c, h + off))

    return pl.pallas_call(
        functools.partial(_gla_kernel, heads=hb, hk=hk, hv=hv),
        grid=(batch, heads // hb, chunks),
        in_specs=[pl.BlockSpec(sel.shape, lambda b, h, c: (0, 0)),
                  cols(hk, 0), cols(hk, dk), cols(hv, 2 * dk),
                  pl.BlockSpec((CHUNK, hb * hk), lambda b, h, c: (b * chunks + c, h)),
                  cols(hv, 2 * dk + dv),
                  pl.BlockSpec((1, hb * hv), lambda b, h, c: (0, h)),
                  pl.BlockSpec((1, hv), lambda b, h, c: (0, 0))],
        out_specs=pl.BlockSpec((CHUNK, hb * hv), lambda b, h, c: (b * chunks + c, h)),
        out_shape=jax.ShapeDtypeStruct((t, dv), BF16),
        scratch_shapes=[pltpu.VMEM((hb, hv, hk), F32)],
        compiler_params=_cparams("parallel", "parallel", "arbitrary"), name="gla_recurrence",
    )(sel, proj, proj, proj, g, proj, r_b.reshape(1, dv), norm_g.reshape(1, hv))


def _tiles(lp, length):
    return dict(
        lp=lp, length=length,
        ew=_pick(lp, (192, 128, 64)),
        norm=_pick(lp, (352, 192, 128, 64)),
        mm=_pick(lp, (1056, 704, 352, 192, 128, 64)),
        down=_pick(lp, (704, 1056, 352, 192, 128, 64)),
        ffn=_pick(lp, (2112, 1056, 704, 352, 192, 128, 64)),
        rw_groups=16,
        gla_heads=4,
    )


def _col_tile(n, cands=(512, 256, 128)):
    return _pick(n, cands)


def _rwkv_layer(h, g_pre, p, *, batch, tiles):
    d = h.shape[1]
    xr, xk, xv, w_raw, a, gate = _rw_input(h, g_pre, p, lp=tiles["lp"], tm=tiles["ew"])
    mm = functools.partial(_matmul, tm=tiles["mm"], tn=_col_tile(d))
    r = mm(xr, p["wr"], name="rw_r")
    k = mm(xk, p["wk"], name="rw_k")
    v = mm(xv, p["wv"], name="rw_v")
    yo = _rwkv_recurrence(r, k, v, w_raw, a, gate, p["k_k"], p["k_a"], p["r_k"].reshape(-1),
                          p["lnx_g"], p["lnx_b"], batch=batch, groups_per_step=tiles["rw_groups"])
    return mm(yo, p["wo"], out_dtype=BF16, name="rw_o")


def _gla_layer(pre, g, p, *, batch, tiles):
    d = pre.shape[1]
    dk = p["a2"].shape[1]
    dv = p["wo"].shape[0]
    heads = max(4, d // 512)
    hk, hv = dk // heads, dv // heads
    mm = functools.partial(_matmul, tm=tiles["mm"])
    whole = tiles["ffn"] == tiles["lp"] and tiles["length"] % BF16_SUBLANES == 0
    proj = _matmul(pre, p["w_in"], tm=tiles["ffn"] if whole else tiles["mm"], tn=_col_tile(dk),
                   rows=tiles["length"] if whole else None, name="gla_in")
    og = _gla_recurrence(proj, g, p["r_b"], p["norm_g"], batch=batch, heads=heads, hk=hk, hv=hv,
                         heads_per_step=min(tiles["gla_heads"], heads))
    return mm(og, p["wo"], tn=_col_tile(d), out_dtype=BF16, name="gla_o")


def _ffn(pre, w_up, w_gate, w_down, layer, conv_w, conv_b, *, tiles):
    f = w_up.shape[2]
    act, w_down_bf = _ffn_in(pre, w_up, w_gate, w_down, layer, conv_w, conv_b, lp=tiles["lp"],
                             length=tiles["length"], tm=tiles["ffn"], tn=_col_tile(f, (256, 128)))
    return _matmul(act, w_down_bf, tm=tiles["down"], tn=_col_tile(w_down.shape[2], (256, 128)),
                   out_dtype=BF16, name="ffn_down")


def kernel(x, meta, norm_g, rw_mu, rw_w0, rw_w1, rw_w2, rw_a0, rw_a1, rw_a2, rw_g1, rw_g2, rw_k_k, rw_k_a, rw_r_k, rw_wr, rw_wk, rw_wv, rw_wo, rw_lnx_g, rw_lnx_b, gla_w_in, gla_a1, gla_a2, gla_a_b, gla_r_b, gla_norm_g, gla_wo, ffn_up, ffn_gate, ffn_conv, ffn_conv_b, ffn_down):
    batch, seq, d = x.shape
    n_meta = meta.shape[0]
    depth = norm_g.shape[0]
    length = n_meta + seq
    lp = -(-length // CHUNK) * CHUNK
    m = jnp.broadcast_to(meta.astype(x.dtype)[None], (batch, n_meta, d))
    h = jnp.concatenate([m, x, jnp.zeros((batch, lp - length, d), x.dtype)], axis=1).reshape(batch * lp, d)
    tiles = _tiles(lp, length)

    pre = gates = None
    for i in range(depth):
        j = i // 2
        if i % 2 == 0:
            p = dict(mu=rw_mu[j], w0=rw_w0[j], w1=rw_w1[j], w2=rw_w2[j], a0=rw_a0[j], a1=rw_a1[j],
                     a2=rw_a2[j], g1=rw_g1[j], g2=rw_g2[j], k_k=rw_k_k[j], k_a=rw_k_a[j], r_k=rw_r_k[j],
                     wr=rw_wr[j], wk=rw_wk[j], wv=rw_wv[j], wo=rw_wo[j], lnx_g=rw_lnx_g[j],
                     lnx_b=rw_lnx_b[j])
            mix = _rwkv_layer(h, norm_g[i, 0], p, batch=batch, tiles=tiles)
        else:
            p = dict(w_in=gla_w_in[j], a1=gla_a1[j], a2=gla_a2[j], a_b=gla_a_b[j], r_b=gla_r_b[j],
                     norm_g=gla_norm_g[j], wo=gla_wo[j])
            mix = _gla_layer(pre, gates, p, batch=batch, tiles=tiles)
        h, pre = _post_norm(h, mix, norm_g[i, 1], norm_g[i, 2], tm=tiles["norm"])
        f = _ffn(pre, ffn_up, ffn_gate, ffn_down, i, ffn_conv[i], ffn_conv_b[i], tiles=tiles)
        if i + 1 < depth and (i + 1) % 2 == 1:
            jn = (i + 1) // 2
            h, pre, gates = _post_norm(h, f, norm_g[i, 3], norm_g[i + 1, 0], tm=tiles["norm"],
                                       gate=(gla_a1[jn], gla_a2[jn], gla_a_b[jn]))
        elif i + 1 < depth:
            h, pre = _post_norm(h, f, norm_g[i, 3], norm_g[i + 1, 0], tm=tiles["norm"])
    return _post_norm_final(h, f, norm_g[depth - 1, 3], batch=batch, lp=lp, start=n_meta, seq=seq,
                            tm=_pick(seq, (256, 128, 64)))
```

```python
import functools

import numpy as np
import jax
import jax.numpy as jnp
from jax import lax
from jax.experimental import pallas as pl
from jax.experimental.pallas import tpu as pltpu

F32 = jnp.float32
BF16 = jnp.bfloat16

NORM_EPS = 1e-6
RW_LNX_EPS = 64e-5
RW_HEAD = 64
RW_HEAD_SHIFT = RW_HEAD.bit_length() - 1
INV_BLOCK_SHIFT = 4
DECAY_SCALE = float(np.exp(-0.5))
GLA_GATE_TAU = 16.0
CHUNK = 64
LANES = 128
BF16_SUBLANES = 16
GROUP = LANES
HEADS_PER_GROUP = GROUP // RW_HEAD
VMEM_LIMIT = 56 * 1024 * 1024


def _cparams(*sem):
    return pltpu.CompilerParams(dimension_semantics=sem, vmem_limit_bytes=VMEM_LIMIT)


def _pick(n, cands):
    for c in cands:
        if n % c == 0:
            return c
    raise ValueError(f"no tile for {n} in {cands}")


def _rms(x):
    return x * lax.rsqrt(jnp.mean(x * x, axis=-1, keepdims=True) + NORM_EPS)


def _sigmoid(z):
    return 1.0 / (1.0 + jnp.exp(-z))


def _dot(a, b):
    return jnp.dot(a, b, preferred_element_type=F32)


def _dot_nt(a, b):
    return lax.dot_general(a, b, (((1,), (1,)), ((), ())), preferred_element_type=F32)


def _dot_tn(a, b):
    return lax.dot_general(a, b, (((0,), (0,)), ((), ())), preferred_element_type=F32)


def _split(x):
    hi = x.astype(BF16)
    lo = (x - hi.astype(F32)).astype(BF16)
    return hi, lo


def _mm_kernel(x_ref, w_ref, o_ref, *, rows):
    w = w_ref[...].astype(BF16)
    o_ref[:rows, :] = _dot(x_ref[:rows, :], w).astype(o_ref.dtype)
    if rows < o_ref.shape[0]:
        o_ref[rows:, :] = jnp.zeros((o_ref.shape[0] - rows, o_ref.shape[1]), o_ref.dtype)


def _matmul(x, w, *, tm, tn, rows=None, out_dtype=F32, name):
    t, k = x.shape
    n = w.shape[1]
    x_mode = {} if rows is None else dict(pipeline_mode=pl.Buffered(1))
    return pl.pallas_call(
        functools.partial(_mm_kernel, rows=tm if rows is None else rows), grid=(t // tm, n // tn),
        in_specs=[pl.BlockSpec((tm, k), lambda i, j: (i, 0), **x_mode),
                  pl.BlockSpec((k, tn), lambda i, j: (0, j))],
        out_specs=pl.BlockSpec((tm, tn), lambda i, j: (i, j)),
        out_shape=jax.ShapeDtypeStruct((t, n), out_dtype),
        compiler_params=_cparams("parallel", "arbitrary"), name=name,
    )(x, w)


def _pad_rank(w1, w2):
    r = w1.shape[1]
    rp = -(-r // LANES) * LANES
    return (jnp.pad(w1, ((0, 0), (0, rp - r))).astype(BF16),
            jnp.pad(w2, ((0, rp - r), (0, 0))).astype(BF16))


def _rw_input_kernel(h_ref, halo_ref, g_ref, mu_ref, w1_ref, w2_ref, a1_ref, a2_ref, g1_ref, g2_ref,
                     w0_ref, a0_ref, xr_ref, xk_ref, xv_ref, wraw_ref, a_ref, gate_ref, *, blocks_per_seq):
    i = pl.program_id(0)
    g = g_ref[...]
    pre = _rms(h_ref[...]) * g
    halo = (_rms(halo_ref[...]) * g)[7:8, :]
    halo = jnp.where(i % blocks_per_seq == 0, 0.0, halo)
    row = lax.broadcasted_iota(jnp.int32, pre.shape, 0)
    prev = jnp.where(row == 0, halo, pltpu.roll(pre, 1, 0))
    xx = prev - pre
    mix = lambda m: (pre + xx * mu_ref[m:m + 1, :]).astype(BF16)
    xr_ref[...] = mix(0)
    xk_ref[...] = mix(2)
    xv_ref[...] = mix(3)
    wraw_ref[...] = w0_ref[...] + _dot(jnp.tanh(_dot(mix(1), w1_ref[...])).astype(BF16), w2_ref[...])
    a_ref[...] = _sigmoid(a0_ref[...] + _dot(_dot(mix(4), a1_ref[...]).astype(BF16), a2_ref[...]))
    gate_ref[...] = _dot(_sigmoid(_dot(mix(5), g1_ref[...])).astype(BF16), g2_ref[...])


def _rw_input(h, g, p, *, lp, tm):
    t, d = h.shape
    row_spec = pl.BlockSpec((tm, d), lambda i: (i, 0))
    const = lambda arr: pl.BlockSpec(arr.shape, lambda i: (0, 0), pipeline_mode=pl.Buffered(1))
    lora_w = [w for pair in (("w1", "w2"), ("a1", "a2"), ("g1", "g2")) for w in _pad_rank(p[pair[0]], p[pair[1]])]
    vec = lambda a: a.reshape(1, d)
    consts = [vec(g), p["mu"], *lora_w, vec(p["w0"]), vec(p["a0"])]
    return pl.pallas_call(
        functools.partial(_rw_input_kernel, blocks_per_seq=lp // tm),
        grid=(t // tm,),
        in_specs=[row_spec, pl.BlockSpec((8, d), lambda i: (jnp.maximum(i * (tm // 8) - 1, 0), 0))]
                 + [const(c) for c in consts],
        out_specs=[row_spec] * 6,
        out_shape=[jax.ShapeDtypeStruct((t, d), BF16)] * 3 + [jax.ShapeDtypeStruct((t, d), F32)] * 3,
        compiler_params=_cparams("parallel"), name="rw_input",
    )(h, h, *consts)


def _rwkv_kernel(r_ref, k_ref, v_ref, w_ref, a_ref, gate_ref, kk_ref, ka_ref, rk_ref, lg_ref, lb_ref,
                 o_ref, s_ref, *, groups):
    c_id = pl.program_id(2)

    @pl.when(c_id == 0)
    def _():
        s_ref[...] = jnp.zeros_like(s_ref)

    C, W = CHUNK, GROUP
    row = lax.broadcasted_iota(jnp.int32, (C, W), 0)
    lane = lax.broadcasted_iota(jnp.int32, (C, W), 1)
    lane_in = lane & (RW_HEAD - 1)
    lane_head = lane >> RW_HEAD_SHIFT
    tril_incl = lane_in <= row
    tril_strict = lane_in < row
    eye = jnp.where(lane_in == row, 1.0, 0.0)
    diag_block = (lane_in >> INV_BLOCK_SHIFT) == (row >> INV_BLOCK_SHIFT)
    bd_mask = (lax.broadcasted_iota(jnp.int32, (HEADS_PER_GROUP * C, W), 0) >> RW_HEAD_SHIFT) == \
              (lax.broadcasted_iota(jnp.int32, (HEADS_PER_GROUP * C, W), 1) >> RW_HEAD_SHIFT)
    ones_bd = jnp.where(bd_mask, 1.0, 0.0).astype(BF16)
    ones_bd2 = jnp.concatenate([ones_bd, ones_bd], axis=0)
    ltri = jnp.where(lax.broadcasted_iota(jnp.int32, (C, C), 1) <=
                     lax.broadcasted_iota(jnp.int32, (C, C), 0), 1.0, 0.0).astype(BF16)
    gs = range(groups)
    cols = lambda ref, g: ref[:, g * W:(g + 1) * W]

    def bd(x):
        return jnp.concatenate([x] * HEADS_PER_GROUP, axis=0) * ones_bd

    def seg_sum(x):
        return _dot(jnp.concatenate(_split(x), axis=1), ones_bd2)

    def head_mm(lhs, ys):
        top = jnp.concatenate([bd(ys[0]), bd(ys[1])], axis=1)
        res = _dot(jnp.concatenate([jnp.concatenate(x, axis=1) for x in lhs], axis=0),
                   jnp.concatenate([top, top], axis=0))
        return [res[i * C:(i + 1) * C, :W] + res[i * C:(i + 1) * C, W:] for i in range(len(lhs))]

    k = [cols(k_ref, g) for g in gs]
    v = [cols(v_ref, g).astype(BF16) for g in gs]
    a = [cols(a_ref, g) for g in gs]
    ld = [-DECAY_SCALE * _sigmoid(cols(w_ref, g)) for g in gs]
    kk = [k[g] * cols(kk_ref, g) for g in gs]
    kk = [kk[g] * lax.rsqrt(jnp.maximum(seg_sum(kk[g] * kk[g]), 1e-24)) for g in gs]
    kh = [k[g] * (1.0 + (a[g] - 1.0) * cols(ka_ref, g)) for g in gs]
    rb = [kk[g] * a[g] for g in gs]

    cum2 = [_dot(ltri, jnp.concatenate(_split(ld[g]), axis=1)) for g in gs]
    cum = [cum2[g][:, :W] + cum2[g][:, W:] for g in gs]
    c_last = [cum[g][C - 1:C, :] for g in gs]
    e_nc = [jnp.exp(-cum[g]) for g in gs]
    lhs = [jnp.concatenate([-kk[g] * jnp.exp(cum[g] - ld[g]), cols(r_ref, g) * jnp.exp(cum[g])],
                           axis=0).astype(BF16) for g in gs]
    b_hat = [(rb[g] * e_nc[g]).astype(BF16) for g in gs]
    k_hat = [(kh[g] * e_nc[g]).astype(BF16) for g in gs]
    gram = [_dot_nt(lhs[g], jnp.concatenate([bd(b_hat[g]), bd(k_hat[g])], axis=0)) for g in gs]
    n_ab = [jnp.where(tril_strict, gram[g][:C, :W], 0.0) for g in gs]
    b_rb = [jnp.where(tril_incl, gram[g][C:, :W], 0.0).astype(BF16) for g in gs]
    ak_rk = [jnp.concatenate([jnp.where(tril_strict, gram[g][:C, W:], 0.0),
                              jnp.where(tril_incl, gram[g][C:, W:], 0.0)], axis=0).astype(BF16) for g in gs]

    assert CHUNK == 4 << INV_BLOCK_SHIFT and INV_BLOCK_SHIFT == 4
    nd = [jnp.where(diag_block, n_ab[g], 0.0) for g in gs]
    no_s = [_split(n_ab[g] - nd[g]) for g in gs]
    pd = [eye + nd[g] for g in gs]
    x_s = [_split(nd[g]) for g in gs]
    x_s = [_split(head_mm([x_s[g]], x_s[g])[0]) for g in gs]
    for _ in range(2):
        prod = [head_mm([x_s[g], _split(pd[g])], x_s[g]) for g in gs]
        pd = [pd[g] + prod[g][1] for g in gs]
        x_s = [_split(prod[g][0]) for g in gs]
    pd = [pd[g] + head_mm([_split(pd[g])], x_s[g])[0] for g in gs]
    pd_s = [_split(pd[g]) for g in gs]
    m = [head_mm([pd_s[g]], no_s[g])[0] for g in gs]
    m_s = [_split(m[g]) for g in gs]
    m2_s = [_split(head_mm([m_s[g]], m_s[g])[0]) for g in gs]
    q = [eye + m[g] for g in gs]
    q_s = [_split(q[g] + head_mm([_split(q[g])], m2_s[g])[0]) for g in gs]
    p_s = [_split(head_mm([q_s[g]], pd_s[g])[0]) for g in gs]

    s_t = [s_ref[g] for g in gs]
    zz = [_dot(jnp.concatenate([lhs[g], ak_rk[g]], axis=1),
               jnp.concatenate([bd(s_t[g].astype(BF16)), bd(v[g])], axis=0)) for g in gs]
    sa = [head_mm([p_s[g]], _split(zz[g][:C]))[0].astype(BF16) for g in gs]
    y = [zz[g][C:] + _dot(b_rb[g], bd(sa[g])) for g in gs]

    yc = [y[g] - seg_sum(y[g]) * (1.0 / RW_HEAD) for g in gs]
    var = [seg_sum(yc[g] * yc[g]) * (1.0 / RW_HEAD) for g in gs]
    kh = [cols(k_ref, g) * (1.0 + (cols(a_ref, g) - 1.0) * cols(ka_ref, g)) for g in gs]
    bonus = [seg_sum(cols(r_ref, g) * kh[g] * cols(rk_ref, g)) * cols(v_ref, g) for g in gs]
    for g in gs:
        yn = yc[g] * lax.rsqrt(var[g] + RW_LNX_EPS) * cols(lg_ref, g) + cols(lb_ref, g)
        o_ref[:, g * W:(g + 1) * W] = ((yn + bonus[g]) * cols(gate_ref, g)).astype(BF16)

    full = [_dot_tn(jnp.concatenate([b_hat[g], k_hat[g]], axis=0), jnp.concatenate([sa[g], v[g]], axis=0))
            for g in gs]
    c_col = [seg_sum(eye * c_last[g]) for g in gs]
    for g in gs:
        gain = s_t[g]
        for h in range(HEADS_PER_GROUP):
            gain = gain + jnp.where(lane_head == h, full[g][h * C:(h + 1) * C, :], 0.0)
        s_ref[g] = jnp.exp(c_col[g]) * gain


def _rwkv_recurrence(r, k, v, w_raw, a, gate, k_k, k_a, r_k, lnx_g, lnx_b, *, batch, groups_per_step):
    t, d = r.shape
    chunks = t // (batch * CHUNK)
    gb = min(groups_per_step, d // GROUP)
    wb = gb * GROUP
    seq_spec = pl.BlockSpec((CHUNK, wb), lambda b, g, c: (b * chunks + c, g))
    par_spec = pl.BlockSpec((1, wb), lambda b, g, c: (0, g))
    par = lambda p: p.reshape(1, d)
    return pl.pallas_call(
        functools.partial(_rwkv_kernel, groups=gb),
        grid=(batch, d // wb, chunks),
        in_specs=[seq_spec] * 6 + [par_spec] * 5,
        out_specs=seq_spec,
        out_shape=jax.ShapeDtypeStruct((t, d), BF16),
        scratch_shapes=[pltpu.VMEM((gb, CHUNK, GROUP), F32)],
        compiler_params=_cparams("parallel", "parallel", "arbitrary"), name="rwkv_recurrence",
    )(r, k, v, w_raw, a, gate, par(k_k), par(k_a), par(r_k), par(lnx_g), par(lnx_b))


def _log_gate(x):
    return (jnp.minimum(x, 0.0) - jnp.log(1.0 + jnp.exp(-jnp.abs(x)))) * (1.0 / GLA_GATE_TAU)


def _post_norm_kernel(h_ref, m_ref, gp_ref, *rest):
    h = h_ref[...] + _rms(m_ref[...].astype(F32)) * gp_ref[...]
    if len(rest) == 1:
        rest[0][...] = h
        return
    gn_ref, *gate_refs, h_out, p_out = rest if len(rest) == 3 else rest[:-1]
    pre = (_rms(h) * gn_ref[...]).astype(BF16)
    h_out[...] = h
    p_out[...] = pre
    if gate_refs:
        a1_ref, a2_ref, ab_ref = gate_refs
        rest[-1][...] = _log_gate(_dot(_dot(pre, a1_ref[...]).astype(BF16), a2_ref[...]) + ab_ref[...])


def _post_norm_final(h, m, g_post, *, batch, lp, start, seq, tm):
    d = h.shape[1]
    align = BF16_SUBLANES
    assert lp % align == 0 and start % align == 0 and tm % align == 0
    in_spec = pl.BlockSpec((pl.Element(tm), pl.Element(d)),
                           lambda b, i: (pl.multiple_of(b * lp + start + i * tm, align), 0))
    return pl.pallas_call(
        _post_norm_kernel, grid=(batch, seq // tm),
        in_specs=[in_spec, in_spec, pl.BlockSpec((1, d), lambda b, i: (0, 0))],
        out_specs=pl.BlockSpec((None, tm, d), lambda b, i: (b, i, 0)),
        out_shape=jax.ShapeDtypeStruct((batch, seq, d), F32),
        compiler_params=_cparams("parallel", "parallel"), name="post_norm_final",
    )(h, m, g_post.reshape(1, d))


def _post_norm(h, m, g_post, g_next, *, tm, gate=None):
    t, d = h.shape
    row_spec = pl.BlockSpec((tm, d), lambda i: (i, 0))
    par_spec = pl.BlockSpec((1, d), lambda i: (0, 0))
    in_specs = [row_spec, row_spec, par_spec, par_spec]
    out_specs = [row_spec, row_spec]
    out_shape = [jax.ShapeDtypeStruct((t, d), F32), jax.ShapeDtypeStruct((t, d), BF16)]
    args = [h, m, g_post.reshape(1, d), g_next.reshape(1, d)]
    if gate is not None:
        a1, a2 = _pad_rank(gate[0], gate[1])
        dk = a2.shape[1]
        args += [a1, a2, gate[2].reshape(1, dk)]
        in_specs += [pl.BlockSpec(a.shape, lambda i: (0, 0)) for a in args[4:]]
        out_specs.append(pl.BlockSpec((tm, dk), lambda i: (i, 0)))
        out_shape.append(jax.ShapeDtypeStruct((t, dk), F32))
    return pl.pallas_call(
        _post_norm_kernel, grid=(t // tm,),
        in_specs=in_specs, out_specs=out_specs, out_shape=out_shape,
        compiler_params=_cparams("parallel"), name="post_norm",
    )(*args)


def _ffn_in_kernel(x_ref, halo_ref, wu_ref, wg_ref, cw_ref, cb_ref, wd_ref, o_ref, wd_out, *,
                   blocks_per_seq, rows, halo_rows):
    i = pl.program_id(0)
    tm = x_ref.shape[0]
    wd_out[...] = wd_ref[...].astype(BF16)
    wu = wu_ref[...].astype(BF16)
    wg = wg_ref[...].astype(BF16)
    zh = _dot(halo_ref[...], wg)
    zh = jnp.where(i % blocks_per_seq == 0, 0.0, zh)
    sub = _pick(rows, (352, 688, 96, 48, 16))
    prev1, prev2 = zh[halo_rows - 1:halo_rows, :], zh[halo_rows - 2:halo_rows - 1, :]
    row = lax.broadcasted_iota(jnp.int32, (sub, wu.shape[1]), 0)

    def project(s):
        xs = x_ref[s * sub:(s + 1) * sub, :]
        return _dot(xs, wu), _dot(xs, wg)

    cur = project(0)
    for s in range(rows // sub):
        nxt = project(s + 1) if (s + 1) * sub < rows else None
        u, z = cur
        z1 = jnp.where(row == 0, prev1, pltpu.roll(z, 1, 0))
        z2 = pltpu.roll(z, 2, 0)
        z2 = jnp.where(row == 1, prev1, z2)
        z2 = jnp.where(row == 0, prev2, z2)
        zc = z * cw_ref[2:3, :] + z1 * cw_ref[1:2, :] + z2 * cw_ref[0:1, :] + cb_ref[...]
        o_ref[s * sub:(s + 1) * sub, :] = (zc * _sigmoid(zc) * u).astype(BF16)
        prev1, prev2 = z[sub - 1:sub, :], z[sub - 2:sub - 1, :]
        cur = nxt
    if rows < tm:
        o_ref[rows:, :] = jnp.zeros((tm - rows, o_ref.shape[1]), o_ref.dtype)


def _ffn_in(x, w_up, w_gate, w_down, layer, conv_w, conv_b, *, lp, length, tm, tn):
    t, d = x.shape
    f = w_up.shape[2]
    hr = BF16_SUBLANES
    ni, nj = t // tm, f // tn
    slab = f // (ni * nj)
    assert slab * ni * nj == f and slab % hr == 0
    w_spec = pl.BlockSpec((None, d, tn), lambda i, j: (layer, 0, j))
    return pl.pallas_call(
        functools.partial(_ffn_in_kernel, blocks_per_seq=lp // tm, halo_rows=hr,
                          rows=length if tm == lp and length % hr == 0 else tm),
        grid=(ni, nj),
        in_specs=[pl.BlockSpec((tm, d), lambda i, j: (i, 0), pipeline_mode=pl.Buffered(1)),
                  pl.BlockSpec((hr, d), lambda i, j: (jnp.maximum(i * (tm // hr) - 1, 0), 0)),
                  w_spec, w_spec,
                  pl.BlockSpec((conv_w.shape[0], tn), lambda i, j: (0, j)),
                  pl.BlockSpec((1, tn), lambda i, j: (0, j)),
                  pl.BlockSpec((None, slab, d), lambda i, j: (layer, i * nj + j, 0))],
        out_specs=[pl.BlockSpec((tm, tn), lambda i, j: (i, j)),
                   pl.BlockSpec((slab, d), lambda i, j: (i * nj + j, 0))],
        out_shape=[jax.ShapeDtypeStruct((t, f), BF16), jax.ShapeDtypeStruct((f, d), BF16)],
        compiler_params=_cparams("parallel", "arbitrary"), name="ffn_in",
    )(x, x, w_up, w_gate, conv_w, conv_b.reshape(1, f), w_down)


GLA_LEVELS = 6


def _gla_select_matrices():
    t_i = np.arange(CHUNK)[:, None]
    u_i = np.arange(CHUNK)[None, :]
    mats = [u_i <= t_i]
    for l in range(GLA_LEVELS):
        half = 1 << l
        p = (t_i & ~(2 * half - 1)) + (half - 1)
        mats.append(np.where((t_i & half) != 0, (u_i > p) & (u_i <= t_i), (u_i > t_i) & (u_i <= p)))
    sel = np.concatenate(mats, axis=0).astype(np.float32)
    return jnp.asarray(np.concatenate([sel, sel], axis=1), dtype=BF16)


def _gla_kernel(sel_ref, q_ref, k_ref, v_ref, g_ref, r_ref, rb_ref, ng_ref, o_ref, s_ref, *, heads, hk, hv):
    c_id = pl.program_id(2)

    @pl.when(c_id == 0)
    def _():
        s_ref[...] = jnp.zeros_like(s_ref)

    C = CHUNK
    t_i = lax.broadcasted_iota(jnp.int32, (C, C), 0)
    u_i = lax.broadcasted_iota(jnp.int32, (C, C), 1)
    row = lax.broadcasted_iota(jnp.int32, (C, 1), 0)
    eye = t_i == u_i
    hs = range(heads)
    sel = sel_ref[...]

    q = [q_ref[:, h * hk:(h + 1) * hk] * (hk ** -0.5) for h in hs]
    k = [k_ref[:, h * hk:(h + 1) * hk] for h in hs]
    v = [v_ref[:, h * hv:(h + 1) * hv].astype(BF16) for h in hs]
    g_s = [_split(g_ref[:, h * hk:(h + 1) * hk]) for h in hs]
    cums = [_dot(sel, jnp.concatenate(g_s[h], axis=0)) for h in hs]
    b = [cums[h][:C] for h in hs]
    b_last = [b[h][C - 1:C, :] for h in hs]

    a = [jnp.where(eye, jnp.sum(q[h] * k[h], axis=-1, keepdims=True), 0.0) for h in hs]
    for l in range(GLA_LEVELS):
        upper = (row & (1 << l)) != 0
        same_block = (t_i >> (l + 1)) == (u_i >> (l + 1))
        qk = [jnp.where(upper, q[h], k[h]) * jnp.exp(cums[h][(1 + l) * C:(2 + l) * C]) for h in hs]
        ql = [jnp.where(upper, qk[h], 0.0).astype(BF16) for h in hs]
        kl = [jnp.where(upper, 0.0, qk[h]).astype(BF16) for h in hs]
        a = [a[h] + jnp.where(same_block, _dot_nt(ql[h], kl[h]), 0.0) for h in hs]

    s_t = [s_ref[h] for h in hs]
    qe = [(q[h] * jnp.exp(b[h])).astype(BF16) for h in hs]
    kr = [(k[h] * jnp.exp(b_last[h] - b[h])).astype(BF16) for h in hs]
    o = [_dot_nt(qe[h], s_t[h].astype(BF16)) + _dot(a[h].astype(BF16), v[h]) for h in hs]
    for h in hs:
        s_ref[h] = s_t[h] * jnp.exp(b_last[h]) + _dot_tn(v[h], kr[h])
    for h in hs:
        zg = r_ref[:, h * hv:(h + 1) * hv] + rb_ref[:, h * hv:(h + 1) * hv]
        o_ref[:, h * hv:(h + 1) * hv] = (_rms(o[h]) * ng_ref[...] * (zg * _sigmoid(zg))).astype(BF16)


def _gla_recurrence(proj, g, r_b, norm_g, *, batch, heads, hk, hv, heads_per_step):
    t = proj.shape[0]
    hb = heads_per_step
    chunks = t // (batch * CHUNK)
    dk, dv = heads * hk, heads * hv
    sel = _gla_select_matrices()

    def cols(width, start):
        assert start % (hb * width) == 0
        off = start // (hb * width)
        return pl.BlockSpec((CHUNK, hb * width), lambda b, h, c: (b * chunks + c, h + off))

    return pl.pallas_call(
        functools.partial(_gla_kernel, heads=hb, hk=hk, hv=hv),
        grid=(batch, heads // hb, chunks),
        in_specs=[pl.BlockSpec(sel.shape, lambda b, h, c: (0, 0)),
                  cols(hk, 0), cols(hk, dk), cols(hv, 2 * dk),
                  pl.BlockSpec((CHUNK, hb * hk), lambda b, h, c: (b * chunks + c, h)),
                  cols(hv, 2 * dk + dv),
                  pl.BlockSpec((1, hb * hv), lambda b, h, c: (0, h)),
                  pl.BlockSpec((1, hv), lambda b, h, c: (0, 0))],
        out_specs=pl.BlockSpec((CHUNK, hb * hv), lambda b, h, c: (b * chunks + c, h)),
        out_shape=jax.ShapeDtypeStruct((t, dv), BF16),
        scratch_shapes=[pltpu.VMEM((hb, hv, hk), F32)],
        compiler_params=_cparams("parallel", "parallel", "arbitrary"), name="gla_recurrence",
    )(sel, proj, proj, proj, g, proj, r_b.reshape(1, dv), norm_g.reshape(1, hv))


def _tiles(lp, length):
    return dict(
        lp=lp, length=length,
        ew=_pick(lp, (192, 128, 64)),
        norm=_pick(lp, (352, 192, 128, 64)),
        mm=_pick(lp, (1056, 704, 352, 192, 128, 64)),
        down=_pick(lp, (704, 1056, 352, 192, 128, 64)),
        ffn=_pick(lp, (2112, 1056, 704, 352, 192, 128, 64)),
        rw_groups=32,
        gla_heads=8,
    )


def _col_tile(n, cands=(512, 256, 128)):
    return _pick(n, cands)


def _rwkv_layer(h, g_pre, p, *, batch, tiles):
    d = h.shape[1]
    xr, xk, xv, w_raw, a, gate = _rw_input(h, g_pre, p, lp=tiles["lp"], tm=tiles["ew"])
    mm = functools.partial(_matmul, tm=tiles["mm"], tn=_col_tile(d))
    r = mm(xr, p["wr"], name="rw_r")
    k = mm(xk, p["wk"], name="rw_k")
    v = mm(xv, p["wv"], name="rw_v")
    yo = _rwkv_recurrence(r, k, v, w_raw, a, gate, p["k_k"], p["k_a"], p["r_k"].reshape(-1),
                          p["lnx_g"], p["lnx_b"], batch=batch, groups_per_step=tiles["rw_groups"])
    return mm(yo, p["wo"], out_dtype=BF16, name="rw_o")


def _gla_layer(pre, g, p, *, batch, tiles):
    d = pre.shape[1]
    dk = p["a2"].shape[1]
    dv = p["wo"].shape[0]
    heads = max(4, d // 512)
    hk, hv = dk // heads, dv // heads
    mm = functools.partial(_matmul, tm=tiles["mm"])
    whole = tiles["ffn"] == tiles["lp"] and tiles["length"] % BF16_SUBLANES == 0
    proj = _matmul(pre, p["w_in"], tm=tiles["ffn"] if whole else tiles["mm"], tn=_col_tile(dk),
                   rows=tiles["length"] if whole else None, name="gla_in")
    og = _gla_recurrence(proj, g, p["r_b"], p["norm_g"], batch=batch, heads=heads, hk=hk, hv=hv,
                         heads_per_step=min(tiles["gla_heads"], heads))
    return mm(og, p["wo"], tn=_col_tile(d), out_dtype=BF16, name="gla_o")


def _ffn(pre, w_up, w_gate, w_down, layer, conv_w, conv_b, *, tiles):
    f = w_up.shape[2]
    act, w_down_bf = _ffn_in(pre, w_up, w_gate, w_down, layer, conv_w, conv_b, lp=tiles["lp"],
                             length=tiles["length"], tm=tiles["ffn"], tn=_col_tile(f, (256, 128)))
    return _matmul(act, w_down_bf, tm=tiles["down"], tn=_col_tile(w_down.shape[2], (256, 128)),
                   out_dtype=BF16, name="ffn_down")


def kernel(x, meta, norm_g, rw_mu, rw_w0, rw_w1, rw_w2, rw_a0, rw_a1, rw_a2, rw_g1, rw_g2, rw_k_k, rw_k_a, rw_r_k, rw_wr, rw_wk, rw_wv, rw_wo, rw_lnx_g, rw_lnx_b, gla_w_in, gla_a1, gla_a2, gla_a_b, gla_r_b, gla_norm_g, gla_wo, ffn_up, ffn_gate, ffn_conv, ffn_conv_b, ffn_down):
    batch, seq, d = x.shape
    n_meta = meta.shape[0]
    depth = norm_g.shape[0]
    length = n_meta + seq
    lp = -(-length // CHUNK) * CHUNK
    m = jnp.broadcast_to(meta.astype(x.dtype)[None], (batch, n_meta, d))
    h = jnp.concatenate([m, x, jnp.zeros((batch, lp - length, d), x.dtype)], axis=1).reshape(batch * lp, d)
    tiles = _tiles(lp, length)

    pre = gates = None
    for i in range(depth):
        j = i // 2
        if i % 2 == 0:
            p = dict(mu=rw_mu[j], w0=rw_w0[j], w1=rw_w1[j], w2=rw_w2[j], a0=rw_a0[j], a1=rw_a1[j],
                     a2=rw_a2[j], g1=rw_g1[j], g2=rw_g2[j], k_k=rw_k_k[j], k_a=rw_k_a[j], r_k=rw_r_k[j],
                     wr=rw_wr[j], wk=rw_wk[j], wv=rw_wv[j], wo=rw_wo[j], lnx_g=rw_lnx_g[j],
                     lnx_b=rw_lnx_b[j])
            mix = _rwkv_layer(h, norm_g[i, 0], p, batch=batch, tiles=tiles)
        else:
            p = dict(w_in=gla_w_in[j], a1=gla_a1[j], a2=gla_a2[j], a_b=gla_a_b[j], r_b=gla_r_b[j],
                     norm_g=gla_norm_g[j], wo=gla_wo[j])
            mix = _gla_layer(pre, gates, p, batch=batch, tiles=tiles)
        h, pre = _post_norm(h, mix, norm_g[i, 1], norm_g[i, 2], tm=tiles["norm"])
        f = _ffn(pre, ffn_up, ffn_gate, ffn_down, i, ffn_conv[i], ffn_conv_b[i], tiles=tiles)
        if i + 1 < depth and (i + 1) % 2 == 1:
            jn = (i + 1) // 2
            h, pre, gates = _post_norm(h, f, norm_g[i, 3], norm_g[i + 1, 0], tm=tiles["norm"],
                                       gate=(gla_a1[jn], gla_a2[jn], gla_a_b[jn]))
        elif i + 1 < depth:
            h, pre = _post_norm(h, f, norm_g[i, 3], norm_g[i + 1, 0], tm=tiles["norm"])
    return _post_norm_final(h, f, norm_g[depth - 1, 3], batch=batch, lp=lp, start=n_meta, seq=seq,
                            tm=_pick(seq, (256, 128, 64)))
```

```python
import functools

import numpy as np
import jax
import jax.numpy as jnp
from jax import lax
from jax.experimental import pallas as pl
from jax.experimental.pallas import tpu as pltpu

F32 = jnp.float32
BF16 = jnp.bfloat16

NORM_EPS = 1e-6
RW_LNX_EPS = 64e-5
RW_HEAD = 64
RW_HEAD_SHIFT = RW_HEAD.bit_length() - 1
INV_BLOCK_SHIFT = 4
DECAY_SCALE = float(np.exp(-0.5))
GLA_GATE_TAU = 16.0
CHUNK = 64
LANES = 128
BF16_SUBLANES = 16
GROUP = LANES
HEADS_PER_GROUP = GROUP // RW_HEAD
VMEM_LIMIT = 56 * 1024 * 1024


def _cparams(*sem):
    return pltpu.CompilerParams(dimension_semantics=sem, vmem_limit_bytes=VMEM_LIMIT)


def _pick(n, cands):
    for c in cands:
        if n % c == 0:
            return c
    raise ValueError(f"no tile for {n} in {cands}")


def _rms(x):
    return x * lax.rsqrt(jnp.mean(x * x, axis=-1, keepdims=True) + NORM_EPS)


def _sigmoid(z):
    return 1.0 / (1.0 + jnp.exp(-z))


def _dot(a, b):
    return jnp.dot(a, b, preferred_element_type=F32)


def _dot_nt(a, b):
    return lax.dot_general(a, b, (((1,), (1,)), ((), ())), preferred_element_type=F32)


def _dot_tn(a, b):
    return lax.dot_general(a, b, (((0,), (0,)), ((), ())), preferred_element_type=F32)


def _split(x):
    hi = x.astype(BF16)
    lo = (x - hi.astype(F32)).astype(BF16)
    return hi, lo


def _mm_kernel(x_ref, w_ref, o_ref, *, rows):
    w = w_ref[...].astype(BF16)
    o_ref[:rows, :] = _dot(x_ref[:rows, :], w).astype(o_ref.dtype)
    if rows < o_ref.shape[0]:
        o_ref[rows:, :] = jnp.zeros((o_ref.shape[0] - rows, o_ref.shape[1]), o_ref.dtype)


def _matmul(x, w, *, tm, tn, rows=None, out_dtype=F32, name):
    t, k = x.shape
    n = w.shape[1]
    x_mode = {} if rows is None else dict(pipeline_mode=pl.Buffered(1))
    return pl.pallas_call(
        functools.partial(_mm_kernel, rows=tm if rows is None else rows), grid=(t // tm, n // tn),
        in_specs=[pl.BlockSpec((tm, k), lambda i, j: (i, 0), **x_mode),
                  pl.BlockSpec((k, tn), lambda i, j: (0, j))],
        out_specs=pl.BlockSpec((tm, tn), lambda i, j: (i, j)),
        out_shape=jax.ShapeDtypeStruct((t, n), out_dtype),
        compiler_params=_cparams("parallel", "arbitrary"), name=name,
    )(x, w)


def _pad_rank(w1, w2):
    r = w1.shape[1]
    rp = -(-r // LANES) * LANES
    return (jnp.pad(w1, ((0, 0), (0, rp - r))).astype(BF16),
            jnp.pad(w2, ((0, rp - r), (0, 0))).astype(BF16))


def _rw_input_kernel(h_ref, halo_ref, g_ref, mu_ref, w1_ref, w2_ref, a1_ref, a2_ref, g1_ref, g2_ref,
                     w0_ref, a0_ref, xr_ref, xk_ref, xv_ref, wraw_ref, a_ref, gate_ref, *, blocks_per_seq):
    i = pl.program_id(0)
    g = g_ref[...]
    pre = _rms(h_ref[...]) * g
    halo = (_rms(halo_ref[...]) * g)[7:8, :]
    halo = jnp.where(i % blocks_per_seq == 0, 0.0, halo)
    row = lax.broadcasted_iota(jnp.int32, pre.shape, 0)
    prev = jnp.where(row == 0, halo, pltpu.roll(pre, 1, 0))
    xx = prev - pre
    mix = lambda m: (pre + xx * mu_ref[m:m + 1, :]).astype(BF16)
    xr_ref[...] = mix(0)
    xk_ref[...] = mix(2)
    xv_ref[...] = mix(3)
    wraw_ref[...] = w0_ref[...] + _dot(jnp.tanh(_dot(mix(1), w1_ref[...])).astype(BF16), w2_ref[...])
    a_ref[...] = _sigmoid(a0_ref[...] + _dot(_dot(mix(4), a1_ref[...]).astype(BF16), a2_ref[...]))
    gate_ref[...] = _dot(_sigmoid(_dot(mix(5), g1_ref[...])).astype(BF16), g2_ref[...])


def _rw_input(h, g, p, *, lp, tm):
    t, d = h.shape
    row_spec = pl.BlockSpec((tm, d), lambda i: (i, 0))
    const = lambda arr: pl.BlockSpec(arr.shape, lambda i: (0, 0), pipeline_mode=pl.Buffered(1))
    lora_w = [w for pair in (("w1", "w2"), ("a1", "a2"), ("g1", "g2")) for w in _pad_rank(p[pair[0]], p[pair[1]])]
    vec = lambda a: a.reshape(1, d)
    consts = [vec(g), p["mu"], *lora_w, vec(p["w0"]), vec(p["a0"])]
    return pl.pallas_call(
        functools.partial(_rw_input_kernel, blocks_per_seq=lp // tm),
        grid=(t // tm,),
        in_specs=[row_spec, pl.BlockSpec((8, d), lambda i: (jnp.maximum(i * (tm // 8) - 1, 0), 0))]
                 + [const(c) for c in consts],
        out_specs=[row_spec] * 6,
        out_shape=[jax.ShapeDtypeStruct((t, d), BF16)] * 3 + [jax.ShapeDtypeStruct((t, d), F32)] * 3,
        compiler_params=_cparams("parallel"), name="rw_input",
    )(h, h, *consts)


def _rwkv_kernel(r_ref, k_ref, v_ref, w_ref, a_ref, gate_ref, kk_ref, ka_ref, rk_ref, lg_ref, lb_ref,
                 o_ref, s_ref, *, groups):
    c_id = pl.program_id(2)

    @pl.when(c_id == 0)
    def _():
        s_ref[...] = jnp.zeros_like(s_ref)

    C, W = CHUNK, GROUP
    row = lax.broadcasted_iota(jnp.int32, (C, W), 0)
    lane = lax.broadcasted_iota(jnp.int32, (C, W), 1)
    lane_in = lane & (RW_HEAD - 1)
    lane_head = lane >> RW_HEAD_SHIFT
    tril_incl = lane_in <= row
    tril_strict = lane_in < row
    eye = jnp.where(lane_in == row, 1.0, 0.0)
    diag_block = (lane_in >> INV_BLOCK_SHIFT) == (row >> INV_BLOCK_SHIFT)
    bd_mask = (lax.broadcasted_iota(jnp.int32, (HEADS_PER_GROUP * C, W), 0) >> RW_HEAD_SHIFT) == \
              (lax.broadcasted_iota(jnp.int32, (HEADS_PER_GROUP * C, W), 1) >> RW_HEAD_SHIFT)
    ones_bd = jnp.where(bd_mask, 1.0, 0.0).astype(BF16)
    ones_bd2 = jnp.concatenate([ones_bd, ones_bd], axis=0)
    ltri = jnp.where(lax.broadcasted_iota(jnp.int32, (C, C), 1) <=
                     lax.broadcasted_iota(jnp.int32, (C, C), 0), 1.0, 0.0).astype(BF16)
    gs = range(groups)
    cols = lambda ref, g: ref[:, g * W:(g + 1) * W]

    def bd(x):
        return jnp.concatenate([x] * HEADS_PER_GROUP, axis=0) * ones_bd

    def seg_sum(x):
        return _dot(jnp.concatenate(_split(x), axis=1), ones_bd2)

    def head_mm(lhs, ys):
        top = jnp.concatenate([bd(ys[0]), bd(ys[1])], axis=1)
        res = _dot(jnp.concatenate([jnp.concatenate(x, axis=1) for x in lhs], axis=0),
                   jnp.concatenate([top, top], axis=0))
        return [res[i * C:(i + 1) * C, :W] + res[i * C:(i + 1) * C, W:] for i in range(len(lhs))]

    k = [cols(k_ref, g) for g in gs]
    v = [cols(v_ref, g).astype(BF16) for g in gs]
    a = [cols(a_ref, g) for g in gs]
    ld = [-DECAY_SCALE * _sigmoid(cols(w_ref, g)) for g in gs]
    kk = [k[g] * cols(kk_ref, g) for g in gs]
    kk = [kk[g] * lax.rsqrt(jnp.maximum(seg_sum(kk[g] * kk[g]), 1e-24)) for g in gs]
    kh = [k[g] * (1.0 + (a[g] - 1.0) * cols(ka_ref, g)) for g in gs]
    rb = [kk[g] * a[g] for g in gs]

    cum2 = [_dot(ltri, jnp.concatenate(_split(ld[g]), axis=1)) for g in gs]
    cum = [cum2[g][:, :W] + cum2[g][:, W:] for g in gs]
    c_last = [cum[g][C - 1:C, :] for g in gs]
    e_nc = [jnp.exp(-cum[g]) for g in gs]
    lhs = [jnp.concatenate([-kk[g] * jnp.exp(cum[g] - ld[g]), cols(r_ref, g) * jnp.exp(cum[g])],
                           axis=0).astype(BF16) for g in gs]
    b_hat = [(rb[g] * e_nc[g]).astype(BF16) for g in gs]
    k_hat = [(kh[g] * e_nc[g]).astype(BF16) for g in gs]
    gram = [_dot_nt(lhs[g], jnp.concatenate([bd(b_hat[g]), bd(k_hat[g])], axis=0)) for g in gs]
    n_ab = [jnp.where(tril_strict, gram[g][:C, :W], 0.0) for g in gs]
    b_rb = [jnp.where(tril_incl, gram[g][C:, :W], 0.0).astype(BF16) for g in gs]
    ak_rk = [jnp.concatenate([jnp.where(tril_strict, gram[g][:C, W:], 0.0),
                              jnp.where(tril_incl, gram[g][C:, W:], 0.0)], axis=0).astype(BF16) for g in gs]

    assert CHUNK == 4 << INV_BLOCK_SHIFT and INV_BLOCK_SHIFT == 4
    nd = [jnp.where(diag_block, n_ab[g], 0.0) for g in gs]
    no_s = [_split(n_ab[g] - nd[g]) for g in gs]
    pd = [eye + nd[g] for g in gs]
    x_s = [_split(nd[g]) for g in gs]
    x_s = [_split(head_mm([x_s[g]], x_s[g])[0]) for g in gs]
    for _ in range(2):
        prod = [head_mm([x_s[g], _split(pd[g])], x_s[g]) for g in gs]
        pd = [pd[g] + prod[g][1] for g in gs]
        x_s = [_split(prod[g][0]) for g in gs]
    pd = [pd[g] + head_mm([_split(pd[g])], x_s[g])[0] for g in gs]
    pd_s = [_split(pd[g]) for g in gs]
    m = [head_mm([pd_s[g]], no_s[g])[0] for g in gs]
    m_s = [_split(m[g]) for g in gs]
    m2_s = [_split(head_mm([m_s[g]], m_s[g])[0]) for g in gs]
    q = [eye + m[g] for g in gs]
    q_s = [_split(q[g] + head_mm([_split(q[g])], m2_s[g])[0]) for g in gs]
    p_s = [_split(head_mm([q_s[g]], pd_s[g])[0]) for g in gs]

    s_t = [s_ref[g] for g in gs]
    zz = [_dot(jnp.concatenate([lhs[g], ak_rk[g]], axis=1),
               jnp.concatenate([bd(s_t[g].astype(BF16)), bd(v[g])], axis=0)) for g in gs]
    sa = [head_mm([p_s[g]], _split(zz[g][:C]))[0].astype(BF16) for g in gs]
    y = [zz[g][C:] + _dot(b_rb[g], bd(sa[g])) for g in gs]

    yc = [y[g] - seg_sum(y[g]) * (1.0 / RW_HEAD) for g in gs]
    var = [seg_sum(yc[g] * yc[g]) * (1.0 / RW_HEAD) for g in gs]
    kh = [cols(k_ref, g) * (1.0 + (cols(a_ref, g) - 1.0) * cols(ka_ref, g)) for g in gs]
    bonus = [seg_sum(cols(r_ref, g) * kh[g] * cols(rk_ref, g)) * cols(v_ref, g) for g in gs]
    for g in gs:
        yn = yc[g] * lax.rsqrt(var[g] + RW_LNX_EPS) * cols(lg_ref, g) + cols(lb_ref, g)
        o_ref[:, g * W:(g + 1) * W] = ((yn + bonus[g]) * cols(gate_ref, g)).astype(BF16)

    full = [_dot_tn(jnp.concatenate([b_hat[g], k_hat[g]], axis=0), jnp.concatenate([sa[g], v[g]], axis=0))
            for g in gs]
    c_col = [seg_sum(eye * c_last[g]) for g in gs]
    for g in gs:
        gain = s_t[g]
        for h in range(HEADS_PER_GROUP):
            gain = gain + jnp.where(lane_head == h, full[g][h * C:(h + 1) * C, :], 0.0)
        s_ref[g] = jnp.exp(c_col[g]) * gain


def _rwkv_recurrence(r, k, v, w_raw, a, gate, k_k, k_a, r_k, lnx_g, lnx_b, *, batch, groups_per_step):
    t, d = r.shape
    chunks = t // (batch * CHUNK)
    gb = min(groups_per_step, d // GROUP)
    wb = gb * GROUP
    seq_spec = pl.BlockSpec((CHUNK, wb), lambda b, g, c: (b * chunks + c, g))
    par_spec = pl.BlockSpec((1, wb), lambda b, g, c: (0, g))
    par = lambda p: p.reshape(1, d)
    return pl.pallas_call(
        functools.partial(_rwkv_kernel, groups=gb),
        grid=(batch, d // wb, chunks),
        in_specs=[seq_spec] * 6 + [par_spec] * 5,
        out_specs=seq_spec,
        out_shape=jax.ShapeDtypeStruct((t, d), BF16),
        scratch_shapes=[pltpu.VMEM((gb, CHUNK, GROUP), F32)],
        compiler_params=_cparams("parallel", "parallel", "arbitrary"), name="rwkv_recurrence",
    )(r, k, v, w_raw, a, gate, par(k_k), par(k_a), par(r_k), par(lnx_g), par(lnx_b))


def _log_gate(x):
    return (jnp.minimum(x, 0.0) - jnp.log(1.0 + jnp.exp(-jnp.abs(x)))) * (1.0 / GLA_GATE_TAU)


def _post_norm_kernel(h_ref, m_ref, gp_ref, *rest):
    h = h_ref[...] + _rms(m_ref[...].astype(F32)) * gp_ref[...]
    if len(rest) == 1:
        rest[0][...] = h
        return
    gn_ref, *gate_refs, h_out, p_out = rest if len(rest) == 3 else rest[:-1]
    pre = (_rms(h) * gn_ref[...]).astype(BF16)
    h_out[...] = h
    p_out[...] = pre
    if gate_refs:
        a1_ref, a2_ref, ab_ref = gate_refs
        rest[-1][...] = _log_gate(_dot(_dot(pre, a1_ref[...]).astype(BF16), a2_ref[...]) + ab_ref[...])


def _post_norm_final(h, m, g_post, *, batch, lp, start, seq, tm):
    d = h.shape[1]
    align = BF16_SUBLANES
    assert lp % align == 0 and start % align == 0 and tm % align == 0
    in_spec = pl.BlockSpec((pl.Element(tm), pl.Element(d)),
                           lambda b, i: (pl.multiple_of(b * lp + start + i * tm, align), 0))
    return pl.pallas_call(
        _post_norm_kernel, grid=(batch, seq // tm),
        in_specs=[in_spec, in_spec, pl.BlockSpec((1, d), lambda b, i: (0, 0))],
        out_specs=pl.BlockSpec((None, tm, d), lambda b, i: (b, i, 0)),
        out_shape=jax.ShapeDtypeStruct((batch, seq, d), F32),
        compiler_params=_cparams("parallel", "parallel"), name="post_norm_final",
    )(h, m, g_post.reshape(1, d))


def _post_norm(h, m, g_post, g_next, *, tm, gate=None):
    t, d = h.shape
    row_spec = pl.BlockSpec((tm, d), lambda i: (i, 0))
    par_spec = pl.BlockSpec((1, d), lambda i: (0, 0))
    in_specs = [row_spec, row_spec, par_spec, par_spec]
    out_specs = [row_spec, row_spec]
    out_shape = [jax.ShapeDtypeStruct((t, d), F32), jax.ShapeDtypeStruct((t, d), BF16)]
    args = [h, m, g_post.reshape(1, d), g_next.reshape(1, d)]
    if gate is not None:
        a1, a2 = _pad_rank(gate[0], gate[1])
        dk = a2.shape[1]
        args += [a1, a2, gate[2].reshape(1, dk)]
        in_specs += [pl.BlockSpec(a.shape, lambda i: (0, 0)) for a in args[4:]]
        out_specs.append(pl.BlockSpec((tm, dk), lambda i: (i, 0)))
        out_shape.append(jax.ShapeDtypeStruct((t, dk), F32))
    return pl.pallas_call(
        _post_norm_kernel, grid=(t // tm,),
        in_specs=in_specs, out_specs=out_specs, out_shape=out_shape,
        compiler_params=_cparams("parallel"), name="post_norm",
    )(*args)


def _ffn_in_kernel(x_ref, halo_ref, wu_ref, wg_ref, cw_ref, cb_ref, wd_ref, o_ref, wd_out, *,
                   blocks_per_seq, rows, halo_rows):
    i = pl.program_id(0)
    tm = x_ref.shape[0]
    wd_out[...] = wd_ref[...].astype(BF16)
    wu = wu_ref[...].astype(BF16)
    wg = wg_ref[...].astype(BF16)
    zh = _dot(halo_ref[...], wg)
    zh = jnp.where(i % blocks_per_seq == 0, 0.0, zh)
    sub = _pick(rows, (352, 688, 96, 48, 16))
    prev1, prev2 = zh[halo_rows - 1:halo_rows, :], zh[halo_rows - 2:halo_rows - 1, :]
    row = lax.broadcasted_iota(jnp.int32, (sub, wu.shape[1]), 0)

    def project(s):
        xs = x_ref[s * sub:(s + 1) * sub, :]
        return _dot(xs, wu), _dot(xs, wg)

    cur = project(0)
    for s in range(rows // sub):
        nxt = project(s + 1) if (s + 1) * sub < rows else None
        u, z = cur
        z1 = jnp.where(row == 0, prev1, pltpu.roll(z, 1, 0))
        z2 = pltpu.roll(z, 2, 0)
        z2 = jnp.where(row == 1, prev1, z2)
        z2 = jnp.where(row == 0, prev2, z2)
        zc = z * cw_ref[2:3, :] + z1 * cw_ref[1:2, :] + z2 * cw_ref[0:1, :] + cb_ref[...]
        o_ref[s * sub:(s + 1) * sub, :] = (zc * _sigmoid(zc) * u).astype(BF16)
        prev1, prev2 = z[sub - 1:sub, :], z[sub - 2:sub - 1, :]
        cur = nxt
    if rows < tm:
        o_ref[rows:, :] = jnp.zeros((tm - rows, o_ref.shape[1]), o_ref.dtype)


def _ffn_in(x, w_up, w_gate, w_down, layer, conv_w, conv_b, *, lp, length, tm, tn):
    t, d = x.shape
    f = w_up.shape[2]
    hr = BF16_SUBLANES
    ni, nj = t // tm, f // tn
    slab = f // (ni * nj)
    assert slab * ni * nj == f and slab % hr == 0
    w_spec = pl.BlockSpec((None, d, tn), lambda i, j: (layer, 0, j))
    return pl.pallas_call(
        functools.partial(_ffn_in_kernel, blocks_per_seq=lp // tm, halo_rows=hr,
                          rows=length if tm == lp and length % hr == 0 else tm),
        grid=(ni, nj),
        in_specs=[pl.BlockSpec((tm, d), lambda i, j: (i, 0), pipeline_mode=pl.Buffered(1)),
                  pl.BlockSpec((hr, d), lambda i, j: (jnp.maximum(i * (tm // hr) - 1, 0), 0)),
                  w_spec, w_spec,
                  pl.BlockSpec((conv_w.shape[0], tn), lambda i, j: (0, j)),
                  pl.BlockSpec((1, tn), lambda i, j: (0, j)),
                  pl.BlockSpec((None, slab, d), lambda i, j: (layer, i * nj + j, 0))],
        out_specs=[pl.BlockSpec((tm, tn), lambda i, j: (i, j)),
                   pl.BlockSpec((slab, d), lambda i, j: (i * nj + j, 0))],
        out_shape=[jax.ShapeDtypeStruct((t, f), BF16), jax.ShapeDtypeStruct((f, d), BF16)],
        compiler_params=_cparams("parallel", "arbitrary"), name="ffn_in",
    )(x, x, w_up, w_gate, conv_w, conv_b.reshape(1, f), w_down)


GLA_LEVELS = 6


def _gla_select_matrices():
    t_i = np.arange(CHUNK)[:, None]
    u_i = np.arange(CHUNK)[None, :]
    mats = [u_i <= t_i]
    for l in range(GLA_LEVELS):
        half = 1 << l
        p = (t_i & ~(2 * half - 1)) + (half - 1)
        mats.append(np.where((t_i & half) != 0, (u_i > p) & (u_i <= t_i), (u_i > t_i) & (u_i <= p)))
    sel = np.concatenate(mats, axis=0).astype(np.float32)
    return jnp.asarray(np.concatenate([sel, sel], axis=1), dtype=BF16)


def _gla_kernel(sel_ref, q_ref, k_ref, v_ref, g_ref, r_ref, rb_ref, ng_ref, o_ref, s_ref, *, heads, hk, hv):
    c_id = pl.program_id(2)

    @pl.when(c_id == 0)
    def _():
        s_ref[...] = jnp.zeros_like(s_ref)

    C = CHUNK
    t_i = lax.broadcasted_iota(jnp.int32, (C, C), 0)
    u_i = lax.broadcasted_iota(jnp.int32, (C, C), 1)
    row = lax.broadcasted_iota(jnp.int32, (C, 1), 0)
    eye = t_i == u_i
    hs = range(heads)
    sel = sel_ref[...]

    q = [q_ref[:, h * hk:(h + 1) * hk] * (hk ** -0.5) for h in hs]
    k = [k_ref[:, h * hk:(h + 1) * hk] for h in hs]
    v = [v_ref[:, h * hv:(h + 1) * hv].astype(BF16) for h in hs]
    g_s = [_split(g_ref[:, h * hk:(h + 1) * hk]) for h in hs]
    cums = [_dot(sel, jnp.concatenate(g_s[h], axis=0)) for h in hs]
    b = [cums[h][:C] for h in hs]
    b_last = [b[h][C - 1:C, :] for h in hs]

    a = [jnp.where(eye, jnp.sum(q[h] * k[h], axis=-1, keepdims=True), 0.0) for h in hs]
    for l in range(GLA_LEVELS):
        upper = (row & (1 << l)) != 0
        same_block = (t_i >> (l + 1)) == (u_i >> (l + 1))
        qk = [jnp.where(upper, q[h], k[h]) * jnp.exp(cums[h][(1 + l) * C:(2 + l) * C]) for h in hs]
        ql = [jnp.where(upper, qk[h], 0.0).astype(BF16) for h in hs]
        kl = [jnp.where(upper, 0.0, qk[h]).astype(BF16) for h in hs]
        a = [a[h] + jnp.where(same_block, _dot_nt(ql[h], kl[h]), 0.0) for h in hs]

    s_t = [s_ref[h] for h in hs]
    qe = [(q[h] * jnp.exp(b[h])).astype(BF16) for h in hs]
    kr = [(k[h] * jnp.exp(b_last[h] - b[h])).astype(BF16) for h in hs]
    o = [_dot_nt(qe[h], s_t[h].astype(BF16)) + _dot(a[h].astype(BF16), v[h]) for h in hs]
    for h in hs:
        s_ref[h] = s_t[h] * jnp.exp(b_last[h]) + _dot_tn(v[h], kr[h])
    for h in hs:
        zg = r_ref[:, h * hv:(h + 1) * hv] + rb_ref[:, h * hv:(h + 1) * hv]
        o_ref[:, h * hv:(h + 1) * hv] = (_rms(o[h]) * ng_ref[...] * (zg * _sigmoid(zg))).astype(BF16)


def _gla_recurrence(proj, g, r_b, norm_g, *, batch, heads, hk, hv, heads_per_step):
    t = proj.shape[0]
    hb = heads_per_step
    chunks = t // (batch * CHUNK)
    dk, dv = heads * hk, heads * hv
    sel = _gla_select_matrices()

    def cols(width, start):
        assert start % (hb * width) == 0
        off = start // (hb * width)
        return pl.BlockSpec((CHUNK, hb * width), lambda b, h, c: (b * chunks + c, h + off))

    return pl.pallas_call(
        functools.partial(_gla_kernel, heads=hb, hk=hk, hv=hv),
        grid=(batch, heads // hb, chunks),
        in_specs=[pl.BlockSpec(sel.shape, lambda b, h, c: (0, 0)),
                  cols(hk, 0), cols(hk, dk), cols(hv, 2 * dk),
                  pl.BlockSpec((CHUNK, hb * hk), lambda b, h, c: (b * chunks + c, h)),
                  cols(hv, 2 * dk + dv),
                  pl.BlockSpec((1, hb * hv), lambda b, h, c: (0, h)),
                  pl.BlockSpec((1, hv), lambda b, h, c: (0, 0))],
        out_specs=pl.BlockSpec((CHUNK, hb * hv), lambda b, h, c: (b * chunks + c, h)),
        out_shape=jax.ShapeDtypeStruct((t, dv), BF16),
        scratch_shapes=[pltpu.VMEM((hb, hv, hk), F32)],
        compiler_params=_cparams("parallel", "parallel", "arbitrary"), name="gla_recurrence",
    )(sel, proj, proj, proj, g, proj, r_b.reshape(1, dv), norm_g.reshape(1, hv))


def _tiles(lp, length):
    return dict(
        lp=lp, length=length,
        ew=_pick(lp, (192, 128, 64)),
        norm=_pick(lp, (352, 192, 128, 64)),
        mm=_pick(lp, (1056, 704, 352, 192, 128, 64)),
        down=_pick(lp, (704, 1056, 352, 192, 128, 64)),
        ffn=_pick(lp, (2112, 1056, 704, 352, 192, 128, 64)),
        rw_groups=32,
        gla_heads=8,
    )


def _col_tile(n, cands=(512, 256, 128)):
    return _pick(n, cands)


def _rwkv_layer(h, g_pre, p, *, batch, tiles):
    d = h.shape[1]
    xr, xk, xv, w_raw, a, gate = _rw_input(h, g_pre, p, lp=tiles["lp"], tm=tiles["ew"])
    mm = functools.partial(_matmul, tm=tiles["mm"], tn=_col_tile(d))
    r = mm(xr, p["wr"], name="rw_r")
    k = mm(xk, p["wk"], name="rw_k")
    v = mm(xv, p["wv"], name="rw_v")
    yo = _rwkv_recurrence(r, k, v, w_raw, a, gate, p["k_k"], p["k_a"], p["r_k"].reshape(-1),
                          p["lnx_g"], p["lnx_b"], batch=batch, groups_per_step=tiles["rw_groups"])
    return mm(yo, p["wo"], out_dtype=BF16, name="rw_o")


def _gla_layer(pre, g, p, *, batch, tiles):
    d = pre.shape[1]
    dk = p["a2"].shape[1]
    dv = p["wo"].shape[0]
    heads = max(4, d // 512)
    hk, hv = dk // heads, dv // heads
    mm = functools.partial(_matmul, tm=tiles["mm"])
    whole = tiles["ffn"] == tiles["lp"] and tiles["length"] % BF16_SUBLANES == 0
    proj = _matmul(pre, p["w_in"], tm=tiles["ffn"] if whole else tiles["mm"], tn=_col_tile(dk),
                   rows=tiles["length"] if whole else None, name="gla_in")
    og = _gla_recurrence(proj, g, p["r_b"], p["norm_g"], batch=batch, heads=heads, hk=hk, hv=hv,
                         heads_per_step=min(tiles["gla_heads"], heads))
    return mm(og, p["wo"], tn=_col_tile(d), out_dtype=BF16, name="gla_o")


def _ffn(pre, w_up, w_gate, w_down, layer, conv_w, conv_b, *, tiles):
    f = w_up.shape[2]
    act, w_down_bf = _ffn_in(pre, w_up, w_gate, w_down, layer, conv_w, conv_b, lp=tiles["lp"],
                             length=tiles["length"], tm=tiles["ffn"], tn=_col_tile(f, (256, 128)))
    return _matmul(act, w_down_bf, tm=tiles["down"], tn=_col_tile(w_down.shape[2], (512, 256, 128)),
                   out_dtype=BF16, name="ffn_down")


def kernel(x, meta, norm_g, rw_mu, rw_w0, rw_w1, rw_w2, rw_a0, rw_a1, rw_a2, rw_g1, rw_g2, rw_k_k, rw_k_a, rw_r_k, rw_wr, rw_wk, rw_wv, rw_wo, rw_lnx_g, rw_lnx_b, gla_w_in, gla_a1, gla_a2, gla_a_b, gla_r_b, gla_norm_g, gla_wo, ffn_up, ffn_gate, ffn_conv, ffn_conv_b, ffn_down):
    batch, seq, d = x.shape
    n_meta = meta.shape[0]
    depth = norm_g.shape[0]
    length = n_meta + seq
    lp = -(-length // CHUNK) * CHUNK
    m = jnp.broadcast_to(meta.astype(x.dtype)[None], (batch, n_meta, d))
    h = jnp.concatenate([m, x, jnp.zeros((batch, lp - length, d), x.dtype)], axis=1).reshape(batch * lp, d)
    tiles = _tiles(lp, length)

    pre = gates = None
    for i in range(depth):
        j = i // 2
        if i % 2 == 0:
            p = dict(mu=rw_mu[j], w0=rw_w0[j], w1=rw_w1[j], w2=rw_w2[j], a0=rw_a0[j], a1=rw_a1[j],
                     a2=rw_a2[j], g1=rw_g1[j], g2=rw_g2[j], k_k=rw_k_k[j], k_a=rw_k_a[j], r_k=rw_r_k[j],
                     wr=rw_wr[j], wk=rw_wk[j], wv=rw_wv[j], wo=rw_wo[j], lnx_g=rw_lnx_g[j],
                     lnx_b=rw_lnx_b[j])
            mix = _rwkv_layer(h, norm_g[i, 0], p, batch=batch, tiles=tiles)
        else:
            p = dict(w_in=gla_w_in[j], a1=gla_a1[j], a2=gla_a2[j], a_b=gla_a_b[j], r_b=gla_r_b[j],
                     norm_g=gla_norm_g[j], wo=gla_wo[j])
            mix = _gla_layer(pre, gates, p, batch=batch, tiles=tiles)
        h, pre = _post_norm(h, mix, norm_g[i, 1], norm_g[i, 2], tm=tiles["norm"])
        f = _ffn(pre, ffn_up, ffn_gate, ffn_down, i, ffn_conv[i], ffn_conv_b[i], tiles=tiles)
        if i + 1 < depth and (i + 1) % 2 == 1:
            jn = (i + 1) // 2
            h, pre, gates = _post_norm(h, f, norm_g[i, 3], norm_g[i + 1, 0], tm=tiles["norm"],
                                       gate=(gla_a1[jn], gla_a2[jn], gla_a_b[jn]))
        elif i + 1 < depth:
            h, pre = _post_norm(h, f, norm_g[i, 3], norm_g[i + 1, 0], tm=tiles["norm"])
    return _post_norm_final(h, f, norm_g[depth - 1, 3], batch=batch, lp=lp, start=n_meta, seq=seq,
                            tm=_pick(seq, (256, 128, 64)))
```

```python
import functools

import numpy as np
import jax
import jax.numpy as jnp
from jax import lax
from jax.experimental import pallas as pl
from jax.experimental.pallas import tpu as pltpu

F32 = jnp.float32
BF16 = jnp.bfloat16

NORM_EPS = 1e-6
RW_LNX_EPS = 64e-5
RW_HEAD = 64
RW_HEAD_SHIFT = RW_HEAD.bit_length() - 1
INV_BLOCK_SHIFT = 4
DECAY_SCALE = float(np.exp(-0.5))
GLA_GATE_TAU = 16.0
CHUNK = 64
LANES = 128
BF16_SUBLANES = 16
GROUP = LANES
HEADS_PER_GROUP = GROUP // RW_HEAD
VMEM_LIMIT = 56 * 1024 * 1024


def _cparams(*sem):
    return pltpu.CompilerParams(dimension_semantics=sem, vmem_limit_bytes=VMEM_LIMIT)


def _pick(n, cands):
    for c in cands:
        if n % c == 0:
            return c
    raise ValueError(f"no tile for {n} in {cands}")


def _rms(x):
    return x * lax.rsqrt(jnp.mean(x * x, axis=-1, keepdims=True) + NORM_EPS)


def _sigmoid(z):
    return 1.0 / (1.0 + jnp.exp(-z))


def _dot(a, b):
    return jnp.dot(a, b, preferred_element_type=F32)


def _dot_nt(a, b):
    return lax.dot_general(a, b, (((1,), (1,)), ((), ())), preferred_element_type=F32)


def _dot_tn(a, b):
    return lax.dot_general(a, b, (((0,), (0,)), ((), ())), preferred_element_type=F32)


def _split(x):
    hi = x.astype(BF16)
    lo = (x - hi.astype(F32)).astype(BF16)
    return hi, lo


def _mm_kernel(x_ref, w_ref, o_ref, *, rows):
    w = w_ref[...].astype(BF16)
    o_ref[:rows, :] = _dot(x_ref[:rows, :], w).astype(o_ref.dtype)
    if rows < o_ref.shape[0]:
        o_ref[rows:, :] = jnp.zeros((o_ref.shape[0] - rows, o_ref.shape[1]), o_ref.dtype)


def _matmul(x, w, *, tm, tn, rows=None, out_dtype=F32, name):
    t, k = x.shape
    n = w.shape[1]
    x_mode = {} if rows is None else dict(pipeline_mode=pl.Buffered(1))
    return pl.pallas_call(
        functools.partial(_mm_kernel, rows=tm if rows is None else rows), grid=(t // tm, n // tn),
        in_specs=[pl.BlockSpec((tm, k), lambda i, j: (i, 0), **x_mode),
                  pl.BlockSpec((k, tn), lambda i, j: (0, j))],
        out_specs=pl.BlockSpec((tm, tn), lambda i, j: (i, j)),
        out_shape=jax.ShapeDtypeStruct((t, n), out_dtype),
        compiler_params=_cparams("parallel", "arbitrary"), name=name,
    )(x, w)


def _mm_ws_kernel(x_ref, w_ref, o_ref, wbf_ref):
    @pl.when(pl.program_id(1) == 0)
    def _():
        wbf_ref[...] = w_ref[...].astype(BF16)
    o_ref[...] = _dot(x_ref[...], wbf_ref[...]).astype(o_ref.dtype)


def _matmul_ws(x, w, *, tm, tn, out_dtype=F32, name):
    t, k = x.shape
    n = w.shape[1]
    return pl.pallas_call(
        _mm_ws_kernel, grid=(n // tn, t // tm),
        in_specs=[pl.BlockSpec((tm, k), lambda j, i: (i, 0)),
                  pl.BlockSpec((k, tn), lambda j, i: (0, j))],
        out_specs=pl.BlockSpec((tm, tn), lambda j, i: (i, j)),
        out_shape=jax.ShapeDtypeStruct((t, n), out_dtype),
        scratch_shapes=[pltpu.VMEM((k, tn), BF16)],
        compiler_params=_cparams("parallel", "arbitrary"), name=name,
    )(x, w)


def _pad_rank(w1, w2):
    r = w1.shape[1]
    rp = -(-r // LANES) * LANES
    return (jnp.pad(w1, ((0, 0), (0, rp - r))).astype(BF16),
            jnp.pad(w2, ((0, rp - r), (0, 0))).astype(BF16))


def _rw_input_kernel(h_ref, halo_ref, g_ref, mu_ref, w1_ref, w2_ref, a1_ref, a2_ref, g1_ref, g2_ref,
                     w0_ref, a0_ref, xr_ref, xk_ref, xv_ref, wraw_ref, a_ref, gate_ref, *, blocks_per_seq):
    i = pl.program_id(0)
    g = g_ref[...]
    pre = _rms(h_ref[...]) * g
    halo = (_rms(halo_ref[...]) * g)[7:8, :]
    halo = jnp.where(i % blocks_per_seq == 0, 0.0, halo)
    row = lax.broadcasted_iota(jnp.int32, pre.shape, 0)
    prev = jnp.where(row == 0, halo, pltpu.roll(pre, 1, 0))
    xx = prev - pre
    mix = lambda m: (pre + xx * mu_ref[m:m + 1, :]).astype(BF16)
    xr_ref[...] = mix(0)
    xk_ref[...] = mix(2)
    xv_ref[...] = mix(3)
    wraw_ref[...] = w0_ref[...] + _dot(jnp.tanh(_dot(mix(1), w1_ref[...])).astype(BF16), w2_ref[...])
    a_ref[...] = _sigmoid(a0_ref[...] + _dot(_dot(mix(4), a1_ref[...]).astype(BF16), a2_ref[...]))
    gate_ref[...] = _dot(_sigmoid(_dot(mix(5), g1_ref[...])).astype(BF16), g2_ref[...])


def _rw_input(h, g, p, *, lp, tm):
    t, d = h.shape
    row_spec = pl.BlockSpec((tm, d), lambda i: (i, 0))
    const = lambda arr: pl.BlockSpec(arr.shape, lambda i: (0, 0), pipeline_mode=pl.Buffered(1))
    lora_w = [w for pair in (("w1", "w2"), ("a1", "a2"), ("g1", "g2")) for w in _pad_rank(p[pair[0]], p[pair[1]])]
    vec = lambda a: a.reshape(1, d)
    consts = [vec(g), p["mu"], *lora_w, vec(p["w0"]), vec(p["a0"])]
    return pl.pallas_call(
        functools.partial(_rw_input_kernel, blocks_per_seq=lp // tm),
        grid=(t // tm,),
        in_specs=[row_spec, pl.BlockSpec((8, d), lambda i: (jnp.maximum(i * (tm // 8) - 1, 0), 0))]
                 + [const(c) for c in consts],
        out_specs=[row_spec] * 6,
        out_shape=[jax.ShapeDtypeStruct((t, d), BF16)] * 3 + [jax.ShapeDtypeStruct((t, d), F32)] * 3,
        compiler_params=_cparams("parallel"), name="rw_input",
    )(h, h, *consts)


def _rwkv_kernel(r_ref, k_ref, v_ref, w_ref, a_ref, gate_ref, kk_ref, ka_ref, rk_ref, lg_ref, lb_ref,
                 o_ref, s_ref, *, groups):
    c_id = pl.program_id(2)

    @pl.when(c_id == 0)
    def _():
        s_ref[...] = jnp.zeros_like(s_ref)

    C, W = CHUNK, GROUP
    row = lax.broadcasted_iota(jnp.int32, (C, W), 0)
    lane = lax.broadcasted_iota(jnp.int32, (C, W), 1)
    lane_in = lane & (RW_HEAD - 1)
    lane_head = lane >> RW_HEAD_SHIFT
    tril_incl = lane_in <= row
    tril_strict = lane_in < row
    eye = jnp.where(lane_in == row, 1.0, 0.0)
    diag_block = (lane_in >> INV_BLOCK_SHIFT) == (row >> INV_BLOCK_SHIFT)
    bd_mask = (lax.broadcasted_iota(jnp.int32, (HEADS_PER_GROUP * C, W), 0) >> RW_HEAD_SHIFT) == \
              (lax.broadcasted_iota(jnp.int32, (HEADS_PER_GROUP * C, W), 1) >> RW_HEAD_SHIFT)
    ones_bd = jnp.where(bd_mask, 1.0, 0.0).astype(BF16)
    ones_bd2 = jnp.concatenate([ones_bd, ones_bd], axis=0)
    ltri = jnp.where(lax.broadcasted_iota(jnp.int32, (C, C), 1) <=
                     lax.broadcasted_iota(jnp.int32, (C, C), 0), 1.0, 0.0).astype(BF16)
    gs = range(groups)
    cols = lambda ref, g: ref[:, g * W:(g + 1) * W]

    def bd(x):
        return jnp.concatenate([x] * HEADS_PER_GROUP, axis=0) * ones_bd

    def seg_sum(x):
        return _dot(jnp.concatenate(_split(x), axis=1), ones_bd2)

    def head_mm(lhs, ys):
        top = jnp.concatenate([bd(ys[0]), bd(ys[1])], axis=1)
        res = _dot(jnp.concatenate([jnp.concatenate(x, axis=1) for x in lhs], axis=0),
                   jnp.concatenate([top, top], axis=0))
        return [res[i * C:(i + 1) * C, :W] + res[i * C:(i + 1) * C, W:] for i in range(len(lhs))]

    k = [cols(k_ref, g) for g in gs]
    v = [cols(v_ref, g).astype(BF16) for g in gs]
    a = [cols(a_ref, g) for g in gs]
    ld = [-DECAY_SCALE * _sigmoid(cols(w_ref, g)) for g in gs]
    kk = [k[g] * cols(kk_ref, g) for g in gs]
    kk = [kk[g] * lax.rsqrt(jnp.maximum(seg_sum(kk[g] * kk[g]), 1e-24)) for g in gs]
    kh = [k[g] * (1.0 + (a[g] - 1.0) * cols(ka_ref, g)) for g in gs]
    rb = [kk[g] * a[g] for g in gs]

    cum2 = [_dot(ltri, jnp.concatenate(_split(ld[g]), axis=1)) for g in gs]
    cum = [cum2[g][:, :W] + cum2[g][:, W:] for g in gs]
    c_last = [cum[g][C - 1:C, :] for g in gs]
    e_nc = [jnp.exp(-cum[g]) for g in gs]
    lhs = [jnp.concatenate([-kk[g] * jnp.exp(cum[g] - ld[g]), cols(r_ref, g) * jnp.exp(cum[g])],
                           axis=0).astype(BF16) for g in gs]
    b_hat = [(rb[g] * e_nc[g]).astype(BF16) for g in gs]
    k_hat = [(kh[g] * e_nc[g]).astype(BF16) for g in gs]
    gram = [_dot_nt(lhs[g], jnp.concatenate([bd(b_hat[g]), bd(k_hat[g])], axis=0)) for g in gs]
    n_ab = [jnp.where(tril_strict, gram[g][:C, :W], 0.0) for g in gs]
    b_rb = [jnp.where(tril_incl, gram[g][C:, :W], 0.0).astype(BF16) for g in gs]
    ak_rk = [jnp.concatenate([jnp.where(tril_strict, gram[g][:C, W:], 0.0),
                              jnp.where(tril_incl, gram[g][C:, W:], 0.0)], axis=0).astype(BF16) for g in gs]

    assert CHUNK == 4 << INV_BLOCK_SHIFT and INV_BLOCK_SHIFT == 4
    nd = [jnp.where(diag_block, n_ab[g], 0.0) for g in gs]
    no_s = [_split(n_ab[g] - nd[g]) for g in gs]
    pd = [eye + nd[g] for g in gs]
    x_s = [_split(nd[g]) for g in gs]
    x_s = [_split(head_mm([x_s[g]], x_s[g])[0]) for g in gs]
    for _ in range(2):
        prod = [head_mm([x_s[g], _split(pd[g])], x_s[g]) for g in gs]
        pd = [pd[g] + prod[g][1] for g in gs]
        x_s = [_split(prod[g][0]) for g in gs]
    pd = [pd[g] + head_mm([_split(pd[g])], x_s[g])[0] for g in gs]
    pd_s = [_split(pd[g]) for g in gs]
    m = [head_mm([pd_s[g]], no_s[g])[0] for g in gs]
    m_s = [_split(m[g]) for g in gs]
    m2_s = [_split(head_mm([m_s[g]], m_s[g])[0]) for g in gs]
    q = [eye + m[g] for g in gs]
    q_s = [_split(q[g] + head_mm([_split(q[g])], m2_s[g])[0]) for g in gs]
    p_s = [_split(head_mm([q_s[g]], pd_s[g])[0]) for g in gs]

    s_t = [s_ref[g] for g in gs]
    zz = [_dot(jnp.concatenate([lhs[g], ak_rk[g]], axis=1),
               jnp.concatenate([bd(s_t[g].astype(BF16)), bd(v[g])], axis=0)) for g in gs]
    sa = [head_mm([p_s[g]], _split(zz[g][:C]))[0].astype(BF16) for g in gs]
    y = [zz[g][C:] + _dot(b_rb[g], bd(sa[g])) for g in gs]

    yc = [y[g] - seg_sum(y[g]) * (1.0 / RW_HEAD) for g in gs]
    var = [seg_sum(yc[g] * yc[g]) * (1.0 / RW_HEAD) for g in gs]
    kh = [cols(k_ref, g) * (1.0 + (cols(a_ref, g) - 1.0) * cols(ka_ref, g)) for g in gs]
    bonus = [seg_sum(cols(r_ref, g) * kh[g] * cols(rk_ref, g)) * cols(v_ref, g) for g in gs]
    for g in gs:
        yn = yc[g] * lax.rsqrt(var[g] + RW_LNX_EPS) * cols(lg_ref, g) + cols(lb_ref, g)
        o_ref[:, g * W:(g + 1) * W] = ((yn + bonus[g]) * cols(gate_ref, g)).astype(BF16)

    full = [_dot_tn(jnp.concatenate([b_hat[g], k_hat[g]], axis=0), jnp.concatenate([sa[g], v[g]], axis=0))
            for g in gs]
    c_col = [seg_sum(eye * c_last[g]) for g in gs]
    for g in gs:
        gain = s_t[g]
        for h in range(HEADS_PER_GROUP):
            gain = gain + jnp.where(lane_head == h, full[g][h * C:(h + 1) * C, :], 0.0)
        s_ref[g] = jnp.exp(c_col[g]) * gain


def _rwkv_recurrence(r, k, v, w_raw, a, gate, k_k, k_a, r_k, lnx_g, lnx_b, *, batch, groups_per_step):
    t, d = r.shape
    chunks = t // (batch * CHUNK)
    gb = min(groups_per_step, d // GROUP)
    wb = gb * GROUP
    seq_spec = pl.BlockSpec((CHUNK, wb), lambda b, g, c: (b * chunks + c, g))
    par_spec = pl.BlockSpec((1, wb), lambda b, g, c: (0, g))
    par = lambda p: p.reshape(1, d)
    return pl.pallas_call(
        functools.partial(_rwkv_kernel, groups=gb),
        grid=(batch, d // wb, chunks),
        in_specs=[seq_spec] * 6 + [par_spec] * 5,
        out_specs=seq_spec,
        out_shape=jax.ShapeDtypeStruct((t, d), BF16),
        scratch_shapes=[pltpu.VMEM((gb, CHUNK, GROUP), F32)],
        compiler_params=_cparams("parallel", "parallel", "arbitrary"), name="rwkv_recurrence",
    )(r, k, v, w_raw, a, gate, par(k_k), par(k_a), par(r_k), par(lnx_g), par(lnx_b))


def _log_gate(x):
    return (jnp.minimum(x, 0.0) - jnp.log(1.0 + jnp.exp(-jnp.abs(x)))) * (1.0 / GLA_GATE_TAU)


def _post_norm_kernel(h_ref, m_ref, gp_ref, *rest):
    h = h_ref[...] + _rms(m_ref[...].astype(F32)) * gp_ref[...]
    if len(rest) == 1:
        rest[0][...] = h
        return
    gn_ref, *gate_refs, h_out, p_out = rest if len(rest) == 3 else rest[:-1]
    pre = (_rms(h) * gn_ref[...]).astype(BF16)
    h_out[...] = h
    p_out[...] = pre
    if gate_refs:
        a1_ref, a2_ref, ab_ref = gate_refs
        rest[-1][...] = _log_gate(_dot(_dot(pre, a1_ref[...]).astype(BF16), a2_ref[...]) + ab_ref[...])


def _post_norm_final(h, m, g_post, *, batch, lp, start, seq, tm):
    d = h.shape[1]
    align = BF16_SUBLANES
    assert lp % align == 0 and start % align == 0 and tm % align == 0
    in_spec = pl.BlockSpec((pl.Element(tm), pl.Element(d)),
                           lambda b, i: (pl.multiple_of(b * lp + start + i * tm, align), 0))
    return pl.pallas_call(
        _post_norm_kernel, grid=(batch, seq // tm),
        in_specs=[in_spec, in_spec, pl.BlockSpec((1, d), lambda b, i: (0, 0))],
        out_specs=pl.BlockSpec((None, tm, d), lambda b, i: (b, i, 0)),
        out_shape=jax.ShapeDtypeStruct((batch, seq, d), F32),
        compiler_params=_cparams("parallel", "parallel"), name="post_norm_final",
    )(h, m, g_post.reshape(1, d))


def _post_norm(h, m, g_post, g_next, *, tm, gate=None):
    t, d = h.shape
    row_spec = pl.BlockSpec((tm, d), lambda i: (i, 0))
    par_spec = pl.BlockSpec((1, d), lambda i: (0, 0))
    in_specs = [row_spec, row_spec, par_spec, par_spec]
    out_specs = [row_spec, row_spec]
    out_shape = [jax.ShapeDtypeStruct((t, d), F32), jax.ShapeDtypeStruct((t, d), BF16)]
    args = [h, m, g_post.reshape(1, d), g_next.reshape(1, d)]
    if gate is not None:
        a1, a2 = _pad_rank(gate[0], gate[1])
        dk = a2.shape[1]
        args += [a1, a2, gate[2].reshape(1, dk)]
        in_specs += [pl.BlockSpec(a.shape, lambda i: (0, 0)) for a in args[4:]]
        out_specs.append(pl.BlockSpec((tm, dk), lambda i: (i, 0)))
        out_shape.append(jax.ShapeDtypeStruct((t, dk), F32))
    return pl.pallas_call(
        _post_norm_kernel, grid=(t // tm,),
        in_specs=in_specs, out_specs=out_specs, out_shape=out_shape,
        compiler_params=_cparams("parallel"), name="post_norm",
    )(*args)


def _ffn_in_kernel(x_ref, halo_ref, wu_ref, wg_ref, cw_ref, cb_ref, wd_ref, o_ref, wd_out, *,
                   blocks_per_seq, rows, halo_rows):
    i = pl.program_id(0)
    tm = x_ref.shape[0]
    wd_out[...] = wd_ref[...].astype(BF16)
    wu = wu_ref[...].astype(BF16)
    wg = wg_ref[...].astype(BF16)
    zh = _dot(halo_ref[...], wg)
    zh = jnp.where(i % blocks_per_seq == 0, 0.0, zh)
    sub = _pick(rows, (352, 688, 96, 48, 16))
    prev1, prev2 = zh[halo_rows - 1:halo_rows, :], zh[halo_rows - 2:halo_rows - 1, :]
    row = lax.broadcasted_iota(jnp.int32, (sub, wu.shape[1]), 0)

    def project(s):
        xs = x_ref[s * sub:(s + 1) * sub, :]
        return _dot(xs, wu), _dot(xs, wg)

    cur = project(0)
    for s in range(rows // sub):
        nxt = project(s + 1) if (s + 1) * sub < rows else None
        u, z = cur
        z1 = jnp.where(row == 0, prev1, pltpu.roll(z, 1, 0))
        z2 = pltpu.roll(z, 2, 0)
        z2 = jnp.where(row == 1, prev1, z2)
        z2 = jnp.where(row == 0, prev2, z2)
        zc = z * cw_ref[2:3, :] + z1 * cw_ref[1:2, :] + z2 * cw_ref[0:1, :] + cb_ref[...]
        o_ref[s * sub:(s + 1) * sub, :] = (zc * _sigmoid(zc) * u).astype(BF16)
        prev1, prev2 = z[sub - 1:sub, :], z[sub - 2:sub - 1, :]
        cur = nxt
    if rows < tm:
        o_ref[rows:, :] = jnp.zeros((tm - rows, o_ref.shape[1]), o_ref.dtype)


def _ffn_in(x, w_up, w_gate, w_down, layer, conv_w, conv_b, *, lp, length, tm, tn):
    t, d = x.shape
    f = w_up.shape[2]
    hr = BF16_SUBLANES
    ni, nj = t // tm, f // tn
    slab = f // (ni * nj)
    assert slab * ni * nj == f and slab % hr == 0
    w_spec = pl.BlockSpec((None, d, tn), lambda i, j: (layer, 0, j))
    return pl.pallas_call(
        functools.partial(_ffn_in_kernel, blocks_per_seq=lp // tm, halo_rows=hr,
                          rows=length if tm == lp and length % hr == 0 else tm),
        grid=(ni, nj),
        in_specs=[pl.BlockSpec((tm, d), lambda i, j: (i, 0), pipeline_mode=pl.Buffered(1)),
                  pl.BlockSpec((hr, d), lambda i, j: (jnp.maximum(i * (tm // hr) - 1, 0), 0)),
                  w_spec, w_spec,
                  pl.BlockSpec((conv_w.shape[0], tn), lambda i, j: (0, j)),
                  pl.BlockSpec((1, tn), lambda i, j: (0, j)),
                  pl.BlockSpec((None, slab, d), lambda i, j: (layer, i * nj + j, 0))],
        out_specs=[pl.BlockSpec((tm, tn), lambda i, j: (i, j)),
                   pl.BlockSpec((slab, d), lambda i, j: (i * nj + j, 0))],
        out_shape=[jax.ShapeDtypeStruct((t, f), BF16), jax.ShapeDtypeStruct((f, d), BF16)],
        compiler_params=_cparams("parallel", "arbitrary"), name="ffn_in",
    )(x, x, w_up, w_gate, conv_w, conv_b.reshape(1, f), w_down)


GLA_LEVELS = 6


def _gla_select_matrices():
    t_i = np.arange(CHUNK)[:, None]
    u_i = np.arange(CHUNK)[None, :]
    mats = [u_i <= t_i]
    for l in range(GLA_LEVELS):
        half = 1 << l
        p = (t_i & ~(2 * half - 1)) + (half - 1)
        mats.append(np.where((t_i & half) != 0, (u_i > p) & (u_i <= t_i), (u_i > t_i) & (u_i <= p)))
    sel = np.concatenate(mats, axis=0).astype(np.float32)
    return jnp.asarray(np.concatenate([sel, sel], axis=1), dtype=BF16)


def _gla_kernel(sel_ref, q_ref, k_ref, v_ref, g_ref, r_ref, rb_ref, ng_ref, o_ref, s_ref, *, heads, hk, hv):
    c_id = pl.program_id(2)

    @pl.when(c_id == 0)
    def _():
        s_ref[...] = jnp.zeros_like(s_ref)

    C = CHUNK
    t_i = lax.broadcasted_iota(jnp.int32, (C, C), 0)
    u_i = lax.broadcasted_iota(jnp.int32, (C, C), 1)
    row = lax.broadcasted_iota(jnp.int32, (C, 1), 0)
    eye = t_i == u_i
    hs = range(heads)
    sel = sel_ref[...]

    q = [q_ref[:, h * hk:(h + 1) * hk] * (hk ** -0.5) for h in hs]
    k = [k_ref[:, h * hk:(h + 1) * hk] for h in hs]
    v = [v_ref[:, h * hv:(h + 1) * hv].astype(BF16) for h in hs]
    g_s = [_split(g_ref[:, h * hk:(h + 1) * hk]) for h in hs]
    cums = [_dot(sel, jnp.concatenate(g_s[h], axis=0)) for h in hs]
    b = [cums[h][:C] for h in hs]
    b_last = [b[h][C - 1:C, :] for h in hs]

    a = [jnp.where(eye, jnp.sum(q[h] * k[h], axis=-1, keepdims=True), 0.0) for h in hs]
    for l in range(GLA_LEVELS):
        upper = (row & (1 << l)) != 0
        same_block = (t_i >> (l + 1)) == (u_i >> (l + 1))
        qk = [jnp.where(upper, q[h], k[h]) * jnp.exp(cums[h][(1 + l) * C:(2 + l) * C]) for h in hs]
        ql = [jnp.where(upper, qk[h], 0.0).astype(BF16) for h in hs]
        kl = [jnp.where(upper, 0.0, qk[h]).astype(BF16) for h in hs]
        a = [a[h] + jnp.where(same_block, _dot_nt(ql[h], kl[h]), 0.0) for h in hs]

    s_t = [s_ref[h] for h in hs]
    qe = [(q[h] * jnp.exp(b[h])).astype(BF16) for h in hs]
    kr = [(k[h] * jnp.exp(b_last[h] - b[h])).astype(BF16) for h in hs]
    o = [_dot_nt(qe[h], s_t[h].astype(BF16)) + _dot(a[h].astype(BF16), v[h]) for h in hs]
    for h in hs:
        s_ref[h] = s_t[h] * jnp.exp(b_last[h]) + _dot_tn(v[h], kr[h])
    for h in hs:
        zg = r_ref[:, h * hv:(h + 1) * hv] + rb_ref[:, h * hv:(h + 1) * hv]
        o_ref[:, h * hv:(h + 1) * hv] = (_rms(o[h]) * ng_ref[...] * (zg * _sigmoid(zg))).astype(BF16)


def _gla_recurrence(proj, g, r_b, norm_g, *, batch, heads, hk, hv, heads_per_step):
    t = proj.shape[0]
    hb = heads_per_step
    chunks = t // (batch * CHUNK)
    dk, dv = heads * hk, heads * hv
    sel = _gla_select_matrices()

    def cols(width, start):
        assert start % (hb * width) == 0
        off = start // (hb * width)
        return pl.BlockSpec((CHUNK, hb * width), lambda b, h, c: (b * chunks + c, h + off))

    return pl.pallas_call(
        functools.partial(_gla_kernel, heads=hb, hk=hk, hv=hv),
        grid=(batch, heads // hb, chunks),
        in_specs=[pl.BlockSpec(sel.shape, lambda b, h, c: (0, 0)),
                  cols(hk, 0), cols(hk, dk), cols(hv, 2 * dk),
                  pl.BlockSpec((CHUNK, hb * hk), lambda b, h, c: (b * chunks + c, h)),
                  cols(hv, 2 * dk + dv),
                  pl.BlockSpec((1, hb * hv), lambda b, h, c: (0, h)),
                  pl.BlockSpec((1, hv), lambda b, h, c: (0, 0))],
        out_specs=pl.BlockSpec((CHUNK, hb * hv), lambda b, h, c: (b * chunks + c, h)),
        out_shape=jax.ShapeDtypeStruct((t, dv), BF16),
        scratch_shapes=[pltpu.VMEM((hb, hv, hk), F32)],
        compiler_params=_cparams("parallel", "parallel", "arbitrary"), name="gla_recurrence",
    )(sel, proj, proj, proj, g, proj, r_b.reshape(1, dv), norm_g.reshape(1, hv))


def _tiles(lp, length):
    return dict(
        lp=lp, length=length,
        ew=_pick(lp, (192, 128, 64)),
        norm=_pick(lp, (352, 192, 128, 64)),
        mm=_pick(lp, (1056, 704, 352, 192, 128, 64)),
        down=_pick(lp, (704, 1056, 352, 192, 128, 64)),
        ffn=_pick(lp, (2112, 1056, 704, 352, 192, 128, 64)),
        rw_groups=32,
        gla_heads=8,
    )


def _col_tile(n, cands=(512, 256, 128)):
    return _pick(n, cands)


def _rwkv_layer(h, g_pre, p, *, batch, tiles):
    d = h.shape[1]
    xr, xk, xv, w_raw, a, gate = _rw_input(h, g_pre, p, lp=tiles["lp"], tm=tiles["ew"])
    mm = functools.partial(_matmul_ws, tm=tiles["mm"], tn=_col_tile(d))
    r = mm(xr, p["wr"], name="rw_r")
    k = mm(xk, p["wk"], name="rw_k")
    v = mm(xv, p["wv"], name="rw_v")
    yo = _rwkv_recurrence(r, k, v, w_raw, a, gate, p["k_k"], p["k_a"], p["r_k"].reshape(-1),
                          p["lnx_g"], p["lnx_b"], batch=batch, groups_per_step=tiles["rw_groups"])
    return mm(yo, p["wo"], out_dtype=BF16, name="rw_o")


def _gla_layer(pre, g, p, *, batch, tiles):
    d = pre.shape[1]
    dk = p["a2"].shape[1]
    dv = p["wo"].shape[0]
    heads = max(4, d // 512)
    hk, hv = dk // heads, dv // heads
    mm = functools.partial(_matmul_ws, tm=tiles["mm"])
    whole = tiles["ffn"] == tiles["lp"] and tiles["length"] % BF16_SUBLANES == 0
    proj = _matmul(pre, p["w_in"], tm=tiles["ffn"] if whole else tiles["mm"], tn=_col_tile(dk),
                   rows=tiles["length"] if whole else None, name="gla_in")
    og = _gla_recurrence(proj, g, p["r_b"], p["norm_g"], batch=batch, heads=heads, hk=hk, hv=hv,
                         heads_per_step=min(tiles["gla_heads"], heads))
    return mm(og, p["wo"], tn=_col_tile(d), out_dtype=BF16, name="gla_o")


def _ffn(pre, w_up, w_gate, w_down, layer, conv_w, conv_b, *, tiles):
    f = w_up.shape[2]
    act, w_down_bf = _ffn_in(pre, w_up, w_gate, w_down, layer, conv_w, conv_b, lp=tiles["lp"],
                             length=tiles["length"], tm=tiles["ffn"], tn=_col_tile(f, (256, 128)))
    return _matmul(act, w_down_bf, tm=tiles["down"], tn=_col_tile(w_down.shape[2], (512, 256, 128)),
                   out_dtype=BF16, name="ffn_down")


def kernel(x, meta, norm_g, rw_mu, rw_w0, rw_w1, rw_w2, rw_a0, rw_a1, rw_a2, rw_g1, rw_g2, rw_k_k, rw_k_a, rw_r_k, rw_wr, rw_wk, rw_wv, rw_wo, rw_lnx_g, rw_lnx_b, gla_w_in, gla_a1, gla_a2, gla_a_b, gla_r_b, gla_norm_g, gla_wo, ffn_up, ffn_gate, ffn_conv, ffn_conv_b, ffn_down):
    batch, seq, d = x.shape
    n_meta = meta.shape[0]
    depth = norm_g.shape[0]
    length = n_meta + seq
    lp = -(-length // CHUNK) * CHUNK
    m = jnp.broadcast_to(meta.astype(x.dtype)[None], (batch, n_meta, d))
    h = jnp.concatenate([m, x, jnp.zeros((batch, lp - length, d), x.dtype)], axis=1).reshape(batch * lp, d)
    tiles = _tiles(lp, length)

    pre = gates = None
    for i in range(depth):
        j = i // 2
        if i % 2 == 0:
            p = dict(mu=rw_mu[j], w0=rw_w0[j], w1=rw_w1[j], w2=rw_w2[j], a0=rw_a0[j], a1=rw_a1[j],
                     a2=rw_a2[j], g1=rw_g1[j], g2=rw_g2[j], k_k=rw_k_k[j], k_a=rw_k_a[j], r_k=rw_r_k[j],
                     wr=rw_wr[j], wk=rw_wk[j], wv=rw_wv[j], wo=rw_wo[j], lnx_g=rw_lnx_g[j],
                     lnx_b=rw_lnx_b[j])
            mix = _rwkv_layer(h, norm_g[i, 0], p, batch=batch, tiles=tiles)
        else:
            p = dict(w_in=gla_w_in[j], a1=gla_a1[j], a2=gla_a2[j], a_b=gla_a_b[j], r_b=gla_r_b[j],
                     norm_g=gla_norm_g[j], wo=gla_wo[j])
            mix = _gla_layer(pre, gates, p, batch=batch, tiles=tiles)
        h, pre = _post_norm(h, mix, norm_g[i, 1], norm_g[i, 2], tm=tiles["norm"])
        f = _ffn(pre, ffn_up, ffn_gate, ffn_down, i, ffn_conv[i], ffn_conv_b[i], tiles=tiles)
        if i + 1 < depth and (i + 1) % 2 == 1:
            jn = (i + 1) // 2
            h, pre, gates = _post_norm(h, f, norm_g[i, 3], norm_g[i + 1, 0], tm=tiles["norm"],
                                       gate=(gla_a1[jn], gla_a2[jn], gla_a_b[jn]))
        elif i + 1 < depth:
            h, pre = _post_norm(h, f, norm_g[i, 3], norm_g[i + 1, 0], tm=tiles["norm"])
    return _post_norm_final(h, f, norm_g[depth - 1, 3], batch=batch, lp=lp, start=n_meta, seq=seq,
                            tm=_pick(seq, (256, 128, 64)))
```
